```python
import math
import jax, jax.numpy as jnp
from jax import lax
import numpy as np

D_MODEL = 2048
BATCH = 4
SEQ = 4096
DEPTH = 2
DEC_BATCH = 16
DEC_SEQ = 64
PAST_LEN = 2048

CHUNK = 64
N_MIXERS = 2
N_A_LAYERS = (DEPTH + 1) // 2
N_B_LAYERS = DEPTH // 2
A_HEADS = 16
A_HEAD_DIM = D_MODEL // A_HEADS
BAND_CHUNKS = 8
BAND_ROWS = BAND_CHUNKS * CHUNK
REL_CLIP = 4 * CHUNK
MLA_HEADS = 16
MLA_Q_RANK = D_MODEL // 4
MLA_KV_RANK = D_MODEL // 4
MLA_NOPE_DIM = 128
MLA_ROPE_DIM = 64
MLA_V_DIM = 128
ROPE_THETA = 10000.0
Q_BLOCK = 128
D_FF = 4 * D_MODEL
NORM_EPS = 1e-6
NEG_INF = -1e30

kernel_name = "hybrid_streaming_band_mla_step"


def rmsnorm(x, g):
    xf = x.astype(jnp.float32)
    y = xf * lax.rsqrt(jnp.mean(xf * xf, axis=-1, keepdims=True) + NORM_EPS)
    return (y * g.astype(jnp.float32)).astype(x.dtype)


def rope(x, pos):
    half = x.shape[-1] // 2
    inv = 1.0 / (ROPE_THETA ** (jnp.arange(half, dtype=jnp.float32) / half))
    ang = pos.astype(jnp.float32)[:, None] * inv[None, :]
    shp = (pos.shape[0],) + (1,) * (x.ndim - 3) + (half,)
    cos = jnp.cos(ang).reshape(shp)
    sin = jnp.sin(ang).reshape(shp)
    xf = x.astype(jnp.float32)
    x1, x2 = xf[..., :half], xf[..., half:]
    return jnp.concatenate([x1 * cos - x2 * sin, x2 * cos + x1 * sin], axis=-1).astype(x.dtype)


def masked_softmax(scores, mask):
    return jax.nn.softmax(jnp.where(mask, scores, NEG_INF), axis=-1)


def a_project(h, w_qkv):
    b, s, _ = h.shape
    qkv = (h @ w_qkv).reshape(b, s, 3, A_HEADS, A_HEAD_DIM)
    return qkv[:, :, 0], qkv[:, :, 1], qkv[:, :, 2]


def band_attend(q, k, v, q_pos, k_pos, rel_bias):
    scores = jnp.einsum('bqhd,bkhd->bhqk', q, k).astype(jnp.float32) * (A_HEAD_DIM ** -0.5)
    rel = jnp.clip(q_pos[:, None] - k_pos[None, :], -REL_CLIP, REL_CLIP) + REL_CLIP
    scores = scores + rel_bias[:, rel].astype(jnp.float32)[None]
    qc = q_pos[:, None] // CHUNK
    kc = k_pos[None, :] // CHUNK
    mask = (k_pos[None, :] >= 0) & (kc <= qc) & (kc >= qc - BAND_CHUNKS)
    p = masked_softmax(scores, mask[None, None])
    return jnp.einsum('bhqk,bkhd->bqhd', p.astype(v.dtype), v)


def mixer_a_prompt(h, w_qkv, w_o, rel_bias):
    b, s, _ = h.shape
    q, k, v = a_project(h, w_qkv)
    pad = ((0, 0), (BAND_ROWS, 0), (0, 0), (0, 0))
    kpad, vpad = jnp.pad(k, pad), jnp.pad(v, pad)
    band = BAND_ROWS + CHUNK

    def one_chunk(c):
        start = c * CHUNK
        q_c = lax.dynamic_slice_in_dim(q, start, CHUNK, axis=1)
        k_b = lax.dynamic_slice_in_dim(kpad, start, band, axis=1)
        v_b = lax.dynamic_slice_in_dim(vpad, start, band, axis=1)
        q_pos = start + jnp.arange(CHUNK)
        k_pos = start - BAND_ROWS + jnp.arange(band)
        return band_attend(q_c, k_b, v_b, q_pos, k_pos, rel_bias)

    o = lax.map(one_chunk, jnp.arange(s // CHUNK))
    o = jnp.moveaxis(o, 0, 1).reshape(b, s, A_HEADS * A_HEAD_DIM)
    keep = min(BAND_ROWS, s)
    return o @ w_o, k[:, s - keep:], v[:, s - keep:]


def mixer_a_sample(h, cache_k, cache_v, w_qkv, w_o, rel_bias):
    b, s, _ = h.shape
    q, k, v = a_project(h, w_qkv)
    n_cache = cache_k.shape[1]
    kk = jnp.concatenate([cache_k.astype(k.dtype), k], axis=1)
    vv = jnp.concatenate([cache_v.astype(v.dtype), v], axis=1)
    q_pos = PAST_LEN + jnp.arange(s)
    k_pos = jnp.concatenate([PAST_LEN - n_cache + jnp.arange(n_cache), q_pos])
    o = band_attend(q, kk, vv, q_pos, k_pos, rel_bias).reshape(b, s, A_HEADS * A_HEAD_DIM)
    return o @ w_o, k, v


def mla_project(h, pos, w_dq, q_norm, w_uq, w_dkv, kv_norm, w_uk):
    cq = rmsnorm(h @ w_dq, q_norm)
    q = jnp.einsum('bsr,rhe->bshe', cq, w_uq)
    q_nope, q_rope = q[..., :MLA_NOPE_DIM], rope(q[..., MLA_NOPE_DIM:], pos)
    q_lat = jnp.einsum('bshn,chn->bshc', q_nope, w_uk)
    dkv = h @ w_dkv
    ckv = rmsnorm(dkv[..., :MLA_KV_RANK], kv_norm)
    kr = rope(dkv[..., MLA_KV_RANK:], pos)
    return q_lat, q_rope, ckv, kr


def mla_attend(q_lat, q_rope, ckv, kr, q_pos, k_pos, w_uv):
    s = jnp.einsum('bqhc,bkc->bhqk', q_lat, ckv) + jnp.einsum('bqhr,bkr->bhqk', q_rope, kr)
    s = s.astype(jnp.float32) * ((MLA_NOPE_DIM + MLA_ROPE_DIM) ** -0.5)
    mask = (k_pos[None, :] // CHUNK) <= (q_pos[:, None] // CHUNK)
    p = masked_softmax(s, mask[None, None])
    o_lat = jnp.einsum('bhqk,bkc->bqhc', p.astype(ckv.dtype), ckv)
    return jnp.einsum('bqhc,chv->bqhv', o_lat, w_uv)


def mixer_b_prompt(h, w_dq, q_norm, w_uq, w_dkv, kv_norm, w_uk, w_uv, w_o):
    b, s, _ = h.shape
    pos = jnp.arange(s)
    q_lat, q_rope, ckv, kr = mla_project(h, pos, w_dq, q_norm, w_uq, w_dkv, kv_norm, w_uk)

    def one_block(n):
        start = n * Q_BLOCK
        ql = lax.dynamic_slice_in_dim(q_lat, start, Q_BLOCK, axis=1)
        qr = lax.dynamic_slice_in_dim(q_rope, start, Q_BLOCK, axis=1)
        return mla_attend(ql, qr, ckv, kr, start + jnp.arange(Q_BLOCK), pos, w_uv)

    o = lax.map(one_block, jnp.arange(s // Q_BLOCK))
    o = jnp.moveaxis(o, 0, 1).reshape(b, s, MLA_HEADS * MLA_V_DIM)
    return o @ w_o, ckv, kr


def mixer_b_sample(h, cache_ckv, cache_kr, w_dq, q_norm, w_uq, w_dkv, kv_norm, w_uk, w_uv, w_o):
    b, s, _ = h.shape
    q_pos = PAST_LEN + jnp.arange(s)
    q_lat, q_rope, ckv, kr = mla_project(h, q_pos, w_dq, q_norm, w_uq, w_dkv, kv_norm, w_uk)
    n_cache = cache_ckv.shape[1]
    ckv_all = jnp.concatenate([cache_ckv.astype(ckv.dtype), ckv], axis=1)
    kr_all = jnp.concatenate([cache_kr.astype(kr.dtype), kr], axis=1)
    k_pos = jnp.concatenate([PAST_LEN - n_cache + jnp.arange(n_cache), q_pos])
    o = mla_attend(q_lat, q_rope, ckv_all, kr_all, q_pos, k_pos, w_uv).reshape(b, s, MLA_HEADS * MLA_V_DIM)
    return o @ w_o, ckv, kr


def sq_relu_mlp(h, w1, w2):
    return jnp.square(jax.nn.relu(h @ w1)) @ w2


def setup_inputs(seed: int = 0) -> dict:
    key = jax.random.key(seed)
    ks = jax.random.split(key, 24)
    f32 = jnp.float32

    def nrm(k, shape, fan_in):
        return jax.random.normal(k, shape, f32) * (fan_in ** -0.5)

    def gain(k, shape):
        return 1.0 + 0.01 * jax.random.normal(k, shape, f32)

    a_win = min(BAND_ROWS, PAST_LEN)
    hd_a = A_HEADS * A_HEAD_DIM
    return {
        "x_prompt": jax.random.normal(ks[0], (BATCH, SEQ, D_MODEL), f32),
        "x_sample": jax.random.normal(ks[1], (DEC_BATCH, DEC_SEQ, D_MODEL), f32),
        "cache_a_k": jax.random.normal(ks[2], (N_A_LAYERS, DEC_BATCH, a_win, A_HEADS, A_HEAD_DIM), f32),
        "cache_a_v": jax.random.normal(ks[3], (N_A_LAYERS, DEC_BATCH, a_win, A_HEADS, A_HEAD_DIM), f32),
        "cache_mla_ckv": jax.random.normal(ks[4], (N_B_LAYERS, DEC_BATCH, PAST_LEN, MLA_KV_RANK), f32),
        "cache_mla_kr": jax.random.normal(ks[5], (N_B_LAYERS, DEC_BATCH, PAST_LEN, MLA_ROPE_DIM), f32),
        "ln_mix_pre": gain(ks[6], (DEPTH, D_MODEL)),
        "ln_mix_post": gain(ks[7], (DEPTH, D_MODEL)),
        "ln_ffn_pre": gain(ks[8], (DEPTH, D_MODEL)),
        "ln_ffn_post": gain(ks[9], (DEPTH, D_MODEL)),
        "a_w_qkv": nrm(ks[10], (N_A_LAYERS, D_MODEL, 3 * hd_a), D_MODEL),
        "a_w_o": nrm(ks[11], (N_A_LAYERS, hd_a, D_MODEL), hd_a),
        "a_rel_bias": 0.1 * jax.random.normal(ks[12], (N_A_LAYERS, A_HEADS, 2 * REL_CLIP + 1), f32),
        "mla_w_dq": nrm(ks[13], (N_B_LAYERS, D_MODEL, MLA_Q_RANK), D_MODEL),
        "mla_q_norm": gain(ks[14], (N_B_LAYERS, MLA_Q_RANK)),
        "mla_w_uq": nrm(ks[15], (N_B_LAYERS, MLA_Q_RANK, MLA_HEADS, MLA_NOPE_DIM + MLA_ROPE_DIM), MLA_Q_RANK),
        "mla_w_dkv": nrm(ks[16], (N_B_LAYERS, D_MODEL, MLA_KV_RANK + MLA_ROPE_DIM), D_MODEL),
        "mla_kv_norm": gain(ks[17], (N_B_LAYERS, MLA_KV_RANK)),
        "mla_w_uk": nrm(ks[18], (N_B_LAYERS, MLA_KV_RANK, MLA_HEADS, MLA_NOPE_DIM), MLA_KV_RANK),
        "mla_w_uv": nrm(ks[19], (N_B_LAYERS, MLA_KV_RANK, MLA_HEADS, MLA_V_DIM), MLA_KV_RANK),
        "mla_w_o": nrm(ks[20], (N_B_LAYERS, MLA_HEADS * MLA_V_DIM, D_MODEL), MLA_HEADS * MLA_V_DIM),
        "ffn_w1": nrm(ks[21], (DEPTH, D_MODEL, D_FF), D_MODEL),
        "ffn_w2": nrm(ks[22], (DEPTH, D_FF, D_MODEL), D_FF),
    }


def reference(x_prompt, x_sample, cache_a_k, cache_a_v, cache_mla_ckv, cache_mla_kr,
              ln_mix_pre, ln_mix_post, ln_ffn_pre, ln_ffn_post,
              a_w_qkv, a_w_o, a_rel_bias,
              mla_w_dq, mla_q_norm, mla_w_uq, mla_w_dkv, mla_kv_norm, mla_w_uk, mla_w_uv, mla_w_o,
              ffn_w1, ffn_w2):
    xp, xs = x_prompt, x_sample
    a_k_p, a_v_p, a_k_s, a_v_s = [], [], [], []
    b_c_p, b_r_p, b_c_s, b_r_s = [], [], [], []
    for i in range(DEPTH):
        j = i // N_MIXERS
        hp = rmsnorm(xp, ln_mix_pre[i])
        hs = rmsnorm(xs, ln_mix_pre[i])
        if i % N_MIXERS == 0:
            op, kp, vp = mixer_a_prompt(hp, a_w_qkv[j], a_w_o[j], a_rel_bias[j])
            os_, ks_, vs_ = mixer_a_sample(hs, cache_a_k[j], cache_a_v[j], a_w_qkv[j], a_w_o[j], a_rel_bias[j])
            a_k_p.append(kp); a_v_p.append(vp); a_k_s.append(ks_); a_v_s.append(vs_)
        else:
            op, cp, rp = mixer_b_prompt(hp, mla_w_dq[j], mla_q_norm[j], mla_w_uq[j], mla_w_dkv[j],
                                        mla_kv_norm[j], mla_w_uk[j], mla_w_uv[j], mla_w_o[j])
            os_, cs_, rs_ = mixer_b_sample(hs, cache_mla_ckv[j], cache_mla_kr[j], mla_w_dq[j], mla_q_norm[j],
                                           mla_w_uq[j], mla_w_dkv[j], mla_kv_norm[j], mla_w_uk[j],
                                           mla_w_uv[j], mla_w_o[j])
            b_c_p.append(cp); b_r_p.append(rp); b_c_s.append(cs_); b_r_s.append(rs_)
        xp = xp + rmsnorm(op, ln_mix_post[i])
        xs = xs + rmsnorm(os_, ln_mix_post[i])
        xp = xp + rmsnorm(sq_relu_mlp(rmsnorm(xp, ln_ffn_pre[i]), ffn_w1[i], ffn_w2[i]), ln_ffn_post[i])
        xs = xs + rmsnorm(sq_relu_mlp(rmsnorm(xs, ln_ffn_pre[i]), ffn_w1[i], ffn_w2[i]), ln_ffn_post[i])
    y_prompt, y_sample = xp, xs
    new_a_k_prompt = jnp.stack(a_k_p, axis=0)
    new_a_v_prompt = jnp.stack(a_v_p, axis=0)
    new_a_k_sample = jnp.stack(a_k_s, axis=0)
    new_a_v_sample = jnp.stack(a_v_s, axis=0)
    new_mla_ckv_prompt = jnp.stack(b_c_p, axis=0)
    new_mla_kr_prompt = jnp.stack(b_r_p, axis=0)
    new_mla_ckv_sample = jnp.stack(b_c_s, axis=0)
    new_mla_kr_sample = jnp.stack(b_r_s, axis=0)
    return (y_prompt, y_sample, new_a_k_prompt, new_a_v_prompt, new_a_k_sample, new_a_v_sample,
            new_mla_ckv_prompt, new_mla_kr_prompt, new_mla_ckv_sample, new_mla_kr_sample)
```

```python
import functools
import math

import jax
import jax.numpy as jnp
from jax import lax
from jax.experimental import pallas as pl
from jax.experimental.pallas import tpu as pltpu

F32 = jnp.float32
BF16 = jnp.bfloat16

D_MODEL = 2048
BATCH = 4
SEQ = 4096
DEC_BATCH = 16
DEC_SEQ = 64
PAST_LEN = 2048
CHUNK = 64
HEADS = 16
HEAD_DIM = 128
BAND_ROWS = 512
REL_CLIP = 256
MLA_RANK = 512
MLA_NOPE = 128
MLA_ROPE = 64
MLA_QK = 256
ROPE_THETA = 10000.0
D_FF = 8192
EPS = 1e-6
NEG = -1e30

T_PROMPT = BATCH * SEQ
T_SAMPLE = DEC_BATCH * DEC_SEQ
T_ALL = T_PROMPT + T_SAMPLE

VMEM_LIMIT_BYTES = 58 * 1024 * 1024

BAND_TQ = 256
BAND_TK = BAND_ROWS + BAND_TQ
BAND_F = 1024

MLA_TQ = 512
MLA_TK = 512

_NT = (((1,), (1,)), ((), ()))


def _params(*sem):
    return pltpu.CompilerParams(dimension_semantics=sem, vmem_limit_bytes=VMEM_LIMIT_BYTES)


def _rms(x, g):
    ms = jnp.mean(x * x, axis=-1, keepdims=True)
    return x * lax.rsqrt(ms + EPS) * g


def _norm_matmul_kernel(x_ref, g_ref, w_ref, o_ref, h_ref):
    @pl.when(pl.program_id(1) == 0)
    def _():
        h_ref[...] = _rms(x_ref[...], g_ref[...]).astype(BF16)

    o_ref[...] = jnp.dot(h_ref[...], w_ref[...], preferred_element_type=F32).astype(o_ref.dtype)


def norm_matmul(x, g, w, *, tm, tn, out_dtype):
    m, k = x.shape
    n = w.shape[1]
    return pl.pallas_call(
        _norm_matmul_kernel,
        grid=(m // tm, n // tn),
        in_specs=[
            pl.BlockSpec((tm, k), lambda i, j: (i, 0)),
            pl.BlockSpec((1, k), lambda i, j: (0, 0)),
            pl.BlockSpec((k, tn), lambda i, j: (0, j)),
        ],
        out_specs=pl.BlockSpec((tm, tn), lambda i, j: (i, j)),
        out_shape=jax.ShapeDtypeStruct((m, n), out_dtype),
        scratch_shapes=[pltpu.VMEM((tm, k), BF16)],
        compiler_params=_params("parallel", "arbitrary"),
        name="norm_matmul",
    )(x, g.reshape(1, k), w)


QKV_TN = 1024
QKV_HEADS_PER_BLOCK = QKV_TN // HEAD_DIM


def _norm_qkv_kernel(x_ref, g_ref, w_ref, hm_ref, kv_ref, h_ref):
    j = pl.program_id(1)

    @pl.when(j == 0)
    def _():
        h_ref[...] = _rms(x_ref[...], g_ref[...]).astype(BF16)

    acc = jnp.dot(h_ref[...], w_ref[...], preferred_element_type=F32)

    @pl.when(j < 4)
    def _():
        kv_ref[...] = acc
        for h in range(QKV_HEADS_PER_BLOCK):
            hm_ref[h] = acc[:, h * HEAD_DIM:(h + 1) * HEAD_DIM].astype(BF16)

    @pl.when(j >= 4)
    def _():
        scaled = acc * (HEAD_DIM ** -0.5)
        for h in range(QKV_HEADS_PER_BLOCK):
            hm_ref[h] = scaled[:, h * HEAD_DIM:(h + 1) * HEAD_DIM].astype(BF16)


def norm_qkv(x, g, w, *, tm):
    m, k = x.shape
    nb = 3 * D_MODEL // QKV_TN

    def col(j):
        return (j + 2) % nb

    return pl.pallas_call(
        _norm_qkv_kernel,
        grid=(m // tm, nb),
        in_specs=[
            pl.BlockSpec((tm, k), lambda i, j: (i, 0)),
            pl.BlockSpec((1, k), lambda i, j: (0, 0)),
            pl.BlockSpec((k, QKV_TN), lambda i, j: (0, col(j))),
        ],
        out_specs=[
            pl.BlockSpec((QKV_HEADS_PER_BLOCK, tm, HEAD_DIM), lambda i, j: (col(j), i, 0)),
            pl.BlockSpec((tm, QKV_TN), lambda i, j: (i, jnp.minimum(j, 3))),
        ],
        out_shape=[
            jax.ShapeDtypeStruct((3 * HEADS, m, HEAD_DIM), BF16),
            jax.ShapeDtypeStruct((m, 2 * D_MODEL), F32),
        ],
        scratch_shapes=[pltpu.VMEM((tm, k), BF16)],
        compiler_params=_params("parallel", "arbitrary"),
        name="norm_qkv",
    )(x, g.reshape(1, k), w)


def _toeplitz_bias(f_row, rows):
    x = jnp.broadcast_to(f_row, (rows, BAND_F))
    return pltpu.roll(x, 0, 1, stride=1, stride_axis=0)


def _band_prompt_kernel(q_ref, k_ref, v_ref, f_ref, o_ref, kpad_ref, vpad_ref, bias_ref):
    r = lax.broadcasted_iota(jnp.int32, (BAND_TQ, BAND_TK), 0) // CHUNK
    j = lax.broadcasted_iota(jnp.int32, (BAND_TQ, BAND_TK), 1)
    jc = j // CHUNK
    allowed = (jc >= r) & (jc <= r + BAND_ROWS // CHUNK)
    bias_ref[...] = jnp.where(allowed, _toeplitz_bias(f_ref[0], BAND_TQ)[:, :BAND_TK], NEG)

    kpad_ref[0:BAND_ROWS, :] = jnp.zeros((BAND_ROWS, HEAD_DIM), BF16)
    vpad_ref[0:BAND_ROWS, :] = jnp.zeros((BAND_ROWS, HEAD_DIM), BF16)
    kpad_ref[BAND_ROWS:, :] = k_ref[0]
    vpad_ref[BAND_ROWS:, :] = v_ref[0]

    def block(qi, carry):
        start = pl.multiple_of(qi * BAND_TQ, BAND_TQ)
        q = q_ref[0, pl.ds(start, BAND_TQ), :]
        kw = kpad_ref[pl.ds(start, BAND_TK), :]
        vw = vpad_ref[pl.ds(start, BAND_TK), :]
        s = lax.dot_general(q, kw, _NT, preferred_element_type=F32) + bias_ref[...]
        s = jnp.where(j >= BAND_ROWS - start, s, NEG)
        m = jnp.max(s, axis=-1, keepdims=True)
        p = jnp.exp(s - m)
        l = jnp.sum(p, axis=-1, keepdims=True)
        o = jnp.dot(p.astype(BF16), vw, preferred_element_type=F32) / l
        o_ref[pl.ds(start, BAND_TQ), :] = o.astype(BF16)
        return carry

    lax.fori_loop(0, SEQ // BAND_TQ, block, 0)


def band_prompt(qkv_hm, f_rows):
    return pl.pallas_call(
        _band_prompt_kernel,
        grid=(BATCH, HEADS),
        in_specs=[
            pl.BlockSpec((1, SEQ, HEAD_DIM), lambda b, h: (h, b, 0)),
            pl.BlockSpec((1, SEQ, HEAD_DIM), lambda b, h: (HEADS + h, b, 0)),
            pl.BlockSpec((1, SEQ, HEAD_DIM), lambda b, h: (2 * HEADS + h, b, 0)),
            pl.BlockSpec((1, 1, BAND_F), lambda b, h: (h, 0, 0)),
        ],
        out_specs=pl.BlockSpec((SEQ, HEAD_DIM), lambda b, h: (b, h)),
        out_shape=jax.ShapeDtypeStruct((T_ALL, D_MODEL), BF16),
        scratch_shapes=[
            pltpu.VMEM((BAND_ROWS + SEQ, HEAD_DIM), BF16),
            pltpu.VMEM((BAND_ROWS + SEQ, HEAD_DIM), BF16),
            pltpu.VMEM((BAND_TQ, BAND_TK), F32),
        ],
        compiler_params=_params("parallel", "parallel"),
        name="band_prompt",
    )(qkv_hm, qkv_hm, qkv_hm, f_rows)


def _band_sample_kernel(q_ref, kn_ref, vn_ref, ck_ref, cv_ref, f_ref, o_in_ref, o_ref):
    del o_in_ref
    for h in range(HEADS):
        cols = slice(h * HEAD_DIM, (h + 1) * HEAD_DIM)
        bias = _toeplitz_bias(f_ref[h], DEC_SEQ)
        q = q_ref[h]
        kc = ck_ref[0, :, cols].astype(BF16)
        vc = cv_ref[0, :, cols].astype(BF16)
        s_c = lax.dot_general(q, kc, _NT, preferred_element_type=F32) + bias[:, :BAND_ROWS]
        s_n = (lax.dot_general(q, kn_ref[h], _NT, preferred_element_type=F32)
               + bias[:, BAND_ROWS:BAND_ROWS + DEC_SEQ])
        m = jnp.maximum(jnp.max(s_c, axis=-1, keepdims=True), jnp.max(s_n, axis=-1, keepdims=True))
        p_c = jnp.exp(s_c - m)
        p_n = jnp.exp(s_n - m)
        l = jnp.sum(p_c, axis=-1, keepdims=True) + jnp.sum(p_n, axis=-1, keepdims=True)
        o = (jnp.dot(p_c.astype(BF16), vc, preferred_element_type=F32)
             + jnp.dot(p_n.astype(BF16), vn_ref[h], preferred_element_type=F32)) / l
        o_ref[:, cols] = o.astype(BF16)


def band_sample(qkv_hm, cache_k, cache_v, f_rows, o_all):
    row0 = T_PROMPT // DEC_SEQ
    return pl.pallas_call(
        _band_sample_kernel,
        grid=(DEC_BATCH,),
        in_specs=[
            pl.BlockSpec((HEADS, DEC_SEQ, HEAD_DIM), lambda s: (0, row0 + s, 0)),
            pl.BlockSpec((HEADS, DEC_SEQ, HEAD_DIM), lambda s: (1, row0 + s, 0)),
            pl.BlockSpec((HEADS, DEC_SEQ, HEAD_DIM), lambda s: (2, row0 + s, 0)),
            pl.BlockSpec((1, BAND_ROWS, D_MODEL), lambda s: (s, 0, 0)),
            pl.BlockSpec((1, BAND_ROWS, D_MODEL), lambda s: (s, 0, 0)),
            pl.BlockSpec((HEADS, 1, BAND_F), lambda s: (0, 0, 0)),
            pl.BlockSpec(memory_space=pl.ANY),
        ],
        out_specs=pl.BlockSpec((DEC_SEQ, D_MODEL), lambda s: (row0 + s, 0)),
        out_shape=jax.ShapeDtypeStruct((T_ALL, D_MODEL), BF16),
        input_output_aliases={6: 0},
        compiler_params=_params("parallel"),
        name="band_sample",
    )(qkv_hm, qkv_hm, qkv_hm, cache_k, cache_v, f_rows, o_all)


def _proj_residual_kernel(a_ref, w_ref, g_ref, x_ref, o_ref):
    y = jnp.dot(a_ref[...], w_ref[...], preferred_element_type=F32)
    o_ref[...] = x_ref[...] + _rms(y, g_ref[...])


def proj_residual(a, w, g, x, *, tm):
    m, k = a.shape
    n = w.shape[1]
    return pl.pallas_call(
        _proj_residual_kernel,
        grid=(m // tm,),
        in_specs=[
            pl.BlockSpec((tm, k), lambda i: (i, 0)),
            pl.BlockSpec((k, n), lambda i: (0, 0)),
            pl.BlockSpec((1, n), lambda i: (0, 0)),
            pl.BlockSpec((tm, n), lambda i: (i, 0)),
        ],
        out_specs=pl.BlockSpec((tm, n), lambda i: (i, 0)),
        out_shape=jax.ShapeDtypeStruct((m, n), F32),
        compiler_params=_params("parallel"),
        name="proj_residual",
    )(a, w, g.reshape(1, n), x)


def _ffn_kernel(x_ref, g1_ref, w1_ref, w2_ref, g2_ref, o_ref, h_ref):
    j = pl.program_id(1)

    @pl.when(j == 0)
    def _():
        h_ref[...] = _rms(x_ref[...], g1_ref[...]).astype(BF16)

    a = jnp.maximum(jnp.dot(h_ref[...], w1_ref[...], preferred_element_type=F32), 0.0)
    y = jnp.dot((a * a).astype(BF16), w2_ref[...], preferred_element_type=F32)

    @pl.when(j == 0)
    def _():
        o_ref[...] = y

    @pl.when(j > 0)
    def _():
        o_ref[...] += y

    @pl.when(j == pl.num_programs(1) - 1)
    def _():
        o_ref[...] = x_ref[...] + _rms(o_ref[...], g2_ref[...])


def ffn(x, g1, w1, w2, g2, *, tm, tf):
    m, d = x.shape
    f = w1.shape[1]
    return pl.pallas_call(
        _ffn_kernel,
        grid=(m // tm, f // tf),
        in_specs=[
            pl.BlockSpec((tm, d), lambda i, j: (i, 0)),
            pl.BlockSpec((1, d), lambda i, j: (0, 0)),
            pl.BlockSpec((d, tf), lambda i, j: (0, j)),
            pl.BlockSpec((tf, d), lambda i, j: (j, 0)),
            pl.BlockSpec((1, d), lambda i, j: (0, 0)),
        ],
        out_specs=pl.BlockSpec((tm, d), lambda i, j: (i, 0)),
        out_shape=jax.ShapeDtypeStruct((m, d), F32),
        scratch_shapes=[pltpu.VMEM((tm, d), BF16)],
        compiler_params=_params("parallel", "arbitrary"),
        name="ffn",
    )(x, g1.reshape(1, d), w1, w2, g2.reshape(1, d))


def _rope128(x, cos, sin):
    lane = lax.broadcasted_iota(jnp.int32, x.shape, 1)
    half = MLA_ROPE // 2
    swapped = jnp.where(lane < half, pltpu.roll(x, 128 - half, 1), pltpu.roll(x, half, 1))
    return x * cos + swapped * sin


def _mla_proj_kernel(d_ref, qn_ref, kvn_ref, wq_ref, wuk_ref, wuv_ref, cos_ref, sin_ref,
                     ckv_ref, kr_ref, q_ref, k_ref, v_ref):
    d = d_ref[...]
    cos = cos_ref[...]
    sin = sin_ref[...]
    cq = _rms(d[:, :MLA_RANK], qn_ref[...]).astype(BF16)
    ckv = _rms(d[:, MLA_RANK:2 * MLA_RANK], kvn_ref[...])
    ckv_ref[...] = ckv
    kr = _rope128(d[:, 2 * MLA_RANK:], cos, sin)
    kr_ref[...] = kr[:, :MLA_ROPE]
    kr_b = kr.astype(BF16)

    q = jnp.dot(cq, wq_ref[...], preferred_element_type=F32) * ((MLA_NOPE + MLA_ROPE) ** -0.5)
    ckv_b = ckv.astype(BF16)
    kn = jnp.dot(ckv_b, wuk_ref[...], preferred_element_type=F32)
    vv = jnp.dot(ckv_b, wuv_ref[...], preferred_element_type=F32)
    for h in range(HEADS):
        base = h * MLA_QK
        q_ref[h, :, 0:MLA_NOPE] = q[:, base:base + MLA_NOPE].astype(BF16)
        q_ref[h, :, MLA_NOPE:] = _rope128(q[:, base + MLA_NOPE:base + MLA_QK], cos, sin).astype(BF16)
        k_ref[h, :, 0:MLA_NOPE] = kn[:, h * MLA_NOPE:(h + 1) * MLA_NOPE].astype(BF16)
        k_ref[h, :, MLA_NOPE:] = kr_b
        v_ref[h] = vv[:, h * HEAD_DIM:(h + 1) * HEAD_DIM].astype(BF16)


def mla_proj(d, q_norm, kv_norm, wq, wuk, wuv, cos, sin, *, tm):
    m = d.shape[0]
    full = lambda i: (0, 0)
    return pl.pallas_call(
        _mla_proj_kernel,
        grid=(m // tm,),
        in_specs=[
            pl.BlockSpec((tm, d.shape[1]), lambda i: (i, 0)),
            pl.BlockSpec((1, MLA_RANK), full),
            pl.BlockSpec((1, MLA_RANK), full),
            pl.BlockSpec(wq.shape, full),
            pl.BlockSpec(wuk.shape, full),
            pl.BlockSpec(wuv.shape, full),
            pl.BlockSpec((tm, 128), lambda i: (i, 0)),
            pl.BlockSpec((tm, 128), lambda i: (i, 0)),
        ],
        out_specs=[
            pl.BlockSpec((tm, MLA_RANK), lambda i: (i, 0)),
            pl.BlockSpec((tm, MLA_ROPE), lambda i: (i, 0)),
            pl.BlockSpec((HEADS, tm, MLA_QK), lambda i: (0, i, 0)),
            pl.BlockSpec((HEADS, tm, MLA_QK), lambda i: (0, i, 0)),
            pl.BlockSpec((HEADS, tm, HEAD_DIM), lambda i: (0, i, 0)),
        ],
        out_shape=[
            jax.ShapeDtypeStruct((m, MLA_RANK), F32),
            jax.ShapeDtypeStruct((m, MLA_ROPE), F32),
            jax.ShapeDtypeStruct((HEADS, m, MLA_QK), BF16),
            jax.ShapeDtypeStruct((HEADS, m, MLA_QK), BF16),
            jax.ShapeDtypeStruct((HEADS, m, HEAD_DIM), BF16),
        ],
        compiler_params=_params("parallel"),
        name="mla_proj",
    )(d, q_norm.reshape(1, -1), kv_norm.reshape(1, -1), wq, wuk, wuv, cos, sin)


def _mla_prompt_kernel(q_ref, k_ref, v_ref, o_ref, m_ref, l_ref, acc_ref):
    qi = pl.program_id(2)
    q = q_ref[0]
    m_ref[...] = jnp.full(m_ref.shape, NEG, F32)
    l_ref[...] = jnp.zeros(l_ref.shape, F32)
    acc_ref[...] = jnp.zeros(acc_ref.shape, F32)

    def step(kj, masked):
        start = pl.multiple_of(kj * MLA_TK, MLA_TK)
        s = lax.dot_general(q, k_ref[0, pl.ds(start, MLA_TK), :], _NT, preferred_element_type=F32)
        if masked:
            rc = lax.broadcasted_iota(jnp.int32, s.shape, 0) // CHUNK
            cc = lax.broadcasted_iota(jnp.int32, s.shape, 1) // CHUNK
            s = jnp.where(cc <= rc, s, NEG)
        m_prev = m_ref[...]
        m_new = jnp.maximum(m_prev, jnp.max(s, axis=-1, keepdims=True))
        alpha = jnp.exp(m_prev - m_new)
        p = jnp.exp(s - m_new)
        l_ref[...] = alpha * l_ref[...] + jnp.sum(p, axis=-1, keepdims=True)
        acc_ref[...] = alpha * acc_ref[...] + jnp.dot(
            p.astype(BF16), v_ref[0, pl.ds(start, MLA_TK), :], preferred_element_type=F32)
        m_ref[...] = m_new

    def full_block(kj, carry):
        step(kj, False)
        return carry

    lax.fori_loop(0, qi, full_block, 0)
    step(qi, True)
    o_ref[...] = (acc_ref[...] / l_ref[...]).astype(BF16)


def mla_prompt(q, k, v):
    nq = SEQ // MLA_TQ
    return pl.pallas_call(
        _mla_prompt_kernel,
        grid=(BATCH, HEADS, nq),
        in_specs=[
            pl.BlockSpec((1, MLA_TQ, MLA_QK), lambda b, h, i: (h, b * nq + i, 0)),
            pl.BlockSpec((1, SEQ, MLA_QK), lambda b, h, i: (h, b, 0)),
            pl.BlockSpec((1, SEQ, HEAD_DIM), lambda b, h, i: (h, b, 0)),
        ],
        out_specs=pl.BlockSpec((MLA_TQ, HEAD_DIM), lambda b, h, i: (b * nq + i, h)),
        out_shape=jax.ShapeDtypeStruct((T_ALL, D_MODEL), BF16),
        scratch_shapes=[
            pltpu.VMEM((MLA_TQ, 1), F32),
            pltpu.VMEM((MLA_TQ, 1), F32),
            pltpu.VMEM((MLA_TQ, HEAD_DIM), F32),
        ],
        compiler_params=_params("parallel", "parallel", "arbitrary"),
        name="mla_prompt",
    )(q, k, v)


def _mla_sample_kernel(q_ref, cn_ref, rn_ref, cc_ref, cr_ref, wuk_ref, wuv_ref, o_in_ref, o_ref,
                       ql_ref, qr_ref):
    del o_in_ref
    for h in range(HEADS):
        rows = slice(h * DEC_SEQ, (h + 1) * DEC_SEQ)
        ql_ref[rows, :] = lax.dot_general(
            q_ref[h, :, 0:MLA_NOPE], wuk_ref[:, h * MLA_NOPE:(h + 1) * MLA_NOPE], _NT,
            preferred_element_type=F32).astype(BF16)
        qr_ref[rows, :] = q_ref[h, :, MLA_NOPE:]
    ql = ql_ref[...]
    qr = qr_ref[...][:, :MLA_ROPE]
    cc = cc_ref[0].astype(BF16)
    cr = cr_ref[0].astype(BF16)
    cn = cn_ref[...].astype(BF16)
    rn = rn_ref[...].astype(BF16)
    s_c = (lax.dot_general(ql, cc, _NT, preferred_element_type=F32)
           + lax.dot_general(qr, cr, _NT, preferred_element_type=F32))
    s_n = (lax.dot_general(ql, cn, _NT, preferred_element_type=F32)
           + lax.dot_general(qr, rn, _NT, preferred_element_type=F32))
    m = jnp.maximum(jnp.max(s_c, axis=-1, keepdims=True), jnp.max(s_n, axis=-1, keepdims=True))
    p_c = jnp.exp(s_c - m)
    p_n = jnp.exp(s_n - m)
    l = jnp.sum(p_c, axis=-1, keepdims=True) + jnp.sum(p_n, axis=-1, keepdims=True)
    o_lat = ((jnp.dot(p_c.astype(BF16), cc, preferred_element_type=F32)
              + jnp.dot(p_n.astype(BF16), cn, preferred_element_type=F32)) / l).astype(BF16)
    for h in range(HEADS):
        cols = slice(h * HEAD_DIM, (h + 1) * HEAD_DIM)
        o_ref[:, cols] = jnp.dot(o_lat[h * DEC_SEQ:(h + 1) * DEC_SEQ], wuv_ref[:, cols],
                                 preferred_element_type=F32).astype(BF16)


def mla_sample(q, ckv, kr, cache_ckv, cache_kr, wuk, wuv, o_all):
    row0 = T_PROMPT // DEC_SEQ
    full = lambda s: (0, 0)
    return pl.pallas_call(
        _mla_sample_kernel,
        grid=(DEC_BATCH,),
        in_specs=[
            pl.BlockSpec((HEADS, DEC_SEQ, MLA_QK), lambda s: (0, row0 + s, 0)),
            pl.BlockSpec((DEC_SEQ, MLA_RANK), lambda s: (row0 + s, 0)),
            pl.BlockSpec((DEC_SEQ, MLA_ROPE), lambda s: (row0 + s, 0)),
            pl.BlockSpec((1, PAST_LEN, MLA_RANK), lambda s: (s, 0, 0)),
            pl.BlockSpec((1, PAST_LEN, MLA_ROPE), lambda s: (s, 0, 0)),
            pl.BlockSpec(wuk.shape, full),
            pl.BlockSpec(wuv.shape, full),
            pl.BlockSpec(memory_space=pl.ANY),
        ],
        out_specs=pl.BlockSpec((DEC_SEQ, D_MODEL), lambda s: (row0 + s, 0)),
        out_shape=jax.ShapeDtypeStruct((T_ALL, D_MODEL), BF16),
        scratch_shapes=[
            pltpu.VMEM((HEADS * DEC_SEQ, MLA_RANK), BF16),
            pltpu.VMEM((HEADS * DEC_SEQ, 128), BF16),
        ],
        input_output_aliases={7: 0},
        compiler_params=_params("parallel"),
        name="mla_sample",
    )(q, ckv, kr, cache_ckv, cache_kr, wuk, wuv, o_all)


def _bias_rows(rel_bias):
    far = jnp.broadcast_to(rel_bias[:, 2 * REL_CLIP:], (HEADS, BAND_F // 4))
    mid = rel_bias[:, 1:][:, ::-1]
    return jnp.concatenate([far, mid, far], axis=1).reshape(HEADS, 1, BAND_F)


def _rope_tables():
    half = MLA_ROPE // 2
    pos = jnp.concatenate([
        jnp.tile(jnp.arange(SEQ), BATCH),
        jnp.tile(PAST_LEN + jnp.arange(DEC_SEQ), DEC_BATCH)]).astype(F32)
    inv = 1.0 / (ROPE_THETA ** (jnp.arange(half, dtype=F32) / half))
    ang = pos[:, None] * inv[None, :]
    c, s = jnp.cos(ang), jnp.sin(ang)
    z = jnp.zeros((T_ALL, 128 - MLA_ROPE), F32)
    return jnp.concatenate([c, c, z], axis=1), jnp.concatenate([-s, s, z], axis=1)


def kernel(x_prompt, x_sample, cache_a_k, cache_a_v, cache_mla_ckv, cache_mla_kr, ln_mix_pre, ln_mix_post, ln_ffn_pre, ln_ffn_post, a_w_qkv, a_w_o, a_rel_bias, mla_w_dq, mla_q_norm, mla_w_uq, mla_w_dkv, mla_kv_norm, mla_w_uk, mla_w_uv, mla_w_o, ffn_w1, ffn_w2):
    x = jnp.concatenate([x_prompt.reshape(T_PROMPT, D_MODEL), x_sample.reshape(T_SAMPLE, D_MODEL)], axis=0)
    w1 = ffn_w1.astype(BF16)
    w2 = ffn_w2.astype(BF16)

    qkv_hm, kv = norm_qkv(x, ln_mix_pre[0], a_w_qkv[0].astype(BF16), tm=1024)
    o = band_prompt(qkv_hm, _bias_rows(a_rel_bias[0]))
    o = band_sample(qkv_hm,
                    cache_a_k[0].reshape(DEC_BATCH, BAND_ROWS, D_MODEL),
                    cache_a_v[0].reshape(DEC_BATCH, BAND_ROWS, D_MODEL),
                    _bias_rows(a_rel_bias[0]), o)
    x = proj_residual(o, a_w_o[0].astype(BF16), ln_mix_post[0], x, tm=512)
    x = ffn(x, ln_ffn_pre[0], w1[0], w2[0], ln_ffn_post[0], tm=512, tf=1024)

    kv_p = kv[:T_PROMPT].reshape(BATCH, SEQ, 2, HEADS, HEAD_DIM)[:, SEQ - BAND_ROWS:]
    kv_s = kv[T_PROMPT:].reshape(DEC_BATCH, DEC_SEQ, 2, HEADS, HEAD_DIM)

    pad = jnp.zeros((D_MODEL, 128 - MLA_ROPE), F32)
    w_d = jnp.concatenate([mla_w_dq[0], mla_w_dkv[0], pad], axis=1).astype(BF16)
    wq = jnp.pad(mla_w_uq[0], ((0, 0), (0, 0), (0, MLA_QK - MLA_NOPE - MLA_ROPE)))
    wq = wq.reshape(MLA_RANK, HEADS * MLA_QK).astype(BF16)
    wuk = mla_w_uk[0].reshape(MLA_RANK, HEADS * MLA_NOPE).astype(BF16)
    wuv = mla_w_uv[0].reshape(MLA_RANK, HEADS * HEAD_DIM).astype(BF16)
    cos, sin = _rope_tables()

    d = norm_matmul(x, ln_mix_pre[1], w_d, tm=512, tn=w_d.shape[1], out_dtype=F32)
    ckv, kr, q, k, v = mla_proj(d, mla_q_norm[0], mla_kv_norm[0], wq, wuk, wuv, cos, sin, tm=256)
    o = mla_prompt(q, k, v)
    o = mla_sample(q, ckv, kr, cache_mla_ckv[0], cache_mla_kr[0], wuk, wuv, o)
    x = proj_residual(o, mla_w_o[0].astype(BF16), ln_mix_post[1], x, tm=512)
    x = ffn(x, ln_ffn_pre[1], w1[1], w2[1], ln_ffn_post[1], tm=512, tf=1024)

    return (
        x[:T_PROMPT].reshape(BATCH, SEQ, D_MODEL),
        x[T_PROMPT:].reshape(DEC_BATCH, DEC_SEQ, D_MODEL),
        kv_p[:, :, 0][None], kv_p[:, :, 1][None],
        kv_s[:, :, 0][None], kv_s[:, :, 1][None],
        ckv[:T_PROMPT].reshape(1, BATCH, SEQ, MLA_RANK),
        kr[:T_PROMPT].reshape(1, BATCH, SEQ, MLA_ROPE),
        ckv[T_PROMPT:].reshape(1, DEC_BATCH, DEC_SEQ, MLA_RANK),
        kr[T_PROMPT:].reshape(1, DEC_BATCH, DEC_SEQ, MLA_ROPE),
    )
```

```python
import functools
import math

import jax
import jax.numpy as jnp
from jax import lax
from jax.experimental import pallas as pl
from jax.experimental.pallas import tpu as pltpu

F32 = jnp.float32
BF16 = jnp.bfloat16

D_MODEL = 2048
BATCH = 4
SEQ = 4096
DEC_BATCH = 16
DEC_SEQ = 64
PAST_LEN = 2048
CHUNK = 64
HEADS = 16
HEAD_DIM = 128
BAND_ROWS = 512
REL_CLIP = 256
MLA_RANK = 512
MLA_NOPE = 128
MLA_ROPE = 64
MLA_QK = 256
ROPE_THETA = 10000.0
D_FF = 8192
EPS = 1e-6
NEG = -1e30

T_PROMPT = BATCH * SEQ
T_SAMPLE = DEC_BATCH * DEC_SEQ
T_ALL = T_PROMPT + T_SAMPLE

VMEM_LIMIT_BYTES = 58 * 1024 * 1024

BAND_TQ = 256
BAND_TK = BAND_ROWS + BAND_TQ
BAND_F = 1024
BAND_NQ = SEQ // BAND_TQ
BAND_SUB = 64
LOG2E = math.log2(math.e)
BAND_Q_SCALE = HEAD_DIM ** -0.5 * LOG2E

MLA_TQ = 512
MLA_TK = 512
MLA_NQ = SEQ // MLA_TQ
MLA_SUB = 128
MLA_Q_SCALE = (MLA_NOPE + MLA_ROPE) ** -0.5 * math.log2(math.e)

_NT = (((1,), (1,)), ((), ()))


def _params(*sem):
    return pltpu.CompilerParams(dimension_semantics=sem, vmem_limit_bytes=VMEM_LIMIT_BYTES)


def _rms(x, g):
    ms = jnp.mean(x * x, axis=-1, keepdims=True)
    return x * lax.rsqrt(ms + EPS) * g


def _norm_matmul_kernel(x_ref, g_ref, w_ref, o_ref, h_ref):
    @pl.when(pl.program_id(1) == 0)
    def _():
        h_ref[...] = _rms(x_ref[...], g_ref[...]).astype(BF16)

    o_ref[...] = jnp.dot(h_ref[...], w_ref[...], preferred_element_type=F32).astype(o_ref.dtype)


def norm_matmul(x, g, w, *, tm, tn, out_dtype):
    m, k = x.shape
    n = w.shape[1]
    return pl.pallas_call(
        _norm_matmul_kernel,
        grid=(m // tm, n // tn),
        in_specs=[
            pl.BlockSpec((tm, k), lambda i, j: (i, 0)),
            pl.BlockSpec((1, k), lambda i, j: (0, 0)),
            pl.BlockSpec((k, tn), lambda i, j: (0, j)),
        ],
        out_specs=pl.BlockSpec((tm, tn), lambda i, j: (i, j)),
        out_shape=jax.ShapeDtypeStruct((m, n), out_dtype),
        scratch_shapes=[pltpu.VMEM((tm, k), BF16)],
        compiler_params=_params("parallel", "arbitrary"),
        name="norm_matmul",
    )(x, g.reshape(1, k), w)


QKV_TN = 1024
QKV_HEADS_PER_BLOCK = QKV_TN // HEAD_DIM


def _norm_qkv_kernel(x_ref, g_ref, w_ref, hm_ref, kv_ref, h_ref):
    j = pl.program_id(1)

    @pl.when(j == 0)
    def _():
        h_ref[...] = _rms(x_ref[...], g_ref[...]).astype(BF16)

    acc = jnp.dot(h_ref[...], w_ref[...], preferred_element_type=F32)

    @pl.when(j < 4)
    def _():
        kv_ref[...] = acc
        for h in range(QKV_HEADS_PER_BLOCK):
            hm_ref[h] = acc[:, h * HEAD_DIM:(h + 1) * HEAD_DIM].astype(BF16)

    @pl.when(j >= 4)
    def _():
        scaled = acc * BAND_Q_SCALE
        for h in range(QKV_HEADS_PER_BLOCK):
            hm_ref[h] = scaled[:, h * HEAD_DIM:(h + 1) * HEAD_DIM].astype(BF16)


def norm_qkv(x, g, w, *, tm):
    m, k = x.shape
    nb = 3 * D_MODEL // QKV_TN

    def col(j):
        return (j + 2) % nb

    return pl.pallas_call(
        _norm_qkv_kernel,
        grid=(m // tm, nb),
        in_specs=[
            pl.BlockSpec((tm, k), lambda i, j: (i, 0)),
            pl.BlockSpec((1, k), lambda i, j: (0, 0)),
            pl.BlockSpec((k, QKV_TN), lambda i, j: (0, col(j))),
        ],
        out_specs=[
            pl.BlockSpec((QKV_HEADS_PER_BLOCK, tm, HEAD_DIM), lambda i, j: (col(j), i, 0)),
            pl.BlockSpec((tm, QKV_TN), lambda i, j: (i, jnp.minimum(j, 3))),
        ],
        out_shape=[
            jax.ShapeDtypeStruct((3 * HEADS, m, HEAD_DIM), BF16),
            jax.ShapeDtypeStruct((m, 2 * D_MODEL), F32),
        ],
        scratch_shapes=[pltpu.VMEM((tm, k), BF16)],
        compiler_params=_params("parallel", "arbitrary"),
        name="norm_qkv",
    )(x, g.reshape(1, k), w)


def _toeplitz_bias(f_row, rows):
    x = jnp.broadcast_to(f_row, (rows, BAND_F))
    return pltpu.roll(x, 0, 1, stride=1, stride_axis=0)


def _band_prompt_kernel(q_ref, k_ref, v_ref, f_ref, o_ref, kpad_ref, vpad_ref, bias_ref, s_ref, s2_ref):
    @pl.when((pl.program_id(0) == 0) & (pl.program_id(1) == 0))
    def _():
        kpad_ref[0:BAND_ROWS, :] = jnp.zeros((BAND_ROWS, HEAD_DIM), BF16)
        vpad_ref[0:BAND_ROWS, 0:HEAD_DIM] = jnp.zeros((BAND_ROWS, HEAD_DIM), BF16)
        vpad_ref[:, HEAD_DIM:] = jnp.ones((BAND_ROWS + SEQ, HEAD_DIM), BF16)

    @pl.when(pl.program_id(1) == 0)
    def _():
        r = lax.broadcasted_iota(jnp.int32, (BAND_TQ, BAND_TK), 0) // CHUNK
        j = lax.broadcasted_iota(jnp.int32, (BAND_TQ, BAND_TK), 1)
        jc = j // CHUNK
        allowed = (jc >= r) & (jc <= r + BAND_ROWS // CHUNK)
        base = jnp.where(allowed, _toeplitz_bias(f_ref[0], BAND_TQ)[:, :BAND_TK], NEG)
        bias_ref[2] = base
        bias_ref[1] = jnp.where(j >= BAND_ROWS - BAND_TQ, base, NEG)
        bias_ref[0] = jnp.where(j >= BAND_ROWS, base, NEG)

    kpad_ref[BAND_ROWS:, :] = k_ref[0]
    vpad_ref[BAND_ROWS:, 0:HEAD_DIM] = v_ref[0]

    def scores(qi, dst_ref):
        start = pl.multiple_of(qi * BAND_TQ, BAND_TQ)
        dst_ref[...] = lax.dot_general(q_ref[0, pl.ds(start, BAND_TQ), :],
                                       kpad_ref[pl.ds(start, BAND_TK), :], _NT,
                                       preferred_element_type=F32)

    scores(0, s_ref)

    def block(qi, cur_ref, nxt_ref):
        scores(jnp.minimum(qi + 1, BAND_NQ - 1), nxt_ref)
        which = jnp.minimum(qi, 2)
        start = pl.multiple_of(qi * BAND_TQ, BAND_TQ)
        vw = vpad_ref[pl.ds(start, BAND_TK), :]
        for u in range(BAND_TQ // BAND_SUB):
            rows = slice(u * BAND_SUB, (u + 1) * BAND_SUB)
            s = cur_ref[rows, :] + bias_ref[which, rows, :]
            p = jnp.exp2(s - jnp.max(s, axis=-1, keepdims=True))
            pv = jnp.dot(p.astype(BF16), vw, preferred_element_type=F32)
            o_ref[pl.ds(start + u * BAND_SUB, BAND_SUB), :] = (
                pv[:, :HEAD_DIM] / pv[:, HEAD_DIM:]).astype(BF16)

    def two_blocks(t, carry):
        block(2 * t, s_ref, s2_ref)
        block(2 * t + 1, s2_ref, s_ref)
        return carry

    lax.fori_loop(0, BAND_NQ // 2, two_blocks, 0)


def band_prompt(qkv_hm, f_rows):
    return pl.pallas_call(
        _band_prompt_kernel,
        grid=(HEADS, BATCH),
        in_specs=[
            pl.BlockSpec((1, SEQ, HEAD_DIM), lambda h, b: (h, b, 0)),
            pl.BlockSpec((1, SEQ, HEAD_DIM), lambda h, b: (HEADS + h, b, 0)),
            pl.BlockSpec((1, SEQ, HEAD_DIM), lambda h, b: (2 * HEADS + h, b, 0)),
            pl.BlockSpec((1, 1, BAND_F), lambda h, b: (h, 0, 0)),
        ],
        out_specs=pl.BlockSpec((SEQ, HEAD_DIM), lambda h, b: (b, h)),
        out_shape=jax.ShapeDtypeStruct((T_ALL, D_MODEL), BF16),
        scratch_shapes=[
            pltpu.VMEM((BAND_ROWS + SEQ, HEAD_DIM), BF16),
            pltpu.VMEM((BAND_ROWS + SEQ, 2 * HEAD_DIM), BF16),
            pltpu.VMEM((3, BAND_TQ, BAND_TK), F32),
            pltpu.VMEM((BAND_TQ, BAND_TK), F32),
            pltpu.VMEM((BAND_TQ, BAND_TK), F32),
        ],
        compiler_params=_params("arbitrary", "arbitrary"),
        name="band_prompt",
    )(qkv_hm, qkv_hm, qkv_hm, f_rows)


def _band_sample_kernel(q_ref, kn_ref, vn_ref, ck_ref, cv_ref, f_ref, o_in_ref, o_ref):
    del o_in_ref
    for h in range(HEADS):
        cols = slice(h * HEAD_DIM, (h + 1) * HEAD_DIM)
        bias = _toeplitz_bias(f_ref[h], DEC_SEQ)
        q = q_ref[h]
        kc = ck_ref[0, :, cols].astype(BF16)
        vc = cv_ref[0, :, cols].astype(BF16)
        s_c = lax.dot_general(q, kc, _NT, preferred_element_type=F32) + bias[:, :BAND_ROWS]
        s_n = (lax.dot_general(q, kn_ref[h], _NT, preferred_element_type=F32)
               + bias[:, BAND_ROWS:BAND_ROWS + DEC_SEQ])
        m = jnp.maximum(jnp.max(s_c, axis=-1, keepdims=True), jnp.max(s_n, axis=-1, keepdims=True))
        p_c = jnp.exp2(s_c - m)
        p_n = jnp.exp2(s_n - m)
        l = jnp.sum(p_c, axis=-1, keepdims=True) + jnp.sum(p_n, axis=-1, keepdims=True)
        o = (jnp.dot(p_c.astype(BF16), vc, preferred_element_type=F32)
             + jnp.dot(p_n.astype(BF16), vn_ref[h], preferred_element_type=F32)) / l
        o_ref[:, cols] = o.astype(BF16)


def band_sample(qkv_hm, cache_k, cache_v, f_rows, o_all):
    row0 = T_PROMPT // DEC_SEQ
    return pl.pallas_call(
        _band_sample_kernel,
        grid=(DEC_BATCH,),
        in_specs=[
            pl.BlockSpec((HEADS, DEC_SEQ, HEAD_DIM), lambda s: (0, row0 + s, 0)),
            pl.BlockSpec((HEADS, DEC_SEQ, HEAD_DIM), lambda s: (1, row0 + s, 0)),
            pl.BlockSpec((HEADS, DEC_SEQ, HEAD_DIM), lambda s: (2, row0 + s, 0)),
            pl.BlockSpec((1, BAND_ROWS, D_MODEL), lambda s: (s, 0, 0)),
            pl.BlockSpec((1, BAND_ROWS, D_MODEL), lambda s: (s, 0, 0)),
            pl.BlockSpec((HEADS, 1, BAND_F), lambda s: (0, 0, 0)),
            pl.BlockSpec(memory_space=pl.ANY),
        ],
        out_specs=pl.BlockSpec((DEC_SEQ, D_MODEL), lambda s: (row0 + s, 0)),
        out_shape=jax.ShapeDtypeStruct((T_ALL, D_MODEL), BF16),
        input_output_aliases={6: 0},
        compiler_params=_params("parallel"),
        name="band_sample",
    )(qkv_hm, qkv_hm, qkv_hm, cache_k, cache_v, f_rows, o_all)


def _proj_residual_kernel(a_ref, w_ref, g_ref, x_ref, o_ref):
    y = jnp.dot(a_ref[...], w_ref[...], preferred_element_type=F32)
    o_ref[...] = x_ref[...] + _rms(y, g_ref[...])


def proj_residual(a, w, g, x, *, tm):
    m, k = a.shape
    n = w.shape[1]
    return pl.pallas_call(
        _proj_residual_kernel,
        grid=(m // tm,),
        in_specs=[
            pl.BlockSpec((tm, k), lambda i: (i, 0)),
            pl.BlockSpec((k, n), lambda i: (0, 0)),
            pl.BlockSpec((1, n), lambda i: (0, 0)),
            pl.BlockSpec((tm, n), lambda i: (i, 0)),
        ],
        out_specs=pl.BlockSpec((tm, n), lambda i: (i, 0)),
        out_shape=jax.ShapeDtypeStruct((m, n), F32),
        compiler_params=_params("parallel"),
        name="proj_residual",
    )(a, w, g.reshape(1, n), x)


def _ffn_kernel(x_ref, g1_ref, w1_ref, w2_ref, g2_ref, o_ref, h_ref):
    j = pl.program_id(1)

    @pl.when(j == 0)
    def _():
        h_ref[...] = _rms(x_ref[...], g1_ref[...]).astype(BF16)

    a = jnp.maximum(jnp.dot(h_ref[...], w1_ref[...], preferred_element_type=F32), 0.0)
    y = jnp.dot((a * a).astype(BF16), w2_ref[...], preferred_element_type=F32)

    @pl.when(j == 0)
    def _():
        o_ref[...] = y

    @pl.when(j > 0)
    def _():
        o_ref[...] += y

    @pl.when(j == pl.num_programs(1) - 1)
    def _():
        o_ref[...] = x_ref[...] + _rms(o_ref[...], g2_ref[...])


def ffn(x, g1, w1, w2, g2, *, tm, tf):
    m, d = x.shape
    f = w1.shape[1]
    return pl.pallas_call(
        _ffn_kernel,
        grid=(m // tm, f // tf),
        in_specs=[
            pl.BlockSpec((tm, d), lambda i, j: (i, 0)),
            pl.BlockSpec((1, d), lambda i, j: (0, 0)),
            pl.BlockSpec((d, tf), lambda i, j: (0, j)),
            pl.BlockSpec((tf, d), lambda i, j: (j, 0)),
            pl.BlockSpec((1, d), lambda i, j: (0, 0)),
        ],
        out_specs=pl.BlockSpec((tm, d), lambda i, j: (i, 0)),
        out_shape=jax.ShapeDtypeStruct((m, d), F32),
        scratch_shapes=[pltpu.VMEM((tm, d), BF16)],
        compiler_params=_params("parallel", "arbitrary"),
        name="ffn",
    )(x, g1.reshape(1, d), w1, w2, g2.reshape(1, d))


def _rope128(x, cos, sin):
    lane = lax.broadcasted_iota(jnp.int32, x.shape, 1)
    half = MLA_ROPE // 2
    swapped = jnp.where(lane < half, pltpu.roll(x, 128 - half, 1), pltpu.roll(x, half, 1))
    return x * cos + swapped * sin


def _mla_proj_kernel(d_ref, qn_ref, kvn_ref, wq_ref, wuk_ref, wuv_ref, cos_ref, sin_ref,
                     ckv_ref, kr_ref, q_ref, k_ref, v_ref):
    d = d_ref[...]
    cos = cos_ref[...]
    sin = sin_ref[...]
    cq = _rms(d[:, :MLA_RANK], qn_ref[...]).astype(BF16)
    ckv = _rms(d[:, MLA_RANK:2 * MLA_RANK], kvn_ref[...])
    ckv_ref[...] = ckv
    kr = _rope128(d[:, 2 * MLA_RANK:], cos, sin)
    kr_ref[...] = kr[:, :MLA_ROPE]
    kr_b = kr.astype(BF16)

    q = jnp.dot(cq, wq_ref[...], preferred_element_type=F32) * MLA_Q_SCALE
    ckv_b = ckv.astype(BF16)
    kn = jnp.dot(ckv_b, wuk_ref[...], preferred_element_type=F32)
    vv = jnp.dot(ckv_b, wuv_ref[...], preferred_element_type=F32)
    ones = jnp.ones((d.shape[0], HEAD_DIM), BF16)
    for h in range(HEADS):
        base = h * MLA_QK
        q_ref[h, :, 0:MLA_NOPE] = q[:, base:base + MLA_NOPE].astype(BF16)
        q_ref[h, :, MLA_NOPE:] = _rope128(q[:, base + MLA_NOPE:base + MLA_QK], cos, sin).astype(BF16)
        k_ref[h, :, 0:MLA_NOPE] = kn[:, h * MLA_NOPE:(h + 1) * MLA_NOPE].astype(BF16)
        k_ref[h, :, MLA_NOPE:] = kr_b
        v_ref[h, :, 0:HEAD_DIM] = vv[:, h * HEAD_DIM:(h + 1) * HEAD_DIM].astype(BF16)
        v_ref[h, :, HEAD_DIM:] = ones


def mla_proj(d, q_norm, kv_norm, wq, wuk, wuv, cos, sin, *, tm):
    m = d.shape[0]
    full = lambda i: (0, 0)
    return pl.pallas_call(
        _mla_proj_kernel,
        grid=(m // tm,),
        in_specs=[
            pl.BlockSpec((tm, d.shape[1]), lambda i: (i, 0)),
            pl.BlockSpec((1, MLA_RANK), full),
            pl.BlockSpec((1, MLA_RANK), full),
            pl.BlockSpec(wq.shape, full),
            pl.BlockSpec(wuk.shape, full),
            pl.BlockSpec(wuv.shape, full),
            pl.BlockSpec((tm, 128), lambda i: (i, 0)),
            pl.BlockSpec((tm, 128), lambda i: (i, 0)),
        ],
        out_specs=[
            pl.BlockSpec((tm, MLA_RANK), lambda i: (i, 0)),
            pl.BlockSpec((tm, MLA_ROPE), lambda i: (i, 0)),
            pl.BlockSpec((HEADS, tm, MLA_QK), lambda i: (0, i, 0)),
            pl.BlockSpec((HEADS, tm, MLA_QK), lambda i: (0, i, 0)),
            pl.BlockSpec((HEADS, tm, 2 * HEAD_DIM), lambda i: (0, i, 0)),
        ],
        out_shape=[
            jax.ShapeDtypeStruct((m, MLA_RANK), F32),
            jax.ShapeDtypeStruct((m, MLA_ROPE), F32),
            jax.ShapeDtypeStruct((HEADS, m, MLA_QK), BF16),
            jax.ShapeDtypeStruct((HEADS, m, MLA_QK), BF16),
            jax.ShapeDtypeStruct((HEADS, m, 2 * HEAD_DIM), BF16),
        ],
        compiler_params=_params("parallel"),
        name="mla_proj",
    )(d, q_norm.reshape(1, -1), kv_norm.reshape(1, -1), wq, wuk, wuv, cos, sin)


def _mla_prompt_kernel(q_ref, k_ref, v_ref, o_ref, s_ref, s2_ref, bias_ref, m_ref, acc_ref):
    @pl.when((pl.program_id(0) == 0) & (pl.program_id(1) == 0))
    def _():
        rc = lax.broadcasted_iota(jnp.int32, (MLA_TQ, MLA_TK), 0) // CHUNK
        cc = lax.broadcasted_iota(jnp.int32, (MLA_TQ, MLA_TK), 1) // CHUNK
        bias_ref[0] = jnp.zeros((MLA_TQ, MLA_TK), F32)
        bias_ref[1] = jnp.where(cc <= rc, 0.0, NEG)

    def scores(qi, kj, dst_ref):
        q = q_ref[0, pl.ds(pl.multiple_of(qi * MLA_TQ, MLA_TQ), MLA_TQ), :]
        k = k_ref[0, pl.ds(pl.multiple_of(kj * MLA_TK, MLA_TK), MLA_TK), :]
        dst_ref[...] = lax.dot_general(q, k, _NT, preferred_element_type=F32)

    scores(0, 0, s_ref)

    def pair(carry, cur_ref, nxt_ref):
        qi, kj = carry
        diag = kj == qi
        qn = jnp.where(diag, qi + 1, qi)
        kn = jnp.where(diag, 0, kj + 1)
        scores(jnp.minimum(qn, MLA_NQ - 1), kn, nxt_ref)

        first = kj == 0
        which = diag.astype(jnp.int32)
        q0 = pl.multiple_of(qi * MLA_TQ, MLA_TQ)
        v = v_ref[0, pl.ds(pl.multiple_of(kj * MLA_TK, MLA_TK), MLA_TK), :]
        for u in range(MLA_TQ // MLA_SUB):
            rows = slice(u * MLA_SUB, (u + 1) * MLA_SUB)
            s = cur_ref[rows, :] + bias_ref[which, rows, :]
            m_prev = jnp.where(first, NEG, m_ref[rows, :])
            acc_prev = jnp.where(first, 0.0, acc_ref[rows, :])
            m_new = jnp.maximum(m_prev, jnp.max(s, axis=-1, keepdims=True))
            alpha = jnp.exp2(m_prev - m_new)
            p = jnp.exp2(s - jnp.tile(m_new, (1, MLA_TK // 128)))
            acc = jnp.tile(alpha, (1, 2)) * acc_prev + jnp.dot(
                p.astype(BF16), v, preferred_element_type=F32)
            acc_ref[rows, :] = acc
            m_ref[rows, :] = m_new
            o_ref[pl.ds(q0 + u * MLA_SUB, MLA_SUB), :] = (
                acc[:, :HEAD_DIM] / acc[:, HEAD_DIM:]).astype(BF16)
        return qn, kn

    def two_pairs(t, carry):
        return pair(pair(carry, s_ref, s2_ref), s2_ref, s_ref)

    n_pairs = MLA_NQ * (MLA_NQ + 1) // 2
    assert n_pairs % 2 == 0
    lax.fori_loop(0, n_pairs // 2, two_pairs, (jnp.int32(0), jnp.int32(0)))


def mla_prompt(q, k, v):
    return pl.pallas_call(
        _mla_prompt_kernel,
        grid=(BATCH, HEADS),
        in_specs=[
            pl.BlockSpec((1, SEQ, MLA_QK), lambda b, h: (h, b, 0)),
            pl.BlockSpec((1, SEQ, MLA_QK), lambda b, h: (h, b, 0)),
            pl.BlockSpec((1, SEQ, 2 * HEAD_DIM), lambda b, h: (h, b, 0)),
        ],
        out_specs=pl.BlockSpec((SEQ, HEAD_DIM), lambda b, h: (b, h)),
        out_shape=jax.ShapeDtypeStruct((T_ALL, D_MODEL), BF16),
        scratch_shapes=[
            pltpu.VMEM((MLA_TQ, MLA_TK), F32),
            pltpu.VMEM((MLA_TQ, MLA_TK), F32),
            pltpu.VMEM((2, MLA_TQ, MLA_TK), F32),
            pltpu.VMEM((MLA_TQ, 128), F32),
            pltpu.VMEM((MLA_TQ, 2 * HEAD_DIM), F32),
        ],
        compiler_params=_params("arbitrary", "arbitrary"),
        name="mla_prompt",
    )(q, k, v)


def _mla_sample_kernel(q_ref, cn_ref, rn_ref, cc_ref, cr_ref, wuk_ref, wuv_ref, o_in_ref, o_ref,
                       ql_ref, qr_ref):
    del o_in_ref
    for h in range(HEADS):
        rows = slice(h * DEC_SEQ, (h + 1) * DEC_SEQ)
        ql_ref[rows, :] = lax.dot_general(
            q_ref[h, :, 0:MLA_NOPE], wuk_ref[:, h * MLA_NOPE:(h + 1) * MLA_NOPE], _NT,
            preferred_element_type=F32).astype(BF16)
        qr_ref[rows, :] = q_ref[h, :, MLA_NOPE:]
    ql = ql_ref[...]
    qr = qr_ref[...][:, :MLA_ROPE]
    cc = cc_ref[0].astype(BF16)
    cr = cr_ref[0].astype(BF16)
    cn = cn_ref[...].astype(BF16)
    rn = rn_ref[...].astype(BF16)
    s_c = (lax.dot_general(ql, cc, _NT, preferred_element_type=F32)
           + lax.dot_general(qr, cr, _NT, preferred_element_type=F32))
    s_n = (lax.dot_general(ql, cn, _NT, preferred_element_type=F32)
           + lax.dot_general(qr, rn, _NT, preferred_element_type=F32))
    m = jnp.maximum(jnp.max(s_c, axis=-1, keepdims=True), jnp.max(s_n, axis=-1, keepdims=True))
    p_c = jnp.exp2(s_c - m)
    p_n = jnp.exp2(s_n - m)
    l = jnp.sum(p_c, axis=-1, keepdims=True) + jnp.sum(p_n, axis=-1, keepdims=True)
    o_lat =((jnp.dot(p_c.astype(BF16), cc, preferred_element_type=F32)
              + jnp.dot(p_n.astype(BF16), cn, preferred_element_type=F32)) / l).astype(BF16)
    for h in range(HEADS):
        cols = slice(h * HEAD_DIM, (h + 1) * HEAD_DIM)
        o_ref[:, cols] = jnp.dot(o_lat[h * DEC_SEQ:(h + 1) * DEC_SEQ], wuv_ref[:, cols],
                                 preferred_element_type=F32).astype(BF16)


def mla_sample(q, ckv, kr, cache_ckv, cache_kr, wuk, wuv, o_all):
    row0 = T_PROMPT // DEC_SEQ
    full = lambda s: (0, 0)
    return pl.pallas_call(
        _mla_sample_kernel,
        grid=(DEC_BATCH,),
        in_specs=[
            pl.BlockSpec((HEADS, DEC_SEQ, MLA_QK), lambda s: (0, row0 + s, 0)),
            pl.BlockSpec((DEC_SEQ, MLA_RANK), lambda s: (row0 + s, 0)),
            pl.BlockSpec((DEC_SEQ, MLA_ROPE), lambda s: (row0 + s, 0)),
            pl.BlockSpec((1, PAST_LEN, MLA_RANK), lambda s: (s, 0, 0)),
            pl.BlockSpec((1, PAST_LEN, MLA_ROPE), lambda s: (s, 0, 0)),
            pl.BlockSpec(wuk.shape, full),
            pl.BlockSpec(wuv.shape, full),
            pl.BlockSpec(memory_space=pl.ANY),
        ],
        out_specs=pl.BlockSpec((DEC_SEQ, D_MODEL), lambda s: (row0 + s, 0)),
        out_shape=jax.ShapeDtypeStruct((T_ALL, D_MODEL), BF16),
        scratch_shapes=[
            pltpu.VMEM((HEADS * DEC_SEQ, MLA_RANK), BF16),
            pltpu.VMEM((HEADS * DEC_SEQ, 128), BF16),
        ],
        input_output_aliases={7: 0},
        compiler_params=_params("parallel"),
        name="mla_sample",
    )(q, ckv, kr, cache_ckv, cache_kr, wuk, wuv, o_all)


def _bias_rows(rel_bias):
    far = jnp.broadcast_to(rel_bias[:, 2 * REL_CLIP:], (HEADS, BAND_F // 4))
    mid = rel_bias[:, 1:][:, ::-1]
    return (jnp.concatenate([far, mid, far], axis=1) * LOG2E).reshape(HEADS, 1, BAND_F)


def _rope_tables():
    half = MLA_ROPE // 2
    pos = jnp.concatenate([
        jnp.tile(jnp.arange(SEQ), BATCH),
        jnp.tile(PAST_LEN + jnp.arange(DEC_SEQ), DEC_BATCH)]).astype(F32)
    inv = 1.0 / (ROPE_THETA ** (jnp.arange(half, dtype=F32) / half))
    ang = pos[:, None] * inv[None, :]
    c, s = jnp.cos(ang), jnp.sin(ang)
    z = jnp.zeros((T_ALL, 128 - MLA_ROPE), F32)
    return jnp.concatenate([c, c, z], axis=1), jnp.concatenate([-s, s, z], axis=1)


def kernel(x_prompt, x_sample, cache_a_k, cache_a_v, cache_mla_ckv, cache_mla_kr, ln_mix_pre, ln_mix_post, ln_ffn_pre, ln_ffn_post, a_w_qkv, a_w_o, a_rel_bias, mla_w_dq, mla_q_norm, mla_w_uq, mla_w_dkv, mla_kv_norm, mla_w_uk, mla_w_uv, mla_w_o, ffn_w1, ffn_w2):
    x = jnp.concatenate([x_prompt.reshape(T_PROMPT, D_MODEL), x_sample.reshape(T_SAMPLE, D_MODEL)], axis=0)
    w1 = ffn_w1.astype(BF16)
    w2 = ffn_w2.astype(BF16)

    qkv_hm, kv = norm_qkv(x, ln_mix_pre[0], a_w_qkv[0].astype(BF16), tm=1024)
    o = band_prompt(qkv_hm, _bias_rows(a_rel_bias[0]))
    o = band_sample(qkv_hm,
                    cache_a_k[0].reshape(DEC_BATCH, BAND_ROWS, D_MODEL),
                    cache_a_v[0].reshape(DEC_BATCH, BAND_ROWS, D_MODEL),
                    _bias_rows(a_rel_bias[0]), o)
    x = proj_residual(o, a_w_o[0].astype(BF16), ln_mix_post[0], x, tm=512)
    x = ffn(x, ln_ffn_pre[0], w1[0], w2[0], ln_ffn_post[0], tm=512, tf=1024)

    kv_p = kv[:T_PROMPT].reshape(BATCH, SEQ, 2, HEADS, HEAD_DIM)[:, SEQ - BAND_ROWS:]
    kv_s = kv[T_PROMPT:].reshape(DEC_BATCH, DEC_SEQ, 2, HEADS, HEAD_DIM)

    pad = jnp.zeros((D_MODEL, 128 - MLA_ROPE), F32)
    w_d = jnp.concatenate([mla_w_dq[0], mla_w_dkv[0], pad], axis=1).astype(BF16)
    wq = jnp.pad(mla_w_uq[0], ((0, 0), (0, 0), (0, MLA_QK - MLA_NOPE - MLA_ROPE)))
    wq = wq.reshape(MLA_RANK, HEADS * MLA_QK).astype(BF16)
    wuk = mla_w_uk[0].reshape(MLA_RANK, HEADS * MLA_NOPE).astype(BF16)
    wuv = mla_w_uv[0].reshape(MLA_RANK, HEADS * HEAD_DIM).astype(BF16)
    cos, sin = _rope_tables()

    d = norm_matmul(x, ln_mix_pre[1], w_d, tm=512, tn=w_d.shape[1], out_dtype=F32)
    ckv, kr, q, k, v = mla_proj(d, mla_q_norm[0], mla_kv_norm[0], wq, wuk, wuv, cos, sin, tm=256)
    o = mla_prompt(q, k, v)
    o = mla_sample(q, ckv, kr, cache_mla_ckv[0], cache_mla_kr[0], wuk, wuv, o)
    x = proj_residual(o, mla_w_o[0].astype(BF16), ln_mix_post[1], x, tm=512)
    x = ffn(x, ln_ffn_pre[1], w1[1], w2[1], ln_ffn_post[1], tm=512, tf=1024)

    return (
        x[:T_PROMPT].reshape(BATCH, SEQ, D_MODEL),
        x[T_PROMPT:].reshape(DEC_BATCH, DEC_SEQ, D_MODEL),
        kv_p[:, :, 0][None], kv_p[:, :, 1][None],
        kv_s[:, :, 0][None], kv_s[:, :, 1][None],
        ckv[:T_PROMPT].reshape(1, BATCH, SEQ, MLA_RANK),
        kr[:T_PROMPT].reshape(1, BATCH, SEQ, MLA_ROPE),
        ckv[T_PROMPT:].reshape(1, DEC_BATCH, DEC_SEQ, MLA_RANK),
        kr[T_PROMPT:].reshape(1, DEC_BATCH, DEC_SEQ, MLA_ROPE),
    )
```

```python
import functools
import math

import jax
import jax.numpy as jnp
from jax import lax
from jax.experimental import pallas as pl
from jax.experimental.pallas import tpu as pltpu

F32 = jnp.float32
BF16 = jnp.bfloat16

D_MODEL = 2048
BATCH = 4
SEQ = 4096
DEC_BATCH = 16
DEC_SEQ = 64
PAST_LEN = 2048
CHUNK = 64
HEADS = 16
HEAD_DIM = 128
BAND_ROWS = 512
REL_CLIP = 256
MLA_RANK = 512
MLA_NOPE = 128
MLA_ROPE = 64
MLA_QK = 256
ROPE_THETA = 10000.0
D_FF = 8192
EPS = 1e-6
NEG = -1e30
LOG2E = math.log2(math.e)

T_PROMPT = BATCH * SEQ
T_SAMPLE = DEC_BATCH * DEC_SEQ

VMEM_LIMIT_BYTES = 58 * 1024 * 1024

BAND_TQ = 256
BAND_TK = BAND_ROWS + BAND_TQ
BAND_F = 1024
BAND_NQ = SEQ // BAND_TQ
BAND_SUB = 64
BAND_Q_SCALE = HEAD_DIM ** -0.5 * LOG2E

MLA_TQ = 512
MLA_TK = 512
MLA_NQ = SEQ // MLA_TQ
MLA_SUB = 128
MLA_Q_SCALE = (MLA_NOPE + MLA_ROPE) ** -0.5 * LOG2E

QKV_TN = 1024
QKV_CHUNK = 256
PROJ_SUB = 128
FFN_CHUNK = 512

_NT = (((1,), (1,)), ((), ()))


def _params(*sem):
    return pltpu.CompilerParams(dimension_semantics=sem, vmem_limit_bytes=VMEM_LIMIT_BYTES)


def _rms(x, g):
    ms = jnp.mean(x * x, axis=-1, keepdims=True)
    return x * lax.rsqrt(ms + EPS) * g


def _norm_matmul_kernel(x_ref, g_ref, w_ref, o_ref):
    h = _rms(x_ref[...], g_ref[...]).astype(BF16)
    o_ref[...] = jnp.dot(h, w_ref[...], preferred_element_type=F32)


def norm_matmul(x, g, w, *, tm):
    m, k = x.shape
    n = w.shape[1]
    return pl.pallas_call(
        _norm_matmul_kernel,
        grid=(m // tm,),
        in_specs=[
            pl.BlockSpec((tm, k), lambda i: (i, 0)),
            pl.BlockSpec((1, k), lambda i: (0, 0)),
            pl.BlockSpec((k, n), lambda i: (0, 0)),
        ],
        out_specs=pl.BlockSpec((tm, n), lambda i: (i, 0)),
        out_shape=jax.ShapeDtypeStruct((m, n), F32),
        compiler_params=_params("arbitrary"),
        name="norm_matmul",
    )(x, g.reshape(1, k), w)


def _norm_qkv_kernel(x_ref, g_ref, w_ref, hm_ref, kf_ref, vf_ref, h_ref, *, keep):
    j = pl.program_id(1)

    @pl.when(j == 0)
    def _():
        h_ref[...] = _rms(x_ref[...], g_ref[...]).astype(BF16)

    h = h_ref[...]
    tm = h.shape[0]

    def project(scale, f_ref, f_col):
        for c in range(QKV_TN // QKV_CHUNK):
            cols = slice(c * QKV_CHUNK, (c + 1) * QKV_CHUNK)
            acc = jnp.dot(h, w_ref[:, cols], preferred_element_type=F32)
            if f_ref is not None:
                f_ref[:, f_col + c * QKV_CHUNK:f_col + (c + 1) * QKV_CHUNK] = acc[tm - keep:, :]
            if scale is not None:
                acc = acc * scale
            for hh in range(QKV_CHUNK // HEAD_DIM):
                hm_ref[c * (QKV_CHUNK // HEAD_DIM) + hh] = (
                    acc[:, hh * HEAD_DIM:(hh + 1) * HEAD_DIM].astype(BF16))

    for jj, f_ref in enumerate((kf_ref, kf_ref, vf_ref, vf_ref)):
        pl.when(j == jj)(functools.partial(project, None, f_ref, (jj % 2) * QKV_TN))

    pl.when(j >= 4)(functools.partial(project, BAND_Q_SCALE, None, 0))


def norm_qkv(x, g, w, *, tm, keep, tiles_per_keep):
    m, k = x.shape
    nb = 3 * D_MODEL // QKV_TN
    heads_per_block = QKV_TN // HEAD_DIM
    n_keep = m // (tm * tiles_per_keep) * keep

    def col(j):
        return (j + 2) % nb

    return pl.pallas_call(
        functools.partial(_norm_qkv_kernel, keep=keep),
        grid=(m // tm, nb),
        in_specs=[
            pl.BlockSpec((tm, k), lambda i, j: (i, 0)),
            pl.BlockSpec((1, k), lambda i, j: (0, 0)),
            pl.BlockSpec((k, QKV_TN), lambda i, j: (0, col(j))),
        ],
        out_specs=[
            pl.BlockSpec((heads_per_block, tm, HEAD_DIM), lambda i, j: (col(j), i, 0)),
            pl.BlockSpec((keep, D_MODEL), lambda i, j: (i // tiles_per_keep, 0)),
            pl.BlockSpec((keep, D_MODEL), lambda i, j: (i // tiles_per_keep, 0)),
        ],
        out_shape=[
            jax.ShapeDtypeStruct((3 * HEADS, m, HEAD_DIM), BF16),
            jax.ShapeDtypeStruct((n_keep, D_MODEL), F32),
            jax.ShapeDtypeStruct((n_keep, D_MODEL), F32),
        ],
        scratch_shapes=[pltpu.VMEM((tm, k), BF16)],
        compiler_params=_params("arbitrary", "arbitrary"),
        name="norm_qkv",
    )(x, g.reshape(1, k), w)


def _toeplitz_bias(f_row, rows):
    x = jnp.broadcast_to(f_row, (rows, BAND_F))
    return pltpu.roll(x, 0, 1, stride=1, stride_axis=0)


def _band_prompt_kernel(q_ref, k_ref, v_ref, f_ref, o_ref, kpad_ref, vpad_ref, bias_ref, s_ref, s2_ref):
    @pl.when((pl.program_id(0) == 0) & (pl.program_id(1) == 0))
    def _():
        kpad_ref[0:BAND_ROWS, :] = jnp.zeros((BAND_ROWS, HEAD_DIM), BF16)
        vpad_ref[0:BAND_ROWS, 0:HEAD_DIM] = jnp.zeros((BAND_ROWS, HEAD_DIM), BF16)
        vpad_ref[:, HEAD_DIM:] = jnp.ones((BAND_ROWS + SEQ, HEAD_DIM), BF16)

    @pl.when(pl.program_id(1) == 0)
    def _():
        r = lax.broadcasted_iota(jnp.int32, (BAND_TQ, BAND_TK), 0) // CHUNK
        j = lax.broadcasted_iota(jnp.int32, (BAND_TQ, BAND_TK), 1)
        jc = j // CHUNK
        allowed = (jc >= r) & (jc <= r + BAND_ROWS // CHUNK)
        base = jnp.where(allowed, _toeplitz_bias(f_ref[0], BAND_TQ)[:, :BAND_TK], NEG)
        bias_ref[2] = base
        bias_ref[1] = jnp.where(j >= BAND_ROWS - BAND_TQ, base, NEG)
        bias_ref[0] = jnp.where(j >= BAND_ROWS, base, NEG)

    kpad_ref[BAND_ROWS:, :] = k_ref[0]
    vpad_ref[BAND_ROWS:, 0:HEAD_DIM] = v_ref[0]

    def scores(qi, dst_ref):
        start = pl.multiple_of(qi * BAND_TQ, BAND_TQ)
        dst_ref[...] = lax.dot_general(q_ref[0, pl.ds(start, BAND_TQ), :],
                                       kpad_ref[pl.ds(start, BAND_TK), :], _NT,
                                       preferred_element_type=F32)

    scores(0, s_ref)

    def block(qi, cur_ref, nxt_ref):
        scores(jnp.minimum(qi + 1, BAND_NQ - 1), nxt_ref)
        which = jnp.minimum(qi, 2)
        start = pl.multiple_of(qi * BAND_TQ, BAND_TQ)
        vw = vpad_ref[pl.ds(start, BAND_TK), :]
        for u in range(BAND_TQ // BAND_SUB):
            rows = slice(u * BAND_SUB, (u + 1) * BAND_SUB)
            s = cur_ref[rows, :] + bias_ref[which, rows, :]
            p = jnp.exp2(s - jnp.max(s, axis=-1, keepdims=True))
            pv = jnp.dot(p.astype(BF16), vw, preferred_element_type=F32)
            o_ref[pl.ds(start + u * BAND_SUB, BAND_SUB), :] = (
                pv[:, :HEAD_DIM] / pv[:, HEAD_DIM:]).astype(BF16)

    def two_blocks(t, carry):
        block(2 * t, s_ref, s2_ref)
        block(2 * t + 1, s2_ref, s_ref)
        return carry

    lax.fori_loop(0, BAND_NQ // 2, two_blocks, 0)


def band_prompt(qkv_hm, f_rows):
    return pl.pallas_call(
        _band_prompt_kernel,
        grid=(HEADS, BATCH),
        in_specs=[
            pl.BlockSpec((1, SEQ, HEAD_DIM), lambda h, b: (h, b, 0)),
            pl.BlockSpec((1, SEQ, HEAD_DIM), lambda h, b: (HEADS + h, b, 0)),
            pl.BlockSpec((1, SEQ, HEAD_DIM), lambda h, b: (2 * HEADS + h, b, 0)),
            pl.BlockSpec((1, 1, BAND_F), lambda h, b: (h, 0, 0)),
        ],
        out_specs=pl.BlockSpec((SEQ, HEAD_DIM), lambda h, b: (b, h)),
        out_shape=jax.ShapeDtypeStruct((T_PROMPT, D_MODEL), BF16),
        scratch_shapes=[
            pltpu.VMEM((BAND_ROWS + SEQ, HEAD_DIM), BF16),
            pltpu.VMEM((BAND_ROWS + SEQ, 2 * HEAD_DIM), BF16),
            pltpu.VMEM((3, BAND_TQ, BAND_TK), F32),
            pltpu.VMEM((BAND_TQ, BAND_TK), F32),
            pltpu.VMEM((BAND_TQ, BAND_TK), F32),
        ],
        compiler_params=_params("arbitrary", "arbitrary"),
        name="band_prompt",
    )(qkv_hm, qkv_hm, qkv_hm, f_rows)


def _band_sample_kernel(q_ref, kn_ref, vn_ref, ck_ref, cv_ref, f_ref, o_ref):
    for h in range(HEADS):
        cols = slice(h * HEAD_DIM, (h + 1) * HEAD_DIM)
        bias = _toeplitz_bias(f_ref[h], DEC_SEQ)
        q = q_ref[h]
        kc = ck_ref[0, :, cols].astype(BF16)
        vc = cv_ref[0, :, cols].astype(BF16)
        s_c = lax.dot_general(q, kc, _NT, preferred_element_type=F32) + bias[:, :BAND_ROWS]
        s_n = (lax.dot_general(q, kn_ref[h], _NT, preferred_element_type=F32)
               + bias[:, BAND_ROWS:BAND_ROWS + DEC_SEQ])
        m = jnp.maximum(jnp.max(s_c, axis=-1, keepdims=True), jnp.max(s_n, axis=-1, keepdims=True))
        p_c = jnp.exp2(s_c - m)
        p_n = jnp.exp2(s_n - m)
        l = jnp.sum(p_c, axis=-1, keepdims=True) + jnp.sum(p_n, axis=-1, keepdims=True)
        o = (jnp.dot(p_c.astype(BF16), vc, preferred_element_type=F32)
             + jnp.dot(p_n.astype(BF16), vn_ref[h], preferred_element_type=F32)) / l
        o_ref[:, cols] = o.astype(BF16)


def band_sample(qkv_hm, cache_k, cache_v, f_rows):
    return pl.pallas_call(
        _band_sample_kernel,
        grid=(DEC_BATCH,),
        in_specs=[
            pl.BlockSpec((HEADS, DEC_SEQ, HEAD_DIM), lambda s: (0, s, 0)),
            pl.BlockSpec((HEADS, DEC_SEQ, HEAD_DIM), lambda s: (1, s, 0)),
            pl.BlockSpec((HEADS, DEC_SEQ, HEAD_DIM), lambda s: (2, s, 0)),
            pl.BlockSpec((1, BAND_ROWS, D_MODEL), lambda s: (s, 0, 0)),
            pl.BlockSpec((1, BAND_ROWS, D_MODEL), lambda s: (s, 0, 0)),
            pl.BlockSpec((HEADS, 1, BAND_F), lambda s: (0, 0, 0)),
        ],
        out_specs=pl.BlockSpec((DEC_SEQ, D_MODEL), lambda s: (s, 0)),
        out_shape=jax.ShapeDtypeStruct((T_SAMPLE, D_MODEL), BF16),
        compiler_params=_params("arbitrary"),
        name="band_sample",
    )(qkv_hm, qkv_hm, qkv_hm, cache_k, cache_v, f_rows)


def _proj_residual_kernel(a_ref, w_ref, g_ref, x_ref, o_ref):
    w = w_ref[...]
    g = g_ref[...]
    for u in range(a_ref.shape[0] // PROJ_SUB):
        rows = slice(u * PROJ_SUB, (u + 1) * PROJ_SUB)
        y = jnp.dot(a_ref[rows, :], w, preferred_element_type=F32)
        o_ref[rows, :] = x_ref[rows, :] + _rms(y, g)


def proj_residual(a, w, g, x, *, tm):
    m, k = a.shape
    n = w.shape[1]
    return pl.pallas_call(
        _proj_residual_kernel,
        grid=(m // tm,),
        in_specs=[
            pl.BlockSpec((tm, k), lambda i: (i, 0)),
            pl.BlockSpec((k, n), lambda i: (0, 0)),
            pl.BlockSpec((1, n), lambda i: (0, 0)),
            pl.BlockSpec((tm, n), lambda i: (i, 0)),
        ],
        out_specs=pl.BlockSpec((tm, n), lambda i: (i, 0)),
        out_shape=jax.ShapeDtypeStruct((m, n), F32),
        compiler_params=_params("arbitrary"),
        name="proj_residual",
    )(a, w, g.reshape(1, n), x)


def _ffn_kernel(x_ref, g1_ref, w1_ref, w2_ref, g2_ref, o_ref, h_ref):
    j = pl.program_id(1)

    @pl.when(j == 0)
    def _():
        h_ref[...] = _rms(x_ref[...], g1_ref[...]).astype(BF16)
        o_ref[...] = jnp.zeros(o_ref.shape, F32)

    a = jnp.maximum(jnp.dot(h_ref[...], w1_ref[...], preferred_element_type=F32), 0.0)
    a = (a * a).astype(BF16)
    for c in range(o_ref.shape[1] // FFN_CHUNK):
        cols = slice(c * FFN_CHUNK, (c + 1) * FFN_CHUNK)
        o_ref[:, cols] += jnp.dot(a, w2_ref[:, cols], preferred_element_type=F32)

    @pl.when(j == pl.num_programs(1) - 1)
    def _():
        o_ref[...] = x_ref[...] + _rms(o_ref[...], g2_ref[...])


def ffn(x, g1, w1, w2, g2, *, tm, tf):
    m, d = x.shape
    f = w1.shape[1]
    return pl.pallas_call(
        _ffn_kernel,
        grid=(m // tm, f // tf),
        in_specs=[
            pl.BlockSpec((tm, d), lambda i, j: (i, 0)),
            pl.BlockSpec((1, d), lambda i, j: (0, 0)),
            pl.BlockSpec((d, tf), lambda i, j: (0, j)),
            pl.BlockSpec((tf, d), lambda i, j: (j, 0)),
            pl.BlockSpec((1, d), lambda i, j: (0, 0)),
        ],
        out_specs=pl.BlockSpec((tm, d), lambda i, j: (i, 0)),
        out_shape=jax.ShapeDtypeStruct((m, d), F32),
        scratch_shapes=[pltpu.VMEM((tm, d), BF16)],
        compiler_params=_params("arbitrary", "arbitrary"),
        name="ffn",
    )(x, g1.reshape(1, d), w1, w2, g2.reshape(1, d))


def _rope128(x, cos, sin):
    lane = lax.broadcasted_iota(jnp.int32, x.shape, 1)
    half = MLA_ROPE // 2
    swapped = jnp.where(lane < half, pltpu.roll(x, 128 - half, 1), pltpu.roll(x, half, 1))
    return x * cos + swapped * sin


def _mla_proj_kernel(d_ref, qn_ref, kvn_ref, wq_ref, wuk_ref, wuv_ref, cos_ref, sin_ref,
                     ckv_ref, kr_ref, q_ref, *kv_refs):
    d = d_ref[...]
    cos = cos_ref[...]
    sin = sin_ref[...]
    cq = _rms(d[:, :MLA_RANK], qn_ref[...]).astype(BF16)
    ckv = _rms(d[:, MLA_RANK:2 * MLA_RANK], kvn_ref[...])
    ckv_ref[...] = ckv
    kr = _rope128(d[:, 2 * MLA_RANK:], cos, sin)
    kr_ref[...] = kr[:, :MLA_ROPE]

    q = jnp.dot(cq, wq_ref[...], preferred_element_type=F32) * MLA_Q_SCALE
    for h in range(HEADS):
        base = h * MLA_QK
        q_ref[h, :, 0:MLA_NOPE] = q[:, base:base + MLA_NOPE].astype(BF16)
        q_ref[h, :, MLA_NOPE:] = _rope128(q[:, base + MLA_NOPE:base + MLA_QK], cos, sin).astype(BF16)

    if kv_refs:
        k_ref, v_ref = kv_refs
        kr_b = kr.astype(BF16)
        ckv_b = ckv.astype(BF16)
        kn = jnp.dot(ckv_b, wuk_ref[...], preferred_element_type=F32)
        vv = jnp.dot(ckv_b, wuv_ref[...], preferred_element_type=F32)
        ones = jnp.ones((d.shape[0], HEAD_DIM), BF16)
        for h in range(HEADS):
            k_ref[h, :, 0:MLA_NOPE] = kn[:, h * MLA_NOPE:(h + 1) * MLA_NOPE].astype(BF16)
            k_ref[h, :, MLA_NOPE:] = kr_b
            v_ref[h, :, 0:HEAD_DIM] = vv[:, h * HEAD_DIM:(h + 1) * HEAD_DIM].astype(BF16)
            v_ref[h, :, HEAD_DIM:] = ones


def mla_proj(d, q_norm, kv_norm, wq, wuk, wuv, cos, sin, *, tm, with_kv):
    m = d.shape[0]
    full = lambda i: (0, 0)
    rope_tiles = cos.shape[0] // tm
    hm_spec = pl.BlockSpec((HEADS, tm, MLA_QK), lambda i: (0, i, 0))
    hm_shape = jax.ShapeDtypeStruct((HEADS, m, MLA_QK), BF16)
    n_hm = 3 if with_kv else 1
    return pl.pallas_call(
        _mla_proj_kernel,
        grid=(m // tm,),
        in_specs=[
            pl.BlockSpec((tm, d.shape[1]), lambda i: (i, 0)),
            pl.BlockSpec((1, MLA_RANK), full),
            pl.BlockSpec((1, MLA_RANK), full),
            pl.BlockSpec(wq.shape, full),
            pl.BlockSpec(wuk.shape, full),
            pl.BlockSpec(wuv.shape, full),
            pl.BlockSpec((tm, 128), lambda i: (i % rope_tiles, 0)),
            pl.BlockSpec((tm, 128), lambda i: (i % rope_tiles, 0)),
        ],
        out_specs=[
            pl.BlockSpec((tm, MLA_RANK), lambda i: (i, 0)),
            pl.BlockSpec((tm, MLA_ROPE), lambda i: (i, 0)),
        ] + [hm_spec] * n_hm,
        out_shape=[
            jax.ShapeDtypeStruct((m, MLA_RANK), F32),
            jax.ShapeDtypeStruct((m, MLA_ROPE), F32),
        ] + [hm_shape] * n_hm,
        compiler_params=_params("arbitrary"),
        name="mla_proj",
    )(d, q_norm.reshape(1, -1), kv_norm.reshape(1, -1), wq, wuk, wuv, cos, sin)


def _mla_prompt_kernel(q_ref, k_ref, v_ref, o_ref, s_ref, s2_ref, bias_ref, m_ref, acc_ref):
    @pl.when((pl.program_id(0) == 0) & (pl.program_id(1) == 0))
    def _():
        rc = lax.broadcasted_iota(jnp.int32, (MLA_TQ, MLA_TK), 0) // CHUNK
        cc = lax.broadcasted_iota(jnp.int32, (MLA_TQ, MLA_TK), 1) // CHUNK
        bias_ref[0] = jnp.zeros((MLA_TQ, MLA_TK), F32)
        bias_ref[1] = jnp.where(cc <= rc, 0.0, NEG)

    def scores(qi, kj, dst_ref):
        q = q_ref[0, pl.ds(pl.multiple_of(qi * MLA_TQ, MLA_TQ), MLA_TQ), :]
        k = k_ref[0, pl.ds(pl.multiple_of(kj * MLA_TK, MLA_TK), MLA_TK), :]
        dst_ref[...] = lax.dot_general(q, k, _NT, preferred_element_type=F32)

    scores(0, 0, s_ref)

    def pair(carry, cur_ref, nxt_ref):
        qi, kj = carry
        diag = kj == qi
        qn = jnp.where(diag, qi + 1, qi)
        kn = jnp.where(diag, 0, kj + 1)
        scores(jnp.minimum(qn, MLA_NQ - 1), kn, nxt_ref)

        first = kj == 0
        which = diag.astype(jnp.int32)
        q0 = pl.multiple_of(qi * MLA_TQ, MLA_TQ)
        v = v_ref[0, pl.ds(pl.multiple_of(kj * MLA_TK, MLA_TK), MLA_TK), :]
        for u in range(MLA_TQ // MLA_SUB):
            rows = slice(u * MLA_SUB, (u + 1) * MLA_SUB)
            s = cur_ref[rows, :] + bias_ref[which, rows, :]
            m_prev = jnp.where(first, NEG, m_ref[rows, :])
            acc_prev = jnp.where(first, 0.0, acc_ref[rows, :])
            m_new = jnp.maximum(m_prev, jnp.max(s, axis=-1, keepdims=True))
            alpha = jnp.exp2(m_prev - m_new)
            p = jnp.exp2(s - jnp.tile(m_new, (1, MLA_TK // 128)))
            acc = jnp.tile(alpha, (1, 2)) * acc_prev + jnp.dot(
                p.astype(BF16), v, preferred_element_type=F32)
            acc_ref[rows, :] = acc
            m_ref[rows, :] = m_new
            o_ref[pl.ds(q0 + u * MLA_SUB, MLA_SUB), :] = (
                acc[:, :HEAD_DIM] / acc[:, HEAD_DIM:]).astype(BF16)
        return qn, kn

    def two_pairs(t, carry):
        return pair(pair(carry, s_ref, s2_ref), s2_ref, s_ref)

    n_pairs = MLA_NQ * (MLA_NQ + 1) // 2
    assert n_pairs % 2 == 0
    lax.fori_loop(0, n_pairs // 2, two_pairs, (jnp.int32(0), jnp.int32(0)))


def mla_prompt(q, k, v):
    return pl.pallas_call(
        _mla_prompt_kernel,
        grid=(BATCH, HEADS),
        in_specs=[
            pl.BlockSpec((1, SEQ, MLA_QK), lambda b, h: (h, b, 0)),
            pl.BlockSpec((1, SEQ, MLA_QK), lambda b, h: (h, b, 0)),
            pl.BlockSpec((1, SEQ, 2 * HEAD_DIM), lambda b, h: (h, b, 0)),
        ],
        out_specs=pl.BlockSpec((SEQ, HEAD_DIM), lambda b, h: (b, h)),
        out_shape=jax.ShapeDtypeStruct((T_PROMPT, D_MODEL), BF16),
        scratch_shapes=[
            pltpu.VMEM((MLA_TQ, MLA_TK), F32),
            pltpu.VMEM((MLA_TQ, MLA_TK), F32),
            pltpu.VMEM((2, MLA_TQ, MLA_TK), F32),
            pltpu.VMEM((MLA_TQ, 128), F32),
            pltpu.VMEM((MLA_TQ, 2 * HEAD_DIM), F32),
        ],
        compiler_params=_params("arbitrary", "arbitrary"),
        name="mla_prompt",
    )(q, k, v)


def _mla_sample_kernel(q_ref, cn_ref, rn_ref, cc_ref, cr_ref, wuk_ref, wuv_ref, o_ref,
                       ql_ref, qr_ref):
    for h in range(HEADS):
        rows = slice(h * DEC_SEQ, (h + 1) * DEC_SEQ)
        ql_ref[rows, :] = lax.dot_general(
            q_ref[h, :, 0:MLA_NOPE], wuk_ref[:, h * MLA_NOPE:(h + 1) * MLA_NOPE], _NT,
            preferred_element_type=F32).astype(BF16)
        qr_ref[rows, :] = q_ref[h, :, MLA_NOPE:]
    ql = ql_ref[...]
    qr = qr_ref[...][:, :MLA_ROPE]
    cc = cc_ref[0].astype(BF16)
    cr = cr_ref[0].astype(BF16)
    cn = cn_ref[...].astype(BF16)
    rn = rn_ref[...].astype(BF16)
    s_c = (lax.dot_general(ql, cc, _NT, preferred_element_type=F32)
           + lax.dot_general(qr, cr, _NT, preferred_element_type=F32))
    s_n = (lax.dot_general(ql, cn, _NT, preferred_element_type=F32)
           + lax.dot_general(qr, rn, _NT, preferred_element_type=F32))
    m = jnp.maximum(jnp.max(s_c, axis=-1, keepdims=True), jnp.max(s_n, axis=-1, keepdims=True))
    p_c = jnp.exp2(s_c - m)
    p_n = jnp.exp2(s_n - m)
    l = jnp.sum(p_c, axis=-1, keepdims=True) + jnp.sum(p_n, axis=-1, keepdims=True)
    o_lat = ((jnp.dot(p_c.astype(BF16), cc, preferred_element_type=F32)
              + jnp.dot(p_n.astype(BF16), cn, preferred_element_type=F32)) / l).astype(BF16)
    for h in range(HEADS):
        cols = slice(h * HEAD_DIM, (h + 1) * HEAD_DIM)
        o_ref[:, cols] = jnp.dot(o_lat[h * DEC_SEQ:(h + 1) * DEC_SEQ], wuv_ref[:, cols],
                                 preferred_element_type=F32).astype(BF16)


def mla_sample(q, ckv, kr, cache_ckv, cache_kr, wuk, wuv):
    full = lambda s: (0, 0)
    return pl.pallas_call(
        _mla_sample_kernel,
        grid=(DEC_BATCH,),
        in_specs=[
            pl.BlockSpec((HEADS, DEC_SEQ, MLA_QK), lambda s: (0, s, 0)),
            pl.BlockSpec((DEC_SEQ, MLA_RANK), lambda s: (s, 0)),
            pl.BlockSpec((DEC_SEQ, MLA_ROPE), lambda s: (s, 0)),
            pl.BlockSpec((1, PAST_LEN, MLA_RANK), lambda s: (s, 0, 0)),
            pl.BlockSpec((1, PAST_LEN, MLA_ROPE), lambda s: (s, 0, 0)),
            pl.BlockSpec(wuk.shape, full),
            pl.BlockSpec(wuv.shape, full),
        ],
        out_specs=pl.BlockSpec((DEC_SEQ, D_MODEL), lambda s: (s, 0)),
        out_shape=jax.ShapeDtypeStruct((T_SAMPLE, D_MODEL), BF16),
        scratch_shapes=[
            pltpu.VMEM((HEADS * DEC_SEQ, MLA_RANK), BF16),
            pltpu.VMEM((HEADS * DEC_SEQ, 128), BF16),
        ],
        compiler_params=_params("arbitrary"),
        name="mla_sample",
    )(q, ckv, kr, cache_ckv, cache_kr, wuk, wuv)


def _bias_rows(rel_bias):
    far = jnp.broadcast_to(rel_bias[:, 2 * REL_CLIP:], (HEADS, BAND_F // 4))
    mid = rel_bias[:, 1:][:, ::-1]
    return (jnp.concatenate([far, mid, far], axis=1) * LOG2E).reshape(HEADS, 1, BAND_F)


def _rope_tables(pos):
    half = MLA_ROPE // 2
    inv = 1.0 / (ROPE_THETA ** (jnp.arange(half, dtype=F32) / half))
    ang = pos.astype(F32)[:, None] * inv[None, :]
    c, s = jnp.cos(ang), jnp.sin(ang)
    z = jnp.zeros((pos.shape[0], 128 - MLA_ROPE), F32)
    return jnp.concatenate([c, c, z], axis=1), jnp.concatenate([-s, s, z], axis=1)


PROMPT_TM_QKV = 1024
MLA_PROJ_TM = 256


def kernel(x_prompt, x_sample, cache_a_k, cache_a_v, cache_mla_ckv, cache_mla_kr, ln_mix_pre, ln_mix_post, ln_ffn_pre, ln_ffn_post, a_w_qkv, a_w_o, a_rel_bias, mla_w_dq, mla_q_norm, mla_w_uq, mla_w_dkv, mla_kv_norm, mla_w_uk, mla_w_uv, mla_w_o, ffn_w1, ffn_w2):
    xp = x_prompt.reshape(T_PROMPT, D_MODEL)
    xs = x_sample.reshape(T_SAMPLE, D_MODEL)
    w1 = ffn_w1.astype(BF16)
    w2 = ffn_w2.astype(BF16)

    def mix_out_and_ffn(i, o, w_o, x):
        x = proj_residual(o, w_o, ln_mix_post[i], x, tm=512)
        return ffn(x, ln_ffn_pre[i], w1[i], w2[i], ln_ffn_post[i], tm=512, tf=1024)

    w_qkv = a_w_qkv[0].astype(BF16)
    w_o = a_w_o[0].astype(BF16)
    f_rows = _bias_rows(a_rel_bias[0])
    hm_p, k_p, v_p = norm_qkv(xp, ln_mix_pre[0], w_qkv, tm=PROMPT_TM_QKV, keep=BAND_ROWS,
                              tiles_per_keep=SEQ // PROMPT_TM_QKV)
    hm_s, k_s, v_s = norm_qkv(xs, ln_mix_pre[0], w_qkv, tm=512, keep=512, tiles_per_keep=1)
    o_p = band_prompt(hm_p, f_rows)
    o_s = band_sample(hm_s,
                      cache_a_k[0].reshape(DEC_BATCH, BAND_ROWS, D_MODEL),
                      cache_a_v[0].reshape(DEC_BATCH, BAND_ROWS, D_MODEL), f_rows)
    xp = mix_out_and_ffn(0, o_p, w_o, xp)
    xs = mix_out_and_ffn(0, o_s, w_o, xs)

    pad = jnp.zeros((D_MODEL, 128 - MLA_ROPE), F32)
    w_d = jnp.concatenate([mla_w_dq[0], mla_w_dkv[0], pad], axis=1).astype(BF16)
    wq = jnp.pad(mla_w_uq[0], ((0, 0), (0, 0), (0, MLA_QK - MLA_NOPE - MLA_ROPE)))
    wq = wq.reshape(MLA_RANK, HEADS * MLA_QK).astype(BF16)
    wuk = mla_w_uk[0].reshape(MLA_RANK, HEADS * MLA_NOPE).astype(BF16)
    wuv = mla_w_uv[0].reshape(MLA_RANK, HEADS * HEAD_DIM).astype(BF16)
    w_o = mla_w_o[0].astype(BF16)
    cos_p, sin_p = _rope_tables(jnp.arange(SEQ))
    cos_s, sin_s = _rope_tables(jnp.tile(PAST_LEN + jnp.arange(DEC_SEQ), MLA_PROJ_TM // DEC_SEQ))

    d_p = norm_matmul(xp, ln_mix_pre[1], w_d, tm=512)
    d_s = norm_matmul(xs, ln_mix_pre[1], w_d, tm=512)
    ckv_p, kr_p, q_p, kk_p, vv_p = mla_proj(d_p, mla_q_norm[0], mla_kv_norm[0], wq, wuk, wuv,
                                            cos_p, sin_p, tm=MLA_PROJ_TM, with_kv=True)
    ckv_s, kr_s, q_s = mla_proj(d_s, mla_q_norm[0], mla_kv_norm[0], wq, wuk, wuv,
                                cos_s, sin_s, tm=MLA_PROJ_TM, with_kv=False)
    o_p = mla_prompt(q_p, kk_p, vv_p)
    o_s = mla_sample(q_s, ckv_s, kr_s, cache_mla_ckv[0], cache_mla_kr[0], wuk, wuv)
    xp = mix_out_and_ffn(1, o_p, w_o, xp)
    xs = mix_out_and_ffn(1, o_s, w_o, xs)

    return (
        xp.reshape(BATCH, SEQ, D_MODEL),
        xs.reshape(DEC_BATCH, DEC_SEQ, D_MODEL),
        k_p.reshape(1, BATCH, BAND_ROWS, HEADS, HEAD_DIM),
        v_p.reshape(1, BATCH, BAND_ROWS, HEADS, HEAD_DIM),
        k_s.reshape(1, DEC_BATCH, DEC_SEQ, HEADS, HEAD_DIM),
        v_s.reshape(1, DEC_BATCH, DEC_SEQ, HEADS, HEAD_DIM),
        ckv_p.reshape(1, BATCH, SEQ, MLA_RANK),
        kr_p.reshape(1, BATCH, SEQ, MLA_ROPE),
        ckv_s.reshape(1, DEC_BATCH, DEC_SEQ, MLA_RANK),
        kr_s.reshape(1, DEC_BATCH, DEC_SEQ, MLA_ROPE),
    )
```

```python
import functools
import math

import jax
import jax.numpy as jnp
from jax import lax
from jax.experimental import pallas as pl
from jax.experimental.pallas import tpu as pltpu

F32 = jnp.float32
BF16 = jnp.bfloat16

D_MODEL = 2048
BATCH = 4
SEQ = 4096
DEC_BATCH = 16
DEC_SEQ = 64
PAST_LEN = 2048
CHUNK = 64
HEADS = 16
HEAD_DIM = 128
BAND_ROWS = 512
REL_CLIP = 256
MLA_RANK = 512
MLA_NOPE = 128
MLA_ROPE = 64
MLA_QK = 256
ROPE_THETA = 10000.0
D_FF = 8192
EPS = 1e-6
NEG = -1e30
LOG2E = math.log2(math.e)

T_PROMPT = BATCH * SEQ
T_SAMPLE = DEC_BATCH * DEC_SEQ

VMEM_LIMIT_BYTES = 58 * 1024 * 1024

BAND_TQ = 256
BAND_TK = BAND_ROWS + BAND_TQ
BAND_F = 1024
BAND_NQ = SEQ // BAND_TQ
BAND_SUB = 64
BAND_Q_SCALE = HEAD_DIM ** -0.5 * LOG2E

MLA_TQ = 512
MLA_TK = 512
MLA_NQ = SEQ // MLA_TQ
MLA_SUB = 128
MLA_Q_SCALE = (MLA_NOPE + MLA_ROPE) ** -0.5 * LOG2E

QKV_TN = 1024
QKV_CHUNK = 256
PROJ_SUB = 128
FFN_CHUNK = 512

_NT = (((1,), (1,)), ((), ()))


def _params(*sem):
    return pltpu.CompilerParams(dimension_semantics=sem, vmem_limit_bytes=VMEM_LIMIT_BYTES)


def _rms(x, g):
    ms = jnp.mean(x * x, axis=-1, keepdims=True)
    return x * lax.rsqrt(ms + EPS) * g


def _norm_matmul_kernel(x_ref, g_ref, w_ref, o_ref):
    h = _rms(x_ref[...], g_ref[...]).astype(BF16)
    o_ref[...] = jnp.dot(h, w_ref[...], preferred_element_type=F32)


def norm_matmul(x, g, w, *, tm):
    m, k = x.shape
    n = w.shape[1]
    return pl.pallas_call(
        _norm_matmul_kernel,
        grid=(m // tm,),
        in_specs=[
            pl.BlockSpec((tm, k), lambda i: (i, 0)),
            pl.BlockSpec((1, k), lambda i: (0, 0)),
            pl.BlockSpec((k, n), lambda i: (0, 0)),
        ],
        out_specs=pl.BlockSpec((tm, n), lambda i: (i, 0)),
        out_shape=jax.ShapeDtypeStruct((m, n), F32),
        compiler_params=_params("arbitrary"),
        name="norm_matmul",
    )(x, g.reshape(1, k), w)


def _norm_qkv_kernel(x_ref, g_ref, w_ref, hm_ref, kf_ref, vf_ref, h_ref, *, keep):
    j = pl.program_id(1)

    @pl.when(j == 0)
    def _():
        h_ref[...] = _rms(x_ref[...], g_ref[...]).astype(BF16)

    h = h_ref[...]
    tm = h.shape[0]

    def project(scale, f_ref, f_col):
        for c in range(QKV_TN // QKV_CHUNK):
            cols = slice(c * QKV_CHUNK, (c + 1) * QKV_CHUNK)
            acc = jnp.dot(h, w_ref[:, cols], preferred_element_type=F32)
            if f_ref is not None:
                f_ref[:, f_col + c * QKV_CHUNK:f_col + (c + 1) * QKV_CHUNK] = acc[tm - keep:, :]
            if scale is not None:
                acc = acc * scale
            for hh in range(QKV_CHUNK // HEAD_DIM):
                hm_ref[c * (QKV_CHUNK // HEAD_DIM) + hh] = (
                    acc[:, hh * HEAD_DIM:(hh + 1) * HEAD_DIM].astype(BF16))

    for jj, f_ref in enumerate((kf_ref, kf_ref, vf_ref, vf_ref)):
        pl.when(j == jj)(functools.partial(project, None, f_ref, (jj % 2) * QKV_TN))

    pl.when(j >= 4)(functools.partial(project, BAND_Q_SCALE, None, 0))


def norm_qkv(x, g, w, *, tm, keep, tiles_per_keep):
    m, k = x.shape
    nb = 3 * D_MODEL // QKV_TN
    heads_per_block = QKV_TN // HEAD_DIM
    n_keep = m // (tm * tiles_per_keep) * keep

    def col(j):
        return (j + 2) % nb

    return pl.pallas_call(
        functools.partial(_norm_qkv_kernel, keep=keep),
        grid=(m // tm, nb),
        in_specs=[
            pl.BlockSpec((tm, k), lambda i, j: (i, 0)),
            pl.BlockSpec((1, k), lambda i, j: (0, 0)),
            pl.BlockSpec((k, QKV_TN), lambda i, j: (0, col(j))),
        ],
        out_specs=[
            pl.BlockSpec((heads_per_block, tm, HEAD_DIM), lambda i, j: (col(j), i, 0)),
            pl.BlockSpec((keep, D_MODEL), lambda i, j: (i // tiles_per_keep, 0)),
            pl.BlockSpec((keep, D_MODEL), lambda i, j: (i // tiles_per_keep, 0)),
        ],
        out_shape=[
            jax.ShapeDtypeStruct((3 * HEADS, m, HEAD_DIM), BF16),
            jax.ShapeDtypeStruct((n_keep, D_MODEL), F32),
            jax.ShapeDtypeStruct((n_keep, D_MODEL), F32),
        ],
        scratch_shapes=[pltpu.VMEM((tm, k), BF16)],
        compiler_params=_params("arbitrary", "arbitrary"),
        name="norm_qkv",
    )(x, g.reshape(1, k), w)


def _toeplitz_bias(f_row, rows):
    x = jnp.broadcast_to(f_row, (rows, BAND_F))
    return pltpu.roll(x, 0, 1, stride=1, stride_axis=0)


def _band_prompt_kernel(q_ref, k_ref, v_ref, f_ref, o_ref, kpad_ref, vpad_ref, bias_ref, s_ref, s2_ref):
    @pl.when((pl.program_id(0) == 0) & (pl.program_id(1) == 0))
    def _():
        kpad_ref[0:BAND_ROWS, :] = jnp.zeros((BAND_ROWS, HEAD_DIM), BF16)
        vpad_ref[0:BAND_ROWS, 0:HEAD_DIM] = jnp.zeros((BAND_ROWS, HEAD_DIM), BF16)
        vpad_ref[:, HEAD_DIM:] = jnp.ones((BAND_ROWS + SEQ, HEAD_DIM), BF16)

    @pl.when(pl.program_id(1) == 0)
    def _():
        r = lax.broadcasted_iota(jnp.int32, (BAND_TQ, BAND_TK), 0) // CHUNK
        j = lax.broadcasted_iota(jnp.int32, (BAND_TQ, BAND_TK), 1)
        jc = j // CHUNK
        allowed = (jc >= r) & (jc <= r + BAND_ROWS // CHUNK)
        base = jnp.where(allowed, _toeplitz_bias(f_ref[0], BAND_TQ)[:, :BAND_TK], NEG)
        bias_ref[2] = base
        bias_ref[1] = jnp.where(j >= BAND_ROWS - BAND_TQ, base, NEG)
        bias_ref[0] = jnp.where(j >= BAND_ROWS, base, NEG)

    kpad_ref[BAND_ROWS:, :] = k_ref[0]
    vpad_ref[BAND_ROWS:, 0:HEAD_DIM] = v_ref[0]

    def scores(qi, dst_ref):
        start = qi * BAND_TQ
        dst_ref[...] = lax.dot_general(q_ref[0, start:start + BAND_TQ, :],
                                       kpad_ref[start:start + BAND_TK, :], _NT,
                                       preferred_element_type=F32)

    def softmax_pv(qi, cur_ref):
        start = qi * BAND_TQ
        vw = vpad_ref[start:start + BAND_TK, :]
        for u in range(BAND_TQ // BAND_SUB):
            rows = slice(u * BAND_SUB, (u + 1) * BAND_SUB)
            s = cur_ref[rows, :] + bias_ref[min(qi, 2), rows, :]
            p = jnp.exp2(s - jnp.max(s, axis=-1, keepdims=True))
            pv = jnp.dot(p.astype(BF16), vw, preferred_element_type=F32)
            o_ref[start + u * BAND_SUB:start + (u + 1) * BAND_SUB, :] = (
                pv[:, :HEAD_DIM] / pv[:, HEAD_DIM:]).astype(BF16)

    bufs = (s_ref, s2_ref)
    scores(0, s_ref)
    for qi in range(BAND_NQ):
        if qi + 1 < BAND_NQ:
            scores(qi + 1, bufs[(qi + 1) % 2])
        softmax_pv(qi, bufs[qi % 2])


def band_prompt(qkv_hm, f_rows):
    return pl.pallas_call(
        _band_prompt_kernel,
        grid=(HEADS, BATCH),
        in_specs=[
            pl.BlockSpec((1, SEQ, HEAD_DIM), lambda h, b: (h, b, 0)),
            pl.BlockSpec((1, SEQ, HEAD_DIM), lambda h, b: (HEADS + h, b, 0)),
            pl.BlockSpec((1, SEQ, HEAD_DIM), lambda h, b: (2 * HEADS + h, b, 0)),
            pl.BlockSpec((1, 1, BAND_F), lambda h, b: (h, 0, 0)),
        ],
        out_specs=pl.BlockSpec((SEQ, HEAD_DIM), lambda h, b: (b, h)),
        out_shape=jax.ShapeDtypeStruct((T_PROMPT, D_MODEL), BF16),
        scratch_shapes=[
            pltpu.VMEM((BAND_ROWS + SEQ, HEAD_DIM), BF16),
            pltpu.VMEM((BAND_ROWS + SEQ, 2 * HEAD_DIM), BF16),
            pltpu.VMEM((3, BAND_TQ, BAND_TK), F32),
            pltpu.VMEM((BAND_TQ, BAND_TK), F32),
            pltpu.VMEM((BAND_TQ, BAND_TK), F32),
        ],
        compiler_params=_params("arbitrary", "arbitrary"),
        name="band_prompt",
    )(qkv_hm, qkv_hm, qkv_hm, f_rows)


def _band_sample_kernel(q_ref, kn_ref, vn_ref, ck_ref, cv_ref, f_ref, o_ref):
    for h in range(HEADS):
        cols = slice(h * HEAD_DIM, (h + 1) * HEAD_DIM)
        bias = _toeplitz_bias(f_ref[h], DEC_SEQ)
        q = q_ref[h]
        kc = ck_ref[0, :, cols].astype(BF16)
        vc = cv_ref[0, :, cols].astype(BF16)
        s_c = lax.dot_general(q, kc, _NT, preferred_element_type=F32) + bias[:, :BAND_ROWS]
        s_n = (lax.dot_general(q, kn_ref[h], _NT, preferred_element_type=F32)
               + bias[:, BAND_ROWS:BAND_ROWS + DEC_SEQ])
        m = jnp.maximum(jnp.max(s_c, axis=-1, keepdims=True), jnp.max(s_n, axis=-1, keepdims=True))
        p_c = jnp.exp2(s_c - m)
        p_n = jnp.exp2(s_n - m)
        l = jnp.sum(p_c, axis=-1, keepdims=True) + jnp.sum(p_n, axis=-1, keepdims=True)
        o = (jnp.dot(p_c.astype(BF16), vc, preferred_element_type=F32)
             + jnp.dot(p_n.astype(BF16), vn_ref[h], preferred_element_type=F32)) / l
        o_ref[:, cols] = o.astype(BF16)


def band_sample(qkv_hm, cache_k, cache_v, f_rows):
    return pl.pallas_call(
        _band_sample_kernel,
        grid=(DEC_BATCH,),
        in_specs=[
            pl.BlockSpec((HEADS, DEC_SEQ, HEAD_DIM), lambda s: (0, s, 0)),
            pl.BlockSpec((HEADS, DEC_SEQ, HEAD_DIM), lambda s: (1, s, 0)),
            pl.BlockSpec((HEADS, DEC_SEQ, HEAD_DIM), lambda s: (2, s, 0)),
            pl.BlockSpec((1, BAND_ROWS, D_MODEL), lambda s: (s, 0, 0)),
            pl.BlockSpec((1, BAND_ROWS, D_MODEL), lambda s: (s, 0, 0)),
            pl.BlockSpec((HEADS, 1, BAND_F), lambda s: (0, 0, 0)),
        ],
        out_specs=pl.BlockSpec((DEC_SEQ, D_MODEL), lambda s: (s, 0)),
        out_shape=jax.ShapeDtypeStruct((T_SAMPLE, D_MODEL), BF16),
        compiler_params=_params("arbitrary"),
        name="band_sample",
    )(qkv_hm, qkv_hm, qkv_hm, cache_k, cache_v, f_rows)


def _proj_residual_kernel(a_ref, w_ref, g_ref, x_ref, o_ref):
    w = w_ref[...]
    g = g_ref[...]
    for u in range(a_ref.shape[0] // PROJ_SUB):
        rows = slice(u * PROJ_SUB, (u + 1) * PROJ_SUB)
        y = jnp.dot(a_ref[rows, :], w, preferred_element_type=F32)
        o_ref[rows, :] = x_ref[rows, :] + _rms(y, g)


def proj_residual(a, w, g, x, *, tm):
    m, k = a.shape
    n = w.shape[1]
    return pl.pallas_call(
        _proj_residual_kernel,
        grid=(m // tm,),
        in_specs=[
            pl.BlockSpec((tm, k), lambda i: (i, 0)),
            pl.BlockSpec((k, n), lambda i: (0, 0)),
            pl.BlockSpec((1, n), lambda i: (0, 0)),
            pl.BlockSpec((tm, n), lambda i: (i, 0)),
        ],
        out_specs=pl.BlockSpec((tm, n), lambda i: (i, 0)),
        out_shape=jax.ShapeDtypeStruct((m, n), F32),
        compiler_params=_params("arbitrary"),
        name="proj_residual",
    )(a, w, g.reshape(1, n), x)


def _ffn_kernel(x_ref, g1_ref, w1_ref, w2_ref, g2_ref, o_ref, h_ref):
    j = pl.program_id(1)

    @pl.when(j == 0)
    def _():
        h_ref[...] = _rms(x_ref[...], g1_ref[...]).astype(BF16)
        o_ref[...] = jnp.zeros(o_ref.shape, F32)

    a = jnp.maximum(jnp.dot(h_ref[...], w1_ref[...], preferred_element_type=F32), 0.0)
    a = (a * a).astype(BF16)
    for c in range(o_ref.shape[1] // FFN_CHUNK):
        cols = slice(c * FFN_CHUNK, (c + 1) * FFN_CHUNK)
        o_ref[:, cols] += jnp.dot(a, w2_ref[:, cols], preferred_element_type=F32)

    @pl.when(j == pl.num_programs(1) - 1)
    def _():
        o_ref[...] = x_ref[...] + _rms(o_ref[...], g2_ref[...])


def ffn(x, g1, w1, w2, g2, *, tm, tf):
    m, d = x.shape
    f = w1.shape[1]
    return pl.pallas_call(
        _ffn_kernel,
        grid=(m // tm, f // tf),
        in_specs=[
            pl.BlockSpec((tm, d), lambda i, j: (i, 0)),
            pl.BlockSpec((1, d), lambda i, j: (0, 0)),
            pl.BlockSpec((d, tf), lambda i, j: (0, j)),
            pl.BlockSpec((tf, d), lambda i, j: (j, 0)),
            pl.BlockSpec((1, d), lambda i, j: (0, 0)),
        ],
        out_specs=pl.BlockSpec((tm, d), lambda i, j: (i, 0)),
        out_shape=jax.ShapeDtypeStruct((m, d), F32),
        scratch_shapes=[pltpu.VMEM((tm, d), BF16)],
        compiler_params=_params("arbitrary", "arbitrary"),
        name="ffn",
    )(x, g1.reshape(1, d), w1, w2, g2.reshape(1, d))


def _rope128(x, cos, sin):
    lane = lax.broadcasted_iota(jnp.int32, x.shape, 1)
    half = MLA_ROPE // 2
    swapped = jnp.where(lane < half, pltpu.roll(x, 128 - half, 1), pltpu.roll(x, half, 1))
    return x * cos + swapped * sin


def _mla_proj_kernel(d_ref, qn_ref, kvn_ref, wq_ref, wuk_ref, wuv_ref, cos_ref, sin_ref,
                     ckv_ref, kr_ref, q_ref, *kv_refs):
    d = d_ref[...]
    cos = cos_ref[...]
    sin = sin_ref[...]
    cq = _rms(d[:, :MLA_RANK], qn_ref[...]).astype(BF16)
    ckv = _rms(d[:, MLA_RANK:2 * MLA_RANK], kvn_ref[...])
    ckv_ref[...] = ckv
    kr = _rope128(d[:, 2 * MLA_RANK:], cos, sin)
    kr_ref[...] = kr[:, :MLA_ROPE]

    q = jnp.dot(cq, wq_ref[...], preferred_element_type=F32) * MLA_Q_SCALE
    for h in range(HEADS):
        base = h * MLA_QK
        q_ref[h, :, 0:MLA_NOPE] = q[:, base:base + MLA_NOPE].astype(BF16)
        q_ref[h, :, MLA_NOPE:] = _rope128(q[:, base + MLA_NOPE:base + MLA_QK], cos, sin).astype(BF16)

    if kv_refs:
        k_ref, v_ref = kv_refs
        kr_b = kr.astype(BF16)
        ckv_b = ckv.astype(BF16)
        kn = jnp.dot(ckv_b, wuk_ref[...], preferred_element_type=F32)
        vv = jnp.dot(ckv_b, wuv_ref[...], preferred_element_type=F32)
        ones = jnp.ones((d.shape[0], HEAD_DIM), BF16)
        for h in range(HEADS):
            k_ref[h, :, 0:MLA_NOPE] = kn[:, h * MLA_NOPE:(h + 1) * MLA_NOPE].astype(BF16)
            k_ref[h, :, MLA_NOPE:] = kr_b
            v_ref[h, :, 0:HEAD_DIM] = vv[:, h * HEAD_DIM:(h + 1) * HEAD_DIM].astype(BF16)
            v_ref[h, :, HEAD_DIM:] = ones


def mla_proj(d, q_norm, kv_norm, wq, wuk, wuv, cos, sin, *, tm, with_kv):
    m = d.shape[0]
    full = lambda i: (0, 0)
    rope_tiles = cos.shape[0] // tm
    hm_spec = pl.BlockSpec((HEADS, tm, MLA_QK), lambda i: (0, i, 0))
    hm_shape = jax.ShapeDtypeStruct((HEADS, m, MLA_QK), BF16)
    n_hm = 3 if with_kv else 1
    return pl.pallas_call(
        _mla_proj_kernel,
        grid=(m // tm,),
        in_specs=[
            pl.BlockSpec((tm, d.shape[1]), lambda i: (i, 0)),
            pl.BlockSpec((1, MLA_RANK), full),
            pl.BlockSpec((1, MLA_RANK), full),
            pl.BlockSpec(wq.shape, full),
            pl.BlockSpec(wuk.shape, full),
            pl.BlockSpec(wuv.shape, full),
            pl.BlockSpec((tm, 128), lambda i: (i % rope_tiles, 0)),
            pl.BlockSpec((tm, 128), lambda i: (i % rope_tiles, 0)),
        ],
        out_specs=[
            pl.BlockSpec((tm, MLA_RANK), lambda i: (i, 0)),
            pl.BlockSpec((tm, MLA_ROPE), lambda i: (i, 0)),
        ] + [hm_spec] * n_hm,
        out_shape=[
            jax.ShapeDtypeStruct((m, MLA_RANK), F32),
            jax.ShapeDtypeStruct((m, MLA_ROPE), F32),
        ] + [hm_shape] * n_hm,
        compiler_params=_params("arbitrary"),
        name="mla_proj",
    )(d, q_norm.reshape(1, -1), kv_norm.reshape(1, -1), wq, wuk, wuv, cos, sin)


def _mla_prompt_kernel(q_ref, k_ref, v_ref, o_ref, s_ref, s2_ref, bias_ref, m_ref, acc_ref):
    @pl.when((pl.program_id(0) == 0) & (pl.program_id(1) == 0))
    def _():
        rc = lax.broadcasted_iota(jnp.int32, (MLA_TQ, MLA_TK), 0) // CHUNK
        cc = lax.broadcasted_iota(jnp.int32, (MLA_TQ, MLA_TK), 1) // CHUNK
        bias_ref[...] = jnp.where(cc <= rc, 0.0, NEG)

    def scores(qi, kj, dst_ref):
        q = q_ref[0, qi * MLA_TQ:(qi + 1) * MLA_TQ, :]
        k = k_ref[0, kj * MLA_TK:(kj + 1) * MLA_TK, :]
        dst_ref[...] = lax.dot_general(q, k, _NT, preferred_element_type=F32)

    def softmax_pv(qi, kj, cur_ref):
        v = v_ref[0, kj * MLA_TK:(kj + 1) * MLA_TK, :]
        for u in range(MLA_TQ // MLA_SUB):
            rows = slice(u * MLA_SUB, (u + 1) * MLA_SUB)
            s = cur_ref[rows, :]
            if kj == qi:
                s = s + bias_ref[rows, :]
            if kj == 0:
                m_new = jnp.broadcast_to(jnp.max(s, axis=-1, keepdims=True), (MLA_SUB, 128))
            else:
                m_prev = m_ref[rows, :]
                m_new = jnp.maximum(m_prev, jnp.max(s, axis=-1, keepdims=True))
            p = jnp.exp2(s - jnp.tile(m_new, (1, MLA_TK // 128)))
            acc = jnp.dot(p.astype(BF16), v, preferred_element_type=F32)
            if kj > 0:
                acc = acc + jnp.tile(jnp.exp2(m_prev - m_new), (1, 2)) * acc_ref[rows, :]
            if kj == qi:
                o_ref[qi * MLA_TQ + u * MLA_SUB:qi * MLA_TQ + (u + 1) * MLA_SUB, :] = (
                    acc[:, :HEAD_DIM] / acc[:, HEAD_DIM:]).astype(BF16)
            else:
                acc_ref[rows, :] = acc
                m_ref[rows, :] = m_new

    pairs = [(qi, kj) for qi in range(MLA_NQ) for kj in range(qi + 1)]
    bufs = (s_ref, s2_ref)
    scores(0, 0, s_ref)
    for t, (qi, kj) in enumerate(pairs):
        if t + 1 < len(pairs):
            scores(*pairs[t + 1], bufs[(t + 1) % 2])
        softmax_pv(qi, kj, bufs[t % 2])


def mla_prompt(q, k, v):
    return pl.pallas_call(
        _mla_prompt_kernel,
        grid=(BATCH, HEADS),
        in_specs=[
            pl.BlockSpec((1, SEQ, MLA_QK), lambda b, h: (h, b, 0)),
            pl.BlockSpec((1, SEQ, MLA_QK), lambda b, h: (h, b, 0)),
            pl.BlockSpec((1, SEQ, 2 * HEAD_DIM), lambda b, h: (h, b, 0)),
        ],
        out_specs=pl.BlockSpec((SEQ, HEAD_DIM), lambda b, h: (b, h)),
        out_shape=jax.ShapeDtypeStruct((T_PROMPT, D_MODEL), BF16),
        scratch_shapes=[
            pltpu.VMEM((MLA_TQ, MLA_TK), F32),
            pltpu.VMEM((MLA_TQ, MLA_TK), F32),
            pltpu.VMEM((MLA_TQ, MLA_TK), F32),
            pltpu.VMEM((MLA_TQ, 128), F32),
            pltpu.VMEM((MLA_TQ, 2 * HEAD_DIM), F32),
        ],
        compiler_params=_params("arbitrary", "arbitrary"),
        name="mla_prompt",
    )(q, k, v)


def _mla_sample_kernel(q_ref, cn_ref, rn_ref, cc_ref, cr_ref, wuk_ref, wuv_ref, o_ref,
                       ql_ref, qr_ref):
    for h in range(HEADS):
        rows = slice(h * DEC_SEQ, (h + 1) * DEC_SEQ)
        ql_ref[rows, :] = lax.dot_general(
            q_ref[h, :, 0:MLA_NOPE], wuk_ref[:, h * MLA_NOPE:(h + 1) * MLA_NOPE], _NT,
            preferred_element_type=F32).astype(BF16)
        qr_ref[rows, :] = q_ref[h, :, MLA_NOPE:]
    ql = ql_ref[...]
    qr = qr_ref[...][:, :MLA_ROPE]
    cc = cc_ref[0].astype(BF16)
    cr = cr_ref[0].astype(BF16)
    cn = cn_ref[...].astype(BF16)
    rn = rn_ref[...].astype(BF16)
    s_c = (lax.dot_general(ql, cc, _NT, preferred_element_type=F32)
           + lax.dot_general(qr, cr, _NT, preferred_element_type=F32))
    s_n = (lax.dot_general(ql, cn, _NT, preferred_element_type=F32)
           + lax.dot_general(qr, rn, _NT, preferred_element_type=F32))
    m = jnp.maximum(jnp.max(s_c, axis=-1, keepdims=True), jnp.max(s_n, axis=-1, keepdims=True))
    p_c = jnp.exp2(s_c - m)
    p_n = jnp.exp2(s_n - m)
    l = jnp.sum(p_c, axis=-1, keepdims=True) + jnp.sum(p_n, axis=-1, keepdims=True)
    o_lat = ((jnp.dot(p_c.astype(BF16), cc, preferred_element_type=F32)
              + jnp.dot(p_n.astype(BF16), cn, preferred_element_type=F32)) / l).astype(BF16)
    for h in range(HEADS):
        cols = slice(h * HEAD_DIM, (h + 1) * HEAD_DIM)
        o_ref[:, cols] = jnp.dot(o_lat[h * DEC_SEQ:(h + 1) * DEC_SEQ], wuv_ref[:, cols],
                                 preferred_element_type=F32).astype(BF16)


def mla_sample(q, ckv, kr, cache_ckv, cache_kr, wuk, wuv):
    full = lambda s: (0, 0)
    return pl.pallas_call(
        _mla_sample_kernel,
        grid=(DEC_BATCH,),
        in_specs=[
            pl.BlockSpec((HEADS, DEC_SEQ, MLA_QK), lambda s: (0, s, 0)),
            pl.BlockSpec((DEC_SEQ, MLA_RANK), lambda s: (s, 0)),
            pl.BlockSpec((DEC_SEQ, MLA_ROPE), lambda s: (s, 0)),
            pl.BlockSpec((1, PAST_LEN, MLA_RANK), lambda s: (s, 0, 0)),
            pl.BlockSpec((1, PAST_LEN, MLA_ROPE), lambda s: (s, 0, 0)),
            pl.BlockSpec(wuk.shape, full),
            pl.BlockSpec(wuv.shape, full),
        ],
        out_specs=pl.BlockSpec((DEC_SEQ, D_MODEL), lambda s: (s, 0)),
        out_shape=jax.ShapeDtypeStruct((T_SAMPLE, D_MODEL), BF16),
        scratch_shapes=[
            pltpu.VMEM((HEADS * DEC_SEQ, MLA_RANK), BF16),
            pltpu.VMEM((HEADS * DEC_SEQ, 128), BF16),
        ],
        compiler_params=_params("arbitrary"),
        name="mla_sample",
    )(q, ckv, kr, cache_ckv, cache_kr, wuk, wuv)


def _bias_rows(rel_bias):
    far = jnp.broadcast_to(rel_bias[:, 2 * REL_CLIP:], (HEADS, BAND_F // 4))
    mid = rel_bias[:, 1:][:, ::-1]
    return (jnp.concatenate([far, mid, far], axis=1) * LOG2E).reshape(HEADS, 1, BAND_F)


def _rope_tables(pos):
    half = MLA_ROPE // 2
    inv = 1.0 / (ROPE_THETA ** (jnp.arange(half, dtype=F32) / half))
    ang = pos.astype(F32)[:, None] * inv[None, :]
    c, s = jnp.cos(ang), jnp.sin(ang)
    z = jnp.zeros((pos.shape[0], 128 - MLA_ROPE), F32)
    return jnp.concatenate([c, c, z], axis=1), jnp.concatenate([-s, s, z], axis=1)


PROMPT_TM_QKV = 1024
MLA_PROJ_TM = 256


def kernel(x_prompt, x_sample, cache_a_k, cache_a_v, cache_mla_ckv, cache_mla_kr, ln_mix_pre, ln_mix_post, ln_ffn_pre, ln_ffn_post, a_w_qkv, a_w_o, a_rel_bias, mla_w_dq, mla_q_norm, mla_w_uq, mla_w_dkv, mla_kv_norm, mla_w_uk, mla_w_uv, mla_w_o, ffn_w1, ffn_w2):
    xp = x_prompt.reshape(T_PROMPT, D_MODEL)
    xs = x_sample.reshape(T_SAMPLE, D_MODEL)
    w1 = [ffn_w1[i].astype(BF16) for i in range(2)]
    w2 = [ffn_w2[i].astype(BF16) for i in range(2)]

    def mix_out_and_ffn(i, o, w_o, x):
        x = proj_residual(o, w_o, ln_mix_post[i], x, tm=512)
        return ffn(x, ln_ffn_pre[i], w1[i], w2[i], ln_ffn_post[i], tm=512, tf=1024)

    w_qkv = a_w_qkv[0].astype(BF16)
    w_o = a_w_o[0].astype(BF16)
    f_rows = _bias_rows(a_rel_bias[0])
    hm_p, k_p, v_p = norm_qkv(xp, ln_mix_pre[0], w_qkv, tm=PROMPT_TM_QKV, keep=BAND_ROWS,
                              tiles_per_keep=SEQ // PROMPT_TM_QKV)
    hm_s, k_s, v_s = norm_qkv(xs, ln_mix_pre[0], w_qkv, tm=512, keep=512, tiles_per_keep=1)
    o_p = band_prompt(hm_p, f_rows)
    o_s = band_sample(hm_s,
                      cache_a_k[0].reshape(DEC_BATCH, BAND_ROWS, D_MODEL),
                      cache_a_v[0].reshape(DEC_BATCH, BAND_ROWS, D_MODEL), f_rows)
    xp = mix_out_and_ffn(0, o_p, w_o, xp)
    xs = mix_out_and_ffn(0, o_s, w_o, xs)

    pad = jnp.zeros((D_MODEL, 128 - MLA_ROPE), F32)
    w_d = jnp.concatenate([mla_w_dq[0], mla_w_dkv[0], pad], axis=1).astype(BF16)
    wq = jnp.pad(mla_w_uq[0], ((0, 0), (0, 0), (0, MLA_QK - MLA_NOPE - MLA_ROPE)))
    wq = wq.reshape(MLA_RANK, HEADS * MLA_QK).astype(BF16)
    wuk = mla_w_uk[0].reshape(MLA_RANK, HEADS * MLA_NOPE).astype(BF16)
    wuv = mla_w_uv[0].reshape(MLA_RANK, HEADS * HEAD_DIM).astype(BF16)
    w_o = mla_w_o[0].astype(BF16)
    cos_p, sin_p = _rope_tables(jnp.arange(SEQ))
    cos_s, sin_s = _rope_tables(jnp.tile(PAST_LEN + jnp.arange(DEC_SEQ), MLA_PROJ_TM // DEC_SEQ))

    d_p = norm_matmul(xp, ln_mix_pre[1], w_d, tm=512)
    d_s = norm_matmul(xs, ln_mix_pre[1], w_d, tm=512)
    ckv_p, kr_p, q_p, kk_p, vv_p = mla_proj(d_p, mla_q_norm[0], mla_kv_norm[0], wq, wuk, wuv,
                                            cos_p, sin_p, tm=MLA_PROJ_TM, with_kv=True)
    ckv_s, kr_s, q_s = mla_proj(d_s, mla_q_norm[0], mla_kv_norm[0], wq, wuk, wuv,
                                cos_s, sin_s, tm=MLA_PROJ_TM, with_kv=False)
    o_p = mla_prompt(q_p, kk_p, vv_p)
    o_s = mla_sample(q_s, ckv_s, kr_s, cache_mla_ckv[0], cache_mla_kr[0], wuk, wuv)
    xp = mix_out_and_ffn(1, o_p, w_o, xp)
    xs = mix_out_and_ffn(1, o_s, w_o, xs)

    return (
        xp.reshape(BATCH, SEQ, D_MODEL),
        xs.reshape(DEC_BATCH, DEC_SEQ, D_MODEL),
        k_p.reshape(1, BATCH, BAND_ROWS, HEADS, HEAD_DIM),
        v_p.reshape(1, BATCH, BAND_ROWS, HEADS, HEAD_DIM),
        k_s.reshape(1, DEC_BATCH, DEC_SEQ, HEADS, HEAD_DIM),
        v_s.reshape(1, DEC_BATCH, DEC_SEQ, HEADS, HEAD_DIM),
        ckv_p.reshape(1, BATCH, SEQ, MLA_RANK),
        kr_p.reshape(1, BATCH, SEQ, MLA_ROPE),
        ckv_s.reshape(1, DEC_BATCH, DEC_SEQ, MLA_RANK),
        kr_s.reshape(1, DEC_BATCH, DEC_SEQ, MLA_ROPE),
    )
```

```python
import functools
import math

import jax
import jax.numpy as jnp
from jax import lax
from jax.experimental import pallas as pl
from jax.experimental.pallas import tpu as pltpu

F32 = jnp.float32
BF16 = jnp.bfloat16

D_MODEL = 2048
BATCH = 4
SEQ = 4096
DEC_BATCH = 16
DEC_SEQ = 64
PAST_LEN = 2048
CHUNK = 64
HEADS = 16
HEAD_DIM = 128
BAND_ROWS = 512
REL_CLIP = 256
MLA_RANK = 512
MLA_NOPE = 128
MLA_ROPE = 64
MLA_QK = 256
ROPE_THETA = 10000.0
D_FF = 8192
EPS = 1e-6
NEG = -1e30
LOG2E = math.log2(math.e)

T_PROMPT = BATCH * SEQ
T_SAMPLE = DEC_BATCH * DEC_SEQ

VMEM_LIMIT_BYTES = 58 * 1024 * 1024

BAND_TQ = 256
BAND_TK = BAND_ROWS + BAND_TQ
BAND_F = 1024
BAND_NQ = SEQ // BAND_TQ
BAND_SUB = 32
BAND_Q_SCALE = HEAD_DIM ** -0.5 * LOG2E

MLA_TQ = 512
MLA_TK = 512
MLA_TK_MAX = 1024
MLA_NQ = SEQ // MLA_TQ
MLA_SUB = 32
MLA_PV_SUB = 256
MLA_Q_SCALE = (MLA_NOPE + MLA_ROPE) ** -0.5 * LOG2E

QKV_TN = 1024
QKV_CHUNK = 256
PROJ_SUB = 128
FFN_CHUNK = 512

_NT = (((1,), (1,)), ((), ()))


def _params(*sem):
    return pltpu.CompilerParams(dimension_semantics=sem, vmem_limit_bytes=VMEM_LIMIT_BYTES)


def _rms(x, g):
    ms = jnp.mean(x * x, axis=-1, keepdims=True)
    return x * lax.rsqrt(ms + EPS) * g


def _norm_matmul_kernel(x_ref, g_ref, w_ref, o_ref):
    h = _rms(x_ref[...], g_ref[...]).astype(BF16)
    o_ref[...] = jnp.dot(h, w_ref[...], preferred_element_type=F32)


def norm_matmul(x, g, w, *, tm):
    m, k = x.shape
    n = w.shape[1]
    return pl.pallas_call(
        _norm_matmul_kernel,
        grid=(m // tm,),
        in_specs=[
            pl.BlockSpec((tm, k), lambda i: (i, 0)),
            pl.BlockSpec((1, k), lambda i: (0, 0)),
            pl.BlockSpec((k, n), lambda i: (0, 0)),
        ],
        out_specs=pl.BlockSpec((tm, n), lambda i: (i, 0)),
        out_shape=jax.ShapeDtypeStruct((m, n), F32),
        compiler_params=_params("arbitrary"),
        name="norm_matmul",
    )(x, g.reshape(1, k), w)


def _norm_qkv_kernel(x_ref, g_ref, w_ref, hm_ref, kf_ref, vf_ref, h_ref, *, keep):
    j = pl.program_id(1)

    @pl.when(j == 0)
    def _():
        h_ref[...] = _rms(x_ref[...], g_ref[...]).astype(BF16)

    h = h_ref[...]
    tm = h.shape[0]

    def project(scale, f_ref, f_col):
        for c in range(QKV_TN // QKV_CHUNK):
            cols = slice(c * QKV_CHUNK, (c + 1) * QKV_CHUNK)
            acc = jnp.dot(h, w_ref[:, cols], preferred_element_type=F32)
            if f_ref is not None:
                f_ref[:, f_col + c * QKV_CHUNK:f_col + (c + 1) * QKV_CHUNK] = acc[tm - keep:, :]
            if scale is not None:
                acc = acc * scale
            for hh in range(QKV_CHUNK // HEAD_DIM):
                hm_ref[c * (QKV_CHUNK // HEAD_DIM) + hh] = (
                    acc[:, hh * HEAD_DIM:(hh + 1) * HEAD_DIM].astype(BF16))

    for jj, f_ref in enumerate((kf_ref, kf_ref, vf_ref, vf_ref)):
        pl.when(j == jj)(functools.partial(project, None, f_ref, (jj % 2) * QKV_TN))

    pl.when(j >= 4)(functools.partial(project, BAND_Q_SCALE, None, 0))


def norm_qkv(x, g, w, *, tm, keep, tiles_per_keep):
    m, k = x.shape
    nb = 3 * D_MODEL // QKV_TN
    heads_per_block = QKV_TN // HEAD_DIM
    n_keep = m // (tm * tiles_per_keep) * keep

    def col(j):
        return (j + 2) % nb

    return pl.pallas_call(
        functools.partial(_norm_qkv_kernel, keep=keep),
        grid=(m // tm, nb),
        in_specs=[
            pl.BlockSpec((tm, k), lambda i, j: (i, 0)),
            pl.BlockSpec((1, k), lambda i, j: (0, 0)),
            pl.BlockSpec((k, QKV_TN), lambda i, j: (0, col(j))),
        ],
        out_specs=[
            pl.BlockSpec((heads_per_block, tm, HEAD_DIM), lambda i, j: (col(j), i, 0)),
            pl.BlockSpec((keep, D_MODEL), lambda i, j: (i // tiles_per_keep, 0)),
            pl.BlockSpec((keep, D_MODEL), lambda i, j: (i // tiles_per_keep, 0)),
        ],
        out_shape=[
            jax.ShapeDtypeStruct((3 * HEADS, m, HEAD_DIM), BF16),
            jax.ShapeDtypeStruct((n_keep, D_MODEL), F32),
            jax.ShapeDtypeStruct((n_keep, D_MODEL), F32),
        ],
        scratch_shapes=[pltpu.VMEM((tm, k), BF16)],
        compiler_params=_params("arbitrary", "arbitrary"),
        name="norm_qkv",
    )(x, g.reshape(1, k), w)


def _toeplitz_bias(f_row, rows):
    x = jnp.broadcast_to(f_row, (rows, BAND_F))
    return pltpu.roll(x, 0, 1, stride=1, stride_axis=0)


def _band_prompt_kernel(q_ref, k_ref, v_ref, f_ref, o_ref, kpad_ref, vpad_ref, bias_ref, s_ref, s2_ref,
                        p_ref, p2_ref):
    @pl.when((pl.program_id(0) == 0) & (pl.program_id(1) == 0))
    def _():
        kpad_ref[0:BAND_ROWS, :] = jnp.zeros((BAND_ROWS, HEAD_DIM), BF16)
        vpad_ref[0:BAND_ROWS, 0:HEAD_DIM] = jnp.zeros((BAND_ROWS, HEAD_DIM), BF16)
        vpad_ref[:, HEAD_DIM:] = jnp.ones((BAND_ROWS + SEQ, HEAD_DIM), BF16)

    @pl.when(pl.program_id(1) == 0)
    def _():
        r = lax.broadcasted_iota(jnp.int32, (BAND_TQ, BAND_TK), 0) // CHUNK
        j = lax.broadcasted_iota(jnp.int32, (BAND_TQ, BAND_TK), 1)
        jc = j // CHUNK
        allowed = (jc >= r) & (jc <= r + BAND_ROWS // CHUNK)
        base = jnp.where(allowed, _toeplitz_bias(f_ref[0], BAND_TQ)[:, :BAND_TK], NEG)
        bias_ref[2] = base
        bias_ref[1] = jnp.where(j >= BAND_ROWS - BAND_TQ, base, NEG)
        bias_ref[0] = jnp.where(j >= BAND_ROWS, base, NEG)

    kpad_ref[BAND_ROWS:, :] = k_ref[0]
    vpad_ref[BAND_ROWS:, 0:HEAD_DIM] = v_ref[0]

    def scores(qi, dst_ref):
        start = qi * BAND_TQ
        dst_ref[...] = lax.dot_general(q_ref[0, start:start + BAND_TQ, :],
                                       kpad_ref[start:start + BAND_TK, :], _NT,
                                       preferred_element_type=F32)

    def softmax(qi, cur_ref, p_dst):
        for u in range(BAND_TQ // BAND_SUB):
            rows = slice(u * BAND_SUB, (u + 1) * BAND_SUB)
            s = cur_ref[rows, :] + bias_ref[min(qi, 2), rows, :]
            p_dst[rows, :] = jnp.exp2(s - jnp.max(s, axis=-1, keepdims=True)).astype(BF16)

    def values(qi, p_src):
        start = qi * BAND_TQ
        pv = jnp.dot(p_src[...], vpad_ref[start:start + BAND_TK, :], preferred_element_type=F32)
        o_ref[start:start + BAND_TQ, :] = (pv[:, :HEAD_DIM] / pv[:, HEAD_DIM:]).astype(BF16)

    s_bufs = (s_ref, s2_ref)
    p_bufs = (p_ref, p2_ref)
    scores(0, s_bufs[0])
    scores(1, s_bufs[1])
    softmax(0, s_bufs[0], p_bufs[0])
    for t in range(BAND_NQ):
        if t + 2 < BAND_NQ:
            scores(t + 2, s_bufs[t % 2])
        if t + 1 < BAND_NQ:
            softmax(t + 1, s_bufs[(t + 1) % 2], p_bufs[(t + 1) % 2])
        values(t, p_bufs[t % 2])


def band_prompt(qkv_hm, f_rows):
    return pl.pallas_call(
        _band_prompt_kernel,
        grid=(HEADS, BATCH),
        in_specs=[
            pl.BlockSpec((1, SEQ, HEAD_DIM), lambda h, b: (h, b, 0)),
            pl.BlockSpec((1, SEQ, HEAD_DIM), lambda h, b: (HEADS + h, b, 0)),
            pl.BlockSpec((1, SEQ, HEAD_DIM), lambda h, b: (2 * HEADS + h, b, 0)),
            pl.BlockSpec((1, 1, BAND_F), lambda h, b: (h, 0, 0)),
        ],
        out_specs=pl.BlockSpec((SEQ, HEAD_DIM), lambda h, b: (b, h)),
        out_shape=jax.ShapeDtypeStruct((T_PROMPT, D_MODEL), BF16),
        scratch_shapes=[
            pltpu.VMEM((BAND_ROWS + SEQ, HEAD_DIM), BF16),
            pltpu.VMEM((BAND_ROWS + SEQ, 2 * HEAD_DIM), BF16),
            pltpu.VMEM((3, BAND_TQ, BAND_TK), F32),
            pltpu.VMEM((BAND_TQ, BAND_TK), F32),
            pltpu.VMEM((BAND_TQ, BAND_TK), F32),
            pltpu.VMEM((BAND_TQ, BAND_TK), BF16),
            pltpu.VMEM((BAND_TQ, BAND_TK), BF16),
        ],
        compiler_params=_params("arbitrary", "arbitrary"),
        name="band_prompt",
    )(qkv_hm, qkv_hm, qkv_hm, f_rows)


def _band_sample_kernel(q_ref, kn_ref, vn_ref, ck_ref, cv_ref, f_ref, o_ref):
    for h in range(HEADS):
        cols = slice(h * HEAD_DIM, (h + 1) * HEAD_DIM)
        bias = _toeplitz_bias(f_ref[h], DEC_SEQ)
        q = q_ref[h]
        kc = ck_ref[0, :, h, :].astype(BF16)
        vc = cv_ref[0, :, h, :].astype(BF16)
        s_c = lax.dot_general(q, kc, _NT, preferred_element_type=F32) + bias[:, :BAND_ROWS]
        s_n = (lax.dot_general(q, kn_ref[h], _NT, preferred_element_type=F32)
               + bias[:, BAND_ROWS:BAND_ROWS + DEC_SEQ])
        m = jnp.maximum(jnp.max(s_c, axis=-1, keepdims=True), jnp.max(s_n, axis=-1, keepdims=True))
        p_c = jnp.exp2(s_c - m)
        p_n = jnp.exp2(s_n - m)
        l = jnp.sum(p_c, axis=-1, keepdims=True) + jnp.sum(p_n, axis=-1, keepdims=True)
        o = (jnp.dot(p_c.astype(BF16), vc, preferred_element_type=F32)
             + jnp.dot(p_n.astype(BF16), vn_ref[h], preferred_element_type=F32)) / l
        o_ref[:, cols] = o.astype(BF16)


def band_sample(qkv_hm, cache_k, cache_v, f_rows):
    return pl.pallas_call(
        _band_sample_kernel,
        grid=(DEC_BATCH,),
        in_specs=[
            pl.BlockSpec((HEADS, DEC_SEQ, HEAD_DIM), lambda s: (0, s, 0)),
            pl.BlockSpec((HEADS, DEC_SEQ, HEAD_DIM), lambda s: (1, s, 0)),
            pl.BlockSpec((HEADS, DEC_SEQ, HEAD_DIM), lambda s: (2, s, 0)),
            pl.BlockSpec((1, BAND_ROWS, HEADS, HEAD_DIM), lambda s: (s, 0, 0, 0)),
            pl.BlockSpec((1, BAND_ROWS, HEADS, HEAD_DIM), lambda s: (s, 0, 0, 0)),
            pl.BlockSpec((HEADS, 1, BAND_F), lambda s: (0, 0, 0)),
        ],
        out_specs=pl.BlockSpec((DEC_SEQ, D_MODEL), lambda s: (s, 0)),
        out_shape=jax.ShapeDtypeStruct((T_SAMPLE, D_MODEL), BF16),
        compiler_params=_params("arbitrary"),
        name="band_sample",
    )(qkv_hm, qkv_hm, qkv_hm, cache_k, cache_v, f_rows)


def _proj_residual_kernel(a_ref, w_ref, g_ref, x_ref, o_ref):
    w = w_ref[...]
    g = g_ref[...]
    for u in range(a_ref.shape[0] // PROJ_SUB):
        rows = slice(u * PROJ_SUB, (u + 1) * PROJ_SUB)
        y = jnp.dot(a_ref[rows, :], w, preferred_element_type=F32)
        o_ref[rows, :] = x_ref[rows, :] + _rms(y, g)


def proj_residual(a, w, g, x, *, tm):
    m, k = a.shape
    n = w.shape[1]
    return pl.pallas_call(
        _proj_residual_kernel,
        grid=(m // tm,),
        in_specs=[
            pl.BlockSpec((tm, k), lambda i: (i, 0)),
            pl.BlockSpec((k, n), lambda i: (0, 0)),
            pl.BlockSpec((1, n), lambda i: (0, 0)),
            pl.BlockSpec((tm, n), lambda i: (i, 0)),
        ],
        out_specs=pl.BlockSpec((tm, n), lambda i: (i, 0)),
        out_shape=jax.ShapeDtypeStruct((m, n), F32),
        compiler_params=_params("arbitrary"),
        name="proj_residual",
    )(a, w, g.reshape(1, n), x)


def _ffn_kernel(x_ref, g1_ref, w1_ref, w2_ref, g2_ref, o_ref, h_ref):
    j = pl.program_id(1)

    @pl.when(j == 0)
    def _():
        h_ref[...] = _rms(x_ref[...], g1_ref[...]).astype(BF16)
        o_ref[...] = jnp.zeros(o_ref.shape, F32)

    a = jnp.maximum(jnp.dot(h_ref[...], w1_ref[...], preferred_element_type=F32), 0.0)
    a = (a * a).astype(BF16)
    for c in range(o_ref.shape[1] // FFN_CHUNK):
        cols = slice(c * FFN_CHUNK, (c + 1) * FFN_CHUNK)
        o_ref[:, cols] += jnp.dot(a, w2_ref[:, cols], preferred_element_type=F32)

    @pl.when(j == pl.num_programs(1) - 1)
    def _():
        o_ref[...] = x_ref[...] + _rms(o_ref[...], g2_ref[...])


def ffn(x, g1, w1, w2, g2, *, layer, tm, tf):
    m, d = x.shape
    f = w1.shape[2]
    return pl.pallas_call(
        _ffn_kernel,
        grid=(m // tm, f // tf),
        in_specs=[
            pl.BlockSpec((tm, d), lambda i, j: (i, 0)),
            pl.BlockSpec((1, d), lambda i, j: (0, 0)),
            pl.BlockSpec((None, d, tf), lambda i, j: (layer, 0, j)),
            pl.BlockSpec((None, tf, d), lambda i, j: (layer, j, 0)),
            pl.BlockSpec((1, d), lambda i, j: (0, 0)),
        ],
        out_specs=pl.BlockSpec((tm, d), lambda i, j: (i, 0)),
        out_shape=jax.ShapeDtypeStruct((m, d), F32),
        scratch_shapes=[pltpu.VMEM((tm, d), BF16)],
        compiler_params=_params("arbitrary", "arbitrary"),
        name="ffn",
    )(x, g1.reshape(1, d), w1, w2, g2.reshape(1, d))


def _rope128(x, cos, sin):
    lane = lax.broadcasted_iota(jnp.int32, x.shape, 1)
    half = MLA_ROPE // 2
    swapped = jnp.where(lane < half, pltpu.roll(x, 128 - half, 1), pltpu.roll(x, half, 1))
    return x * cos + swapped * sin


def _mla_proj_kernel(d_ref, qn_ref, kvn_ref, wq_ref, wuk_ref, wuv_ref, cos_ref, sin_ref,
                     ckv_ref, kr_ref, q_ref, *kv_refs):
    d = d_ref[...]
    cos = cos_ref[...]
    sin = sin_ref[...]
    cq = _rms(d[:, :MLA_RANK], qn_ref[...]).astype(BF16)
    ckv = _rms(d[:, MLA_RANK:2 * MLA_RANK], kvn_ref[...])
    ckv_ref[...] = ckv
    kr = _rope128(d[:, 2 * MLA_RANK:], cos, sin)
    kr_ref[...] = kr[:, :MLA_ROPE]

    q = jnp.dot(cq, wq_ref[...], preferred_element_type=F32) * MLA_Q_SCALE
    for h in range(HEADS):
        base = h * MLA_QK
        q_ref[h, :, 0:MLA_NOPE] = q[:, base:base + MLA_NOPE].astype(BF16)
        q_ref[h, :, MLA_NOPE:] = _rope128(q[:, base + MLA_NOPE:base + MLA_QK], cos, sin).astype(BF16)

    if kv_refs:
        k_ref, v_ref = kv_refs
        kr_b = kr.astype(BF16)
        ckv_b = ckv.astype(BF16)
        kn = jnp.dot(ckv_b, wuk_ref[...], preferred_element_type=F32)
        vv = jnp.dot(ckv_b, wuv_ref[...], preferred_element_type=F32)
        ones = jnp.ones((d.shape[0], HEAD_DIM), BF16)
        for h in range(HEADS):
            k_ref[h, :, 0:MLA_NOPE] = kn[:, h * MLA_NOPE:(h + 1) * MLA_NOPE].astype(BF16)
            k_ref[h, :, MLA_NOPE:] = kr_b
            v_ref[h, :, 0:HEAD_DIM] = vv[:, h * HEAD_DIM:(h + 1) * HEAD_DIM].astype(BF16)
            v_ref[h, :, HEAD_DIM:] = ones


def mla_proj(d, q_norm, kv_norm, wq, wuk, wuv, cos, sin, *, tm, with_kv):
    m = d.shape[0]
    full = lambda i: (0, 0)
    rope_tiles = cos.shape[0] // tm
    hm_spec = pl.BlockSpec((HEADS, tm, MLA_QK), lambda i: (0, i, 0))
    hm_shape = jax.ShapeDtypeStruct((HEADS, m, MLA_QK), BF16)
    n_hm = 3 if with_kv else 1
    return pl.pallas_call(
        _mla_proj_kernel,
        grid=(m // tm,),
        in_specs=[
            pl.BlockSpec((tm, d.shape[1]), lambda i: (i, 0)),
            pl.BlockSpec((1, MLA_RANK), full),
            pl.BlockSpec((1, MLA_RANK), full),
            pl.BlockSpec(wq.shape, full),
            pl.BlockSpec(wuk.shape, full),
            pl.BlockSpec(wuv.shape, full),
            pl.BlockSpec((tm, 128), lambda i: (i % rope_tiles, 0)),
            pl.BlockSpec((tm, 128), lambda i: (i % rope_tiles, 0)),
        ],
        out_specs=[
            pl.BlockSpec((tm, MLA_RANK), lambda i: (i, 0)),
            pl.BlockSpec((tm, MLA_ROPE), lambda i: (i, 0)),
        ] + [hm_spec] * n_hm,
        out_shape=[
            jax.ShapeDtypeStruct((m, MLA_RANK), F32),
            jax.ShapeDtypeStruct((m, MLA_ROPE), F32),
        ] + [hm_shape] * n_hm,
        compiler_params=_params("arbitrary"),
        name="mla_proj",
    )(d, q_norm.reshape(1, -1), kv_norm.reshape(1, -1), wq, wuk, wuv, cos, sin)


def _mla_prompt_kernel(q_ref, k_ref, v_ref, o_ref, s_ref, s2_ref, p_ref, p2_ref, alpha_ref, alpha2_ref,
                       bias_ref, m_ref, acc_ref):
    @pl.when((pl.program_id(0) == 0) & (pl.program_id(1) == 0))
    def _():
        rc = lax.broadcasted_iota(jnp.int32, (MLA_TQ, MLA_TK), 0) // CHUNK
        cc = lax.broadcasted_iota(jnp.int32, (MLA_TQ, MLA_TK), 1) // CHUNK
        bias_ref[...] = jnp.where(cc <= rc, 0.0, NEG)

    def scores(qi, k0, klen, dst_ref):
        q = q_ref[0, qi * MLA_TQ:(qi + 1) * MLA_TQ, :]
        k = k_ref[0, k0:k0 + klen, :]
        dst_ref[:, :klen] = lax.dot_general(q, k, _NT, preferred_element_type=F32)

    def softmax(qi, k0, klen, cur_ref, p_dst, alpha_dst):
        first = k0 == 0
        last = k0 + klen == (qi + 1) * MLA_TQ
        for u in range(MLA_TQ // MLA_SUB):
            rows = slice(u * MLA_SUB, (u + 1) * MLA_SUB)
            if last:
                parts = [cur_ref[rows, klen - MLA_TK:klen] + bias_ref[rows, :]]
                if klen > MLA_TK:
                    parts.insert(0, cur_ref[rows, :klen - MLA_TK])
            else:
                parts = [cur_ref[rows, :klen]]
            m_cur = functools.reduce(jnp.maximum, [jnp.max(x, axis=-1, keepdims=True) for x in parts])
            if first:
                m_new = jnp.broadcast_to(m_cur, (MLA_SUB, 128))
            else:
                m_prev = m_ref[rows, :]
                m_new = jnp.maximum(m_prev, m_cur)
                alpha_dst[rows, :] = jnp.exp2(m_prev - m_new)
            if not last:
                m_ref[rows, :] = m_new
            col = 0
            for x in parts:
                p_dst[rows, col:col + x.shape[1]] = jnp.exp2(
                    x - jnp.tile(m_new, (1, x.shape[1] // 128))).astype(BF16)
                col += x.shape[1]

    def values(qi, k0, klen, p_src, alpha_src):
        first = k0 == 0
        last = k0 + klen == (qi + 1) * MLA_TQ
        v = v_ref[0, k0:k0 + klen, :]
        for u in range(MLA_TQ // MLA_PV_SUB):
            rows = slice(u * MLA_PV_SUB, (u + 1) * MLA_PV_SUB)
            acc = jnp.dot(p_src[rows, :klen], v, preferred_element_type=F32)
            if not first:
                acc = acc + jnp.tile(alpha_src[rows, :], (1, 2)) * acc_ref[rows, :]
            if last:
                o_ref[qi * MLA_TQ + u * MLA_PV_SUB:qi * MLA_TQ + (u + 1) * MLA_PV_SUB, :] = (
                    acc[:, :HEAD_DIM] / acc[:, HEAD_DIM:]).astype(BF16)
            else:
                acc_ref[rows, :] = acc

    blocks = [(qi, k0, min(MLA_TK_MAX, (qi + 1) * MLA_TQ - k0))
              for qi in range(MLA_NQ) for k0 in range(0, (qi + 1) * MLA_TQ, MLA_TK_MAX)]
    n = len(blocks)
    s_bufs = (s_ref, s2_ref)
    p_bufs = (p_ref, p2_ref)
    a_bufs = (alpha_ref, alpha2_ref)
    scores(*blocks[0], s_bufs[0])
    scores(*blocks[1], s_bufs[1])
    softmax(*blocks[0], s_bufs[0], p_bufs[0], a_bufs[0])
    for t in range(n):
        if t + 2 < n:
            scores(*blocks[t + 2], s_bufs[t % 2])
        if t + 1 < n:
            softmax(*blocks[t + 1], s_bufs[(t + 1) % 2], p_bufs[(t + 1) % 2], a_bufs[(t + 1) % 2])
        values(*blocks[t], p_bufs[t % 2], a_bufs[t % 2])


def mla_prompt(q, k, v):
    return pl.pallas_call(
        _mla_prompt_kernel,
        grid=(BATCH, HEADS),
        in_specs=[
            pl.BlockSpec((1, SEQ, MLA_QK), lambda b, h: (h, b, 0)),
            pl.BlockSpec((1, SEQ, MLA_QK), lambda b, h: (h, b, 0)),
            pl.BlockSpec((1, SEQ, 2 * HEAD_DIM), lambda b, h: (h, b, 0)),
        ],
        out_specs=pl.BlockSpec((SEQ, HEAD_DIM), lambda b, h: (b, h)),
        out_shape=jax.ShapeDtypeStruct((T_PROMPT, D_MODEL), BF16),
        scratch_shapes=[
            pltpu.VMEM((MLA_TQ, MLA_TK_MAX), F32),
            pltpu.VMEM((MLA_TQ, MLA_TK_MAX), F32),
            pltpu.VMEM((MLA_TQ, MLA_TK_MAX), BF16),
            pltpu.VMEM((MLA_TQ, MLA_TK_MAX), BF16),
            pltpu.VMEM((MLA_TQ, 128), F32),
            pltpu.VMEM((MLA_TQ, 128), F32),
            pltpu.VMEM((MLA_TQ, MLA_TK), F32),
            pltpu.VMEM((MLA_TQ, 128), F32),
            pltpu.VMEM((MLA_TQ, 2 * HEAD_DIM), F32),
        ],
        compiler_params=_params("arbitrary", "arbitrary"),
        name="mla_prompt",
    )(q, k, v)


def _mla_sample_kernel(q_ref, cn_ref, rn_ref, cc_ref, cr_ref, wuk_ref, wuv_ref, o_ref,
                       ql_ref, qr_ref):
    for h in range(HEADS):
        rows = slice(h * DEC_SEQ, (h + 1) * DEC_SEQ)
        ql_ref[rows, :] = lax.dot_general(
            q_ref[h, :, 0:MLA_NOPE], wuk_ref[:, h * MLA_NOPE:(h + 1) * MLA_NOPE], _NT,
            preferred_element_type=F32).astype(BF16)
        qr_ref[rows, :] = q_ref[h, :, MLA_NOPE:]
    ql = ql_ref[...]
    qr = qr_ref[...][:, :MLA_ROPE]
    cc = cc_ref[0].astype(BF16)
    cr = cr_ref[0].astype(BF16)
    cn = cn_ref[...].astype(BF16)
    rn = rn_ref[...].astype(BF16)
    s_c = (lax.dot_general(ql, cc, _NT, preferred_element_type=F32)
           + lax.dot_general(qr, cr, _NT, preferred_element_type=F32))
    s_n = (lax.dot_general(ql, cn, _NT, preferred_element_type=F32)
           + lax.dot_general(qr, rn, _NT, preferred_element_type=F32))
    m = jnp.maximum(jnp.max(s_c, axis=-1, keepdims=True), jnp.max(s_n, axis=-1, keepdims=True))
    p_c = jnp.exp2(s_c - m)
    p_n = jnp.exp2(s_n - m)
    l = jnp.sum(p_c, axis=-1, keepdims=True) + jnp.sum(p_n, axis=-1, keepdims=True)
    o_lat = ((jnp.dot(p_c.astype(BF16), cc, preferred_element_type=F32)
              + jnp.dot(p_n.astype(BF16), cn, preferred_element_type=F32)) / l).astype(BF16)
    for h in range(HEADS):
        cols = slice(h * HEAD_DIM, (h + 1) * HEAD_DIM)
        o_ref[:, cols] = jnp.dot(o_lat[h * DEC_SEQ:(h + 1) * DEC_SEQ], wuv_ref[:, cols],
                                 preferred_element_type=F32).astype(BF16)


def mla_sample(q, ckv, kr, cache_ckv, cache_kr, wuk, wuv):
    full = lambda s: (0, 0)
    return pl.pallas_call(
        _mla_sample_kernel,
        grid=(DEC_BATCH,),
        in_specs=[
            pl.BlockSpec((HEADS, DEC_SEQ, MLA_QK), lambda s: (0, s, 0)),
            pl.BlockSpec((DEC_SEQ, MLA_RANK), lambda s: (s, 0)),
            pl.BlockSpec((DEC_SEQ, MLA_ROPE), lambda s: (s, 0)),
            pl.BlockSpec((1, PAST_LEN, MLA_RANK), lambda s: (s, 0, 0)),
            pl.BlockSpec((1, PAST_LEN, MLA_ROPE), lambda s: (s, 0, 0)),
            pl.BlockSpec(wuk.shape, full),
            pl.BlockSpec(wuv.shape, full),
        ],
        out_specs=pl.BlockSpec((DEC_SEQ, D_MODEL), lambda s: (s, 0)),
        out_shape=jax.ShapeDtypeStruct((T_SAMPLE, D_MODEL), BF16),
        scratch_shapes=[
            pltpu.VMEM((HEADS * DEC_SEQ, MLA_RANK), BF16),
            pltpu.VMEM((HEADS * DEC_SEQ, 128), BF16),
        ],
        compiler_params=_params("arbitrary"),
        name="mla_sample",
    )(q, ckv, kr, cache_ckv, cache_kr, wuk, wuv)


def _bias_rows(rel_bias):
    far = jnp.broadcast_to(rel_bias[:, 2 * REL_CLIP:], (HEADS, BAND_F // 4))
    mid = rel_bias[:, 1:][:, ::-1]
    return (jnp.concatenate([far, mid, far], axis=1) * LOG2E).reshape(HEADS, 1, BAND_F)


def _rope_tables(pos):
    half = MLA_ROPE // 2
    inv = 1.0 / (ROPE_THETA ** (jnp.arange(half, dtype=F32) / half))
    ang = pos.astype(F32)[:, None] * inv[None, :]
    c, s = jnp.cos(ang), jnp.sin(ang)
    z = jnp.zeros((pos.shape[0], 128 - MLA_ROPE), F32)
    return jnp.concatenate([c, c, z], axis=1), jnp.concatenate([-s, s, z], axis=1)


PROMPT_TM_QKV = 1024
MLA_PROJ_TM = 256


def kernel(x_prompt, x_sample, cache_a_k, cache_a_v, cache_mla_ckv, cache_mla_kr, ln_mix_pre, ln_mix_post, ln_ffn_pre, ln_ffn_post, a_w_qkv, a_w_o, a_rel_bias, mla_w_dq, mla_q_norm, mla_w_uq, mla_w_dkv, mla_kv_norm, mla_w_uk, mla_w_uv, mla_w_o, ffn_w1, ffn_w2):
    xp = x_prompt.reshape(T_PROMPT, D_MODEL)
    xs = x_sample.reshape(T_SAMPLE, D_MODEL)
    w1 = ffn_w1.astype(BF16)
    w2 = ffn_w2.astype(BF16)

    def mix_out_and_ffn(i, o, w_o, x):
        x = proj_residual(o, w_o, ln_mix_post[i], x, tm=512)
        return ffn(x, ln_ffn_pre[i], w1, w2, ln_ffn_post[i], layer=i, tm=512, tf=1024)

    w_qkv = a_w_qkv[0].astype(BF16)
    w_o = a_w_o[0].astype(BF16)
    f_rows = _bias_rows(a_rel_bias[0])
    hm_p, k_p, v_p = norm_qkv(xp, ln_mix_pre[0], w_qkv, tm=PROMPT_TM_QKV, keep=BAND_ROWS,
                              tiles_per_keep=SEQ // PROMPT_TM_QKV)
    hm_s, k_s, v_s = norm_qkv(xs, ln_mix_pre[0], w_qkv, tm=512, keep=512, tiles_per_keep=1)
    o_p = band_prompt(hm_p, f_rows)
    o_s = band_sample(hm_s,
                      cache_a_k[0], cache_a_v[0], f_rows)
    xp = mix_out_and_ffn(0, o_p, w_o, xp)
    xs = mix_out_and_ffn(0, o_s, w_o, xs)

    pad = jnp.zeros((D_MODEL, 128 - MLA_ROPE), F32)
    w_d = jnp.concatenate([mla_w_dq[0], mla_w_dkv[0], pad], axis=1).astype(BF16)
    wq = jnp.pad(mla_w_uq[0], ((0, 0), (0, 0), (0, MLA_QK - MLA_NOPE - MLA_ROPE)))
    wq = wq.reshape(MLA_RANK, HEADS * MLA_QK).astype(BF16)
    wuk = mla_w_uk[0].reshape(MLA_RANK, HEADS * MLA_NOPE).astype(BF16)
    wuv = mla_w_uv[0].reshape(MLA_RANK, HEADS * HEAD_DIM).astype(BF16)
    w_o = mla_w_o[0].astype(BF16)
    cos_p, sin_p = _rope_tables(jnp.arange(SEQ))
    cos_s, sin_s = _rope_tables(jnp.tile(PAST_LEN + jnp.arange(DEC_SEQ), MLA_PROJ_TM // DEC_SEQ))

    d_p = norm_matmul(xp, ln_mix_pre[1], w_d, tm=512)
    d_s = norm_matmul(xs, ln_mix_pre[1], w_d, tm=512)
    ckv_p, kr_p, q_p, kk_p, vv_p = mla_proj(d_p, mla_q_norm[0], mla_kv_norm[0], wq, wuk, wuv,
                                            cos_p, sin_p, tm=MLA_PROJ_TM, with_kv=True)
    ckv_s, kr_s, q_s = mla_proj(d_s, mla_q_norm[0], mla_kv_norm[0], wq, wuk, wuv,
                                cos_s, sin_s, tm=MLA_PROJ_TM, with_kv=False)
    o_p = mla_prompt(q_p, kk_p, vv_p)
    o_s = mla_sample(q_s, ckv_s, kr_s, cache_mla_ckv[0], cache_mla_kr[0], wuk, wuv)
    xp = mix_out_and_ffn(1, o_p, w_o, xp)
    xs = mix_out_and_ffn(1, o_s, w_o, xs)

    return (
        xp.reshape(BATCH, SEQ, D_MODEL),
        xs.reshape(DEC_BATCH, DEC_SEQ, D_MODEL),
        k_p.reshape(1, BATCH, BAND_ROWS, HEADS, HEAD_DIM),
        v_p.reshape(1, BATCH, BAND_ROWS, HEADS, HEAD_DIM),
        k_s.reshape(1, DEC_BATCH, DEC_SEQ, HEADS, HEAD_DIM),
        v_s.reshape(1, DEC_BATCH, DEC_SEQ, HEADS, HEAD_DIM),
        ckv_p.reshape(1, BATCH, SEQ, MLA_RANK),
        kr_p.reshape(1, BATCH, SEQ, MLA_ROPE),
        ckv_s.reshape(1, DEC_BATCH, DEC_SEQ, MLA_RANK),
        kr_s.reshape(1, DEC_BATCH, DEC_SEQ, MLA_ROPE),
    )
```

```python
import functools
import math

import jax
import jax.numpy as jnp
from jax import lax
from jax.experimental import pallas as pl
from jax.experimental.pallas import tpu as pltpu

F32 = jnp.float32
BF16 = jnp.bfloat16

D_MODEL = 2048
BATCH = 4
SEQ = 4096
DEC_BATCH = 16
DEC_SEQ = 64
PAST_LEN = 2048
CHUNK = 64
HEADS = 16
HEAD_DIM = 128
BAND_ROWS = 512
REL_CLIP = 256
MLA_RANK = 512
MLA_NOPE = 128
MLA_ROPE = 64
MLA_QK = 256
ROPE_THETA = 10000.0
D_FF = 8192
EPS = 1e-6
NEG = -1e30
LOG2E = math.log2(math.e)

T_PROMPT = BATCH * SEQ
T_SAMPLE = DEC_BATCH * DEC_SEQ

VMEM_LIMIT_BYTES = 58 * 1024 * 1024

BAND_TQ = 256
BAND_TK = BAND_ROWS + BAND_TQ
BAND_F = 1024
BAND_NQ = SEQ // BAND_TQ
BAND_SUB = 32
BAND_Q_SCALE = HEAD_DIM ** -0.5 * LOG2E

MLA_TQ = 512
MLA_TK = 512
MLA_TK_MAX = 1024
MLA_NQ = SEQ // MLA_TQ
MLA_SUB = 32
MLA_PV_SUB = 256
MLA_Q_SCALE = (MLA_NOPE + MLA_ROPE) ** -0.5 * LOG2E

MLA_SAMPLE_HEADS = 16
QKV_TN = 1024
QKV_CHUNK = 256
PROJ_SUB = 128
FFN_CHUNK = 512

_NT = (((1,), (1,)), ((), ()))


def _params(*sem):
    return pltpu.CompilerParams(dimension_semantics=sem, vmem_limit_bytes=VMEM_LIMIT_BYTES)


def _rms(x, g):
    ms = jnp.mean(x * x, axis=-1, keepdims=True)
    return x * lax.rsqrt(ms + EPS) * g


def _norm_matmul_kernel(x_ref, g_ref, w_ref, o_ref):
    h = _rms(x_ref[...], g_ref[...]).astype(BF16)
    o_ref[...] = jnp.dot(h, w_ref[...], preferred_element_type=F32)


def norm_matmul(x, g, w, *, tm):
    m, k = x.shape
    n = w.shape[1]
    return pl.pallas_call(
        _norm_matmul_kernel,
        grid=(m // tm,),
        in_specs=[
            pl.BlockSpec((tm, k), lambda i: (i, 0)),
            pl.BlockSpec((1, k), lambda i: (0, 0)),
            pl.BlockSpec((k, n), lambda i: (0, 0)),
        ],
        out_specs=pl.BlockSpec((tm, n), lambda i: (i, 0)),
        out_shape=jax.ShapeDtypeStruct((m, n), F32),
        compiler_params=_params("arbitrary"),
        name="norm_matmul",
    )(x, g.reshape(1, k), w)


def _norm_qkv_kernel(x_ref, g_ref, w_ref, hm_ref, kf_ref, vf_ref, h_ref, *, keep):
    j = pl.program_id(1)

    @pl.when(j == 0)
    def _():
        h_ref[...] = _rms(x_ref[...], g_ref[...]).astype(BF16)

    h = h_ref[...]
    tm = h.shape[0]

    def project(scale, f_ref, f_col):
        for c in range(QKV_TN // QKV_CHUNK):
            cols = slice(c * QKV_CHUNK, (c + 1) * QKV_CHUNK)
            acc = jnp.dot(h, w_ref[:, cols], preferred_element_type=F32)
            if f_ref is not None:
                f_ref[:, f_col + c * QKV_CHUNK:f_col + (c + 1) * QKV_CHUNK] = acc[tm - keep:, :]
            if scale is not None:
                acc = acc * scale
            for hh in range(QKV_CHUNK // HEAD_DIM):
                hm_ref[c * (QKV_CHUNK // HEAD_DIM) + hh] = (
                    acc[:, hh * HEAD_DIM:(hh + 1) * HEAD_DIM].astype(BF16))

    for jj, f_ref in enumerate((kf_ref, kf_ref, vf_ref, vf_ref)):
        pl.when(j == jj)(functools.partial(project, None, f_ref, (jj % 2) * QKV_TN))

    pl.when(j >= 4)(functools.partial(project, BAND_Q_SCALE, None, 0))


def norm_qkv(x, g, w, *, tm, keep, tiles_per_keep):
    m, k = x.shape
    nb = 3 * D_MODEL // QKV_TN
    heads_per_block = QKV_TN // HEAD_DIM
    n_keep = m // (tm * tiles_per_keep) * keep

    def col(j):
        return (j + 2) % nb

    return pl.pallas_call(
        functools.partial(_norm_qkv_kernel, keep=keep),
        grid=(m // tm, nb),
        in_specs=[
            pl.BlockSpec((tm, k), lambda i, j: (i, 0)),
            pl.BlockSpec((1, k), lambda i, j: (0, 0)),
            pl.BlockSpec((k, QKV_TN), lambda i, j: (0, col(j))),
        ],
        out_specs=[
            pl.BlockSpec((heads_per_block, tm, HEAD_DIM), lambda i, j: (col(j), i, 0)),
            pl.BlockSpec((keep, D_MODEL), lambda i, j: (i // tiles_per_keep, 0)),
            pl.BlockSpec((keep, D_MODEL), lambda i, j: (i // tiles_per_keep, 0)),
        ],
        out_shape=[
            jax.ShapeDtypeStruct((3 * HEADS, m, HEAD_DIM), BF16),
            jax.ShapeDtypeStruct((n_keep, D_MODEL), F32),
            jax.ShapeDtypeStruct((n_keep, D_MODEL), F32),
        ],
        scratch_shapes=[pltpu.VMEM((tm, k), BF16)],
        compiler_params=_params("arbitrary", "arbitrary"),
        name="norm_qkv",
    )(x, g.reshape(1, k), w)


def _toeplitz_bias(f_row, rows):
    x = jnp.broadcast_to(f_row, (rows, BAND_F))
    return pltpu.roll(x, 0, 1, stride=1, stride_axis=0)


def _band_prompt_kernel(q_ref, k_ref, v_ref, f_ref, o_ref, kpad_ref, vpad_ref, bias_ref, s_ref, s2_ref,
                        p_ref, p2_ref):
    @pl.when((pl.program_id(0) == 0) & (pl.program_id(1) == 0))
    def _():
        kpad_ref[0:BAND_ROWS, :] = jnp.zeros((BAND_ROWS, HEAD_DIM), BF16)
        vpad_ref[0:BAND_ROWS, 0:HEAD_DIM] = jnp.zeros((BAND_ROWS, HEAD_DIM), BF16)
        vpad_ref[:, HEAD_DIM:] = jnp.ones((BAND_ROWS + SEQ, HEAD_DIM), BF16)

    @pl.when(pl.program_id(1) == 0)
    def _():
        r = lax.broadcasted_iota(jnp.int32, (BAND_TQ, BAND_TK), 0) // CHUNK
        j = lax.broadcasted_iota(jnp.int32, (BAND_TQ, BAND_TK), 1)
        jc = j // CHUNK
        allowed = (jc >= r) & (jc <= r + BAND_ROWS // CHUNK)
        base = jnp.where(allowed, _toeplitz_bias(f_ref[0], BAND_TQ)[:, :BAND_TK], NEG)
        bias_ref[2] = base
        bias_ref[1] = jnp.where(j >= BAND_ROWS - BAND_TQ, base, NEG)
        bias_ref[0] = jnp.where(j >= BAND_ROWS, base, NEG)

    kpad_ref[BAND_ROWS:, :] = k_ref[0]
    vpad_ref[BAND_ROWS:, 0:HEAD_DIM] = v_ref[0]

    def scores(qi, dst_ref):
        start = qi * BAND_TQ
        dst_ref[...] = lax.dot_general(q_ref[0, start:start + BAND_TQ, :],
                                       kpad_ref[start:start + BAND_TK, :], _NT,
                                       preferred_element_type=F32)

    def softmax(qi, cur_ref, p_dst):
        for u in range(BAND_TQ // BAND_SUB):
            rows = slice(u * BAND_SUB, (u + 1) * BAND_SUB)
            s = cur_ref[rows, :] + bias_ref[min(qi, 2), rows, :]
            p_dst[rows, :] = jnp.exp2(s - jnp.max(s, axis=-1, keepdims=True)).astype(BF16)

    def values(qi, p_src):
        start = qi * BAND_TQ
        pv = jnp.dot(p_src[...], vpad_ref[start:start + BAND_TK, :], preferred_element_type=F32)
        o_ref[start:start + BAND_TQ, :] = (pv[:, :HEAD_DIM] / pv[:, HEAD_DIM:]).astype(BF16)

    s_bufs = (s_ref, s2_ref)
    p_bufs = (p_ref, p2_ref)
    scores(0, s_bufs[0])
    scores(1, s_bufs[1])
    softmax(0, s_bufs[0], p_bufs[0])
    for t in range(BAND_NQ):
        if t + 2 < BAND_NQ:
            scores(t + 2, s_bufs[t % 2])
        if t + 1 < BAND_NQ:
            softmax(t + 1, s_bufs[(t + 1) % 2], p_bufs[(t + 1) % 2])
        values(t, p_bufs[t % 2])


def band_prompt(qkv_hm, f_rows):
    return pl.pallas_call(
        _band_prompt_kernel,
        grid=(HEADS, BATCH),
        in_specs=[
            pl.BlockSpec((1, SEQ, HEAD_DIM), lambda h, b: (h, b, 0)),
            pl.BlockSpec((1, SEQ, HEAD_DIM), lambda h, b: (HEADS + h, b, 0)),
            pl.BlockSpec((1, SEQ, HEAD_DIM), lambda h, b: (2 * HEADS + h, b, 0)),
            pl.BlockSpec((1, 1, BAND_F), lambda h, b: (h, 0, 0)),
        ],
        out_specs=pl.BlockSpec((SEQ, HEAD_DIM), lambda h, b: (b, h)),
        out_shape=jax.ShapeDtypeStruct((T_PROMPT, D_MODEL), BF16),
        scratch_shapes=[
            pltpu.VMEM((BAND_ROWS + SEQ, HEAD_DIM), BF16),
            pltpu.VMEM((BAND_ROWS + SEQ, 2 * HEAD_DIM), BF16),
            pltpu.VMEM((3, BAND_TQ, BAND_TK), F32),
            pltpu.VMEM((BAND_TQ, BAND_TK), F32),
            pltpu.VMEM((BAND_TQ, BAND_TK), F32),
            pltpu.VMEM((BAND_TQ, BAND_TK), BF16),
            pltpu.VMEM((BAND_TQ, BAND_TK), BF16),
        ],
        compiler_params=_params("arbitrary", "arbitrary"),
        name="band_prompt",
    )(qkv_hm, qkv_hm, qkv_hm, f_rows)


def _band_sample_kernel(q_ref, kn_ref, vn_ref, ck_ref, cv_ref, f_ref, o_ref):
    for h in range(HEADS):
        cols = slice(h * HEAD_DIM, (h + 1) * HEAD_DIM)
        bias = _toeplitz_bias(f_ref[h], DEC_SEQ)
        q = q_ref[h]
        kc = ck_ref[0, pl.ds(h, BAND_ROWS, stride=HEADS), :].astype(BF16)
        vc = cv_ref[0, pl.ds(h, BAND_ROWS, stride=HEADS), :].astype(BF16)
        s_c = lax.dot_general(q, kc, _NT, preferred_element_type=F32) + bias[:, :BAND_ROWS]
        s_n = (lax.dot_general(q, kn_ref[h], _NT, preferred_element_type=F32)
               + bias[:, BAND_ROWS:BAND_ROWS + DEC_SEQ])
        m = jnp.maximum(jnp.max(s_c, axis=-1, keepdims=True), jnp.max(s_n, axis=-1, keepdims=True))
        p_c = jnp.exp2(s_c - m)
        p_n = jnp.exp2(s_n - m)
        l = jnp.sum(p_c, axis=-1, keepdims=True) + jnp.sum(p_n, axis=-1, keepdims=True)
        o = (jnp.dot(p_c.astype(BF16), vc, preferred_element_type=F32)
             + jnp.dot(p_n.astype(BF16), vn_ref[h], preferred_element_type=F32)) / l
        o_ref[:, cols] = o.astype(BF16)


def band_sample(qkv_hm, cache_k, cache_v, f_rows):
    return pl.pallas_call(
        _band_sample_kernel,
        grid=(DEC_BATCH,),
        in_specs=[
            pl.BlockSpec((HEADS, DEC_SEQ, HEAD_DIM), lambda s: (0, s, 0)),
            pl.BlockSpec((HEADS, DEC_SEQ, HEAD_DIM), lambda s: (1, s, 0)),
            pl.BlockSpec((HEADS, DEC_SEQ, HEAD_DIM), lambda s: (2, s, 0)),
            pl.BlockSpec((1, BAND_ROWS * HEADS, HEAD_DIM), lambda s: (s, 0, 0)),
            pl.BlockSpec((1, BAND_ROWS * HEADS, HEAD_DIM), lambda s: (s, 0, 0)),
            pl.BlockSpec((HEADS, 1, BAND_F), lambda s: (0, 0, 0)),
        ],
        out_specs=pl.BlockSpec((DEC_SEQ, D_MODEL), lambda s: (s, 0)),
        out_shape=jax.ShapeDtypeStruct((T_SAMPLE, D_MODEL), BF16),
        compiler_params=_params("arbitrary"),
        name="band_sample",
    )(qkv_hm, qkv_hm, qkv_hm, cache_k, cache_v, f_rows)


def _proj_residual_kernel(a_ref, w_ref, g_ref, gn_ref, x_ref, o_ref, h_ref):
    w = w_ref[...]
    g = g_ref[...]
    gn = gn_ref[...]
    for u in range(a_ref.shape[0] // PROJ_SUB):
        rows = slice(u * PROJ_SUB, (u + 1) * PROJ_SUB)
        y = jnp.dot(a_ref[rows, :], w, preferred_element_type=F32)
        x1 = x_ref[rows, :] + _rms(y, g)
        o_ref[rows, :] = x1
        h_ref[rows, :] = _rms(x1, gn).astype(BF16)


def proj_residual(a, w, g, g_next, x, *, tm):
    m, k = a.shape
    n = w.shape[1]
    return pl.pallas_call(
        _proj_residual_kernel,
        grid=(m // tm,),
        in_specs=[
            pl.BlockSpec((tm, k), lambda i: (i, 0)),
            pl.BlockSpec((k, n), lambda i: (0, 0)),
            pl.BlockSpec((1, n), lambda i: (0, 0)),
            pl.BlockSpec((1, n), lambda i: (0, 0)),
            pl.BlockSpec((tm, n), lambda i: (i, 0)),
        ],
        out_specs=[
            pl.BlockSpec((tm, n), lambda i: (i, 0)),
            pl.BlockSpec((tm, n), lambda i: (i, 0)),
        ],
        out_shape=[
            jax.ShapeDtypeStruct((m, n), F32),
            jax.ShapeDtypeStruct((m, n), BF16),
        ],
        compiler_params=_params("arbitrary"),
        name="proj_residual",
    )(a, w, g.reshape(1, n), g_next.reshape(1, n), x)


def _ffn_kernel(x_ref, h_ref, w1_ref, w2_ref, g2_ref, o_ref, ssq_ref):
    j = pl.program_id(1)

    @pl.when(j == 0)
    def _():
        o_ref[...] = jnp.zeros(o_ref.shape, F32)

    a = jnp.maximum(jnp.dot(h_ref[...], w1_ref[...], preferred_element_type=F32), 0.0)
    a = (a * a).astype(BF16)
    for c in range(o_ref.shape[1] // FFN_CHUNK):
        cols = slice(c * FFN_CHUNK, (c + 1) * FFN_CHUNK)
        y = o_ref[:, cols] + jnp.dot(a, w2_ref[:, cols], preferred_element_type=F32)
        o_ref[:, cols] = y
        sq = y * y
        ssq_ref[:, c * 128:(c + 1) * 128] = functools.reduce(
            jnp.add, [sq[:, b * 128:(b + 1) * 128] for b in range(FFN_CHUNK // 128)])

    @pl.when(j == pl.num_programs(1) - 1)
    def _():
        ms = jnp.sum(ssq_ref[...], axis=-1, keepdims=True) * (1.0 / o_ref.shape[1])
        o_ref[...] = x_ref[...] + o_ref[...] * lax.rsqrt(ms + EPS) * g2_ref[...]


def ffn(x, h, w1, w2, g2, *, layer, tm, tf):
    m, d = x.shape
    f = w1.shape[2]
    return pl.pallas_call(
        _ffn_kernel,
        grid=(m // tm, f // tf),
        in_specs=[
            pl.BlockSpec((tm, d), lambda i, j: (i, 0)),
            pl.BlockSpec((tm, d), lambda i, j: (i, 0)),
            pl.BlockSpec((None, d, tf), lambda i, j: (layer, 0, j)),
            pl.BlockSpec((None, tf, d), lambda i, j: (layer, j, 0)),
            pl.BlockSpec((1, d), lambda i, j: (0, 0)),
        ],
        out_specs=pl.BlockSpec((tm, d), lambda i, j: (i, 0)),
        out_shape=jax.ShapeDtypeStruct((m, d), F32),
        scratch_shapes=[pltpu.VMEM((tm, 128 * (d // FFN_CHUNK)), F32)],
        compiler_params=_params("arbitrary", "arbitrary"),
        name="ffn",
    )(x, h, w1, w2, g2.reshape(1, d))


def _rope128(x, cos, sin):
    lane = lax.broadcasted_iota(jnp.int32, x.shape, 1)
    half = MLA_ROPE // 2
    swapped = jnp.where(lane < half, pltpu.roll(x, 128 - half, 1), pltpu.roll(x, half, 1))
    return x * cos + swapped * sin


def _mla_proj_kernel(d_ref, qn_ref, kvn_ref, wq_ref, wuk_ref, wuv_ref, cos_ref, sin_ref,
                     ckv_ref, kr_ref, q_ref, *kv_refs):
    d = d_ref[...]
    cos = cos_ref[...]
    sin = sin_ref[...]
    cq = _rms(d[:, :MLA_RANK], qn_ref[...]).astype(BF16)
    ckv = _rms(d[:, MLA_RANK:2 * MLA_RANK], kvn_ref[...])
    ckv_ref[...] = ckv
    kr = _rope128(d[:, 2 * MLA_RANK:], cos, sin)
    kr_ref[...] = kr[:, :MLA_ROPE]

    q = jnp.dot(cq, wq_ref[...], preferred_element_type=F32) * MLA_Q_SCALE
    for h in range(HEADS):
        base = h * MLA_QK
        q_ref[h, :, 0:MLA_NOPE] = q[:, base:base + MLA_NOPE].astype(BF16)
        q_ref[h, :, MLA_NOPE:] = _rope128(q[:, base + MLA_NOPE:base + MLA_QK], cos, sin).astype(BF16)

    if kv_refs:
        k_ref, v_ref = kv_refs
        kr_b = kr.astype(BF16)
        ckv_b = ckv.astype(BF16)
        kn = jnp.dot(ckv_b, wuk_ref[...], preferred_element_type=F32)
        vv = jnp.dot(ckv_b, wuv_ref[...], preferred_element_type=F32)
        ones = jnp.ones((d.shape[0], HEAD_DIM), BF16)
        for h in range(HEADS):
            k_ref[h, :, 0:MLA_NOPE] = kn[:, h * MLA_NOPE:(h + 1) * MLA_NOPE].astype(BF16)
            k_ref[h, :, MLA_NOPE:] = kr_b
            v_ref[h, :, 0:HEAD_DIM] = vv[:, h * HEAD_DIM:(h + 1) * HEAD_DIM].astype(BF16)
            v_ref[h, :, HEAD_DIM:] = ones


def mla_proj(d, q_norm, kv_norm, wq, wuk, wuv, cos, sin, *, tm, with_kv):
    m = d.shape[0]
    full = lambda i: (0, 0)
    rope_tiles = cos.shape[0] // tm
    hm_spec = pl.BlockSpec((HEADS, tm, MLA_QK), lambda i: (0, i, 0))
    hm_shape = jax.ShapeDtypeStruct((HEADS, m, MLA_QK), BF16)
    n_hm = 3 if with_kv else 1
    return pl.pallas_call(
        _mla_proj_kernel,
        grid=(m // tm,),
        in_specs=[
            pl.BlockSpec((tm, d.shape[1]), lambda i: (i, 0)),
            pl.BlockSpec((1, MLA_RANK), full),
            pl.BlockSpec((1, MLA_RANK), full),
            pl.BlockSpec(wq.shape, full),
            pl.BlockSpec(wuk.shape, full),
            pl.BlockSpec(wuv.shape, full),
            pl.BlockSpec((tm, 128), lambda i: (i % rope_tiles, 0)),
            pl.BlockSpec((tm, 128), lambda i: (i % rope_tiles, 0)),
        ],
        out_specs=[
            pl.BlockSpec((tm, MLA_RANK), lambda i: (i, 0)),
            pl.BlockSpec((tm, MLA_ROPE), lambda i: (i, 0)),
        ] + [hm_spec] * n_hm,
        out_shape=[
            jax.ShapeDtypeStruct((m, MLA_RANK), F32),
            jax.ShapeDtypeStruct((m, MLA_ROPE), F32),
        ] + [hm_shape] * n_hm,
        compiler_params=_params("arbitrary"),
        name="mla_proj",
    )(d, q_norm.reshape(1, -1), kv_norm.reshape(1, -1), wq, wuk, wuv, cos, sin)


def _mla_prompt_kernel(q_ref, k_ref, v_ref, o_ref, s_ref, s2_ref, p_ref, p2_ref, alpha_ref, alpha2_ref,
                       bias_ref, m_ref, acc_ref):
    @pl.when((pl.program_id(0) == 0) & (pl.program_id(1) == 0))
    def _():
        rc = lax.broadcasted_iota(jnp.int32, (MLA_TQ, MLA_TK), 0) // CHUNK
        cc = lax.broadcasted_iota(jnp.int32, (MLA_TQ, MLA_TK), 1) // CHUNK
        bias_ref[...] = jnp.where(cc <= rc, 0.0, NEG)

    def scores(qi, k0, klen, dst_ref):
        q = q_ref[0, qi * MLA_TQ:(qi + 1) * MLA_TQ, :]
        k = k_ref[0, k0:k0 + klen, :]
        dst_ref[:, :klen] = lax.dot_general(q, k, _NT, preferred_element_type=F32)

    def softmax(qi, k0, klen, cur_ref, p_dst, alpha_dst):
        first = k0 == 0
        last = k0 + klen == (qi + 1) * MLA_TQ
        for u in range(MLA_TQ // MLA_SUB):
            rows = slice(u * MLA_SUB, (u + 1) * MLA_SUB)
            if last:
                parts = [cur_ref[rows, klen - MLA_TK:klen] + bias_ref[rows, :]]
                if klen > MLA_TK:
                    parts.insert(0, cur_ref[rows, :klen - MLA_TK])
            else:
                parts = [cur_ref[rows, :klen]]
            m_cur = functools.reduce(jnp.maximum, [jnp.max(x, axis=-1, keepdims=True) for x in parts])
            if first:
                m_new = jnp.broadcast_to(m_cur, (MLA_SUB, 128))
            else:
                m_prev = m_ref[rows, :]
                m_new = jnp.maximum(m_prev, m_cur)
                alpha_dst[rows, :] = jnp.exp2(m_prev - m_new)
            if not last:
                m_ref[rows, :] = m_new
            col = 0
            for x in parts:
                p_dst[rows, col:col + x.shape[1]] = jnp.exp2(
                    x - jnp.tile(m_new, (1, x.shape[1] // 128))).astype(BF16)
                col += x.shape[1]

    def values(qi, k0, klen, p_src, alpha_src):
        first = k0 == 0
        last = k0 + klen == (qi + 1) * MLA_TQ
        v = v_ref[0, k0:k0 + klen, :]
        for u in range(MLA_TQ // MLA_PV_SUB):
            rows = slice(u * MLA_PV_SUB, (u + 1) * MLA_PV_SUB)
            acc = jnp.dot(p_src[rows, :klen], v, preferred_element_type=F32)
            if not first:
                acc = acc + jnp.tile(alpha_src[rows, :], (1, 2)) * acc_ref[rows, :]
            if last:
                o_ref[qi * MLA_TQ + u * MLA_PV_SUB:qi * MLA_TQ + (u + 1) * MLA_PV_SUB, :] = (
                    acc[:, :HEAD_DIM] / acc[:, HEAD_DIM:]).astype(BF16)
            else:
                acc_ref[rows, :] = acc

    blocks = [(qi, k0, min(MLA_TK_MAX, (qi + 1) * MLA_TQ - k0))
              for qi in range(MLA_NQ) for k0 in range(0, (qi + 1) * MLA_TQ, MLA_TK_MAX)]
    n = len(blocks)
    s_bufs = (s_ref, s2_ref)
    p_bufs = (p_ref, p2_ref)
    a_bufs = (alpha_ref, alpha2_ref)
    scores(*blocks[0], s_bufs[0])
    scores(*blocks[1], s_bufs[1])
    softmax(*blocks[0], s_bufs[0], p_bufs[0], a_bufs[0])
    for t in range(n):
        if t + 2 < n:
            scores(*blocks[t + 2], s_bufs[t % 2])
        if t + 1 < n:
            softmax(*blocks[t + 1], s_bufs[(t + 1) % 2], p_bufs[(t + 1) % 2], a_bufs[(t + 1) % 2])
        values(*blocks[t], p_bufs[t % 2], a_bufs[t % 2])


def mla_prompt(q, k, v):
    return pl.pallas_call(
        _mla_prompt_kernel,
        grid=(BATCH, HEADS),
        in_specs=[
            pl.BlockSpec((1, SEQ, MLA_QK), lambda b, h: (h, b, 0)),
            pl.BlockSpec((1, SEQ, MLA_QK), lambda b, h: (h, b, 0)),
            pl.BlockSpec((1, SEQ, 2 * HEAD_DIM), lambda b, h: (h, b, 0)),
        ],
        out_specs=pl.BlockSpec((SEQ, HEAD_DIM), lambda b, h: (b, h)),
        out_shape=jax.ShapeDtypeStruct((T_PROMPT, D_MODEL), BF16),
        scratch_shapes=[
            pltpu.VMEM((MLA_TQ, MLA_TK_MAX), F32),
            pltpu.VMEM((MLA_TQ, MLA_TK_MAX), F32),
            pltpu.VMEM((MLA_TQ, MLA_TK_MAX), BF16),
            pltpu.VMEM((MLA_TQ, MLA_TK_MAX), BF16),
            pltpu.VMEM((MLA_TQ, 128), F32),
            pltpu.VMEM((MLA_TQ, 128), F32),
            pltpu.VMEM((MLA_TQ, MLA_TK), F32),
            pltpu.VMEM((MLA_TQ, 128), F32),
            pltpu.VMEM((MLA_TQ, 2 * HEAD_DIM), F32),
        ],
        compiler_params=_params("arbitrary", "arbitrary"),
        name="mla_prompt",
    )(q, k, v)


def _mla_sample_kernel(q_ref, cn_ref, rn_ref, cc_ref, cr_ref, wuk_ref, wuv_ref, o_ref,
                       ql_ref, qr_ref):
    cc = cc_ref[0].astype(BF16)
    cr = cr_ref[0].astype(BF16)
    cn = cn_ref[...].astype(BF16)
    rn = rn_ref[...].astype(BF16)
    for g in range(HEADS // MLA_SAMPLE_HEADS):
        heads = range(g * MLA_SAMPLE_HEADS, (g + 1) * MLA_SAMPLE_HEADS)
        grp = slice(g * MLA_SAMPLE_HEADS * DEC_SEQ, (g + 1) * MLA_SAMPLE_HEADS * DEC_SEQ)
        for h in heads:
            rows = slice(h * DEC_SEQ, (h + 1) * DEC_SEQ)
            ql_ref[rows, :] = lax.dot_general(
                q_ref[h, :, 0:MLA_NOPE], wuk_ref[:, h * MLA_NOPE:(h + 1) * MLA_NOPE], _NT,
                preferred_element_type=F32).astype(BF16)
            qr_ref[rows, :] = q_ref[h, :, MLA_NOPE:]
        ql = ql_ref[grp, :]
        qr = qr_ref[grp, :][:, :MLA_ROPE]
        s_c = (lax.dot_general(ql, cc, _NT, preferred_element_type=F32)
               + lax.dot_general(qr, cr, _NT, preferred_element_type=F32))
        s_n = (lax.dot_general(ql, cn, _NT, preferred_element_type=F32)
               + lax.dot_general(qr, rn, _NT, preferred_element_type=F32))
        m = jnp.maximum(jnp.max(s_c, axis=-1, keepdims=True), jnp.max(s_n, axis=-1, keepdims=True))
        p_c = jnp.exp2(s_c - m)
        p_n = jnp.exp2(s_n - m)
        l = jnp.sum(p_c, axis=-1, keepdims=True) + jnp.sum(p_n, axis=-1, keepdims=True)
        o_lat = ((jnp.dot(p_c.astype(BF16), cc, preferred_element_type=F32)
                  + jnp.dot(p_n.astype(BF16), cn, preferred_element_type=F32)) / l).astype(BF16)
        for i, h in enumerate(heads):
            cols = slice(h * HEAD_DIM, (h + 1) * HEAD_DIM)
            o_ref[:, cols] = jnp.dot(o_lat[i * DEC_SEQ:(i + 1) * DEC_SEQ], wuv_ref[:, cols],
                                     preferred_element_type=F32).astype(BF16)


def mla_sample(q, ckv, kr, cache_ckv, cache_kr, wuk, wuv):
    full = lambda s: (0, 0)
    return pl.pallas_call(
        _mla_sample_kernel,
        grid=(DEC_BATCH,),
        in_specs=[
            pl.BlockSpec((HEADS, DEC_SEQ, MLA_QK), lambda s: (0, s, 0)),
            pl.BlockSpec((DEC_SEQ, MLA_RANK), lambda s: (s, 0)),
            pl.BlockSpec((DEC_SEQ, MLA_ROPE), lambda s: (s, 0)),
            pl.BlockSpec((1, PAST_LEN, MLA_RANK), lambda s: (s, 0, 0)),
            pl.BlockSpec((1, PAST_LEN, MLA_ROPE), lambda s: (s, 0, 0)),
            pl.BlockSpec(wuk.shape, full),
            pl.BlockSpec(wuv.shape, full),
        ],
        out_specs=pl.BlockSpec((DEC_SEQ, D_MODEL), lambda s: (s, 0)),
        out_shape=jax.ShapeDtypeStruct((T_SAMPLE, D_MODEL), BF16),
        scratch_shapes=[
            pltpu.VMEM((HEADS * DEC_SEQ, MLA_RANK), BF16),
            pltpu.VMEM((HEADS * DEC_SEQ, 128), BF16),
        ],
        compiler_params=_params("arbitrary"),
        name="mla_sample",
    )(q, ckv, kr, cache_ckv, cache_kr, wuk, wuv)


def _bias_rows(rel_bias):
    far = jnp.broadcast_to(rel_bias[:, 2 * REL_CLIP:], (HEADS, BAND_F // 4))
    mid = rel_bias[:, 1:][:, ::-1]
    return (jnp.concatenate([far, mid, far], axis=1) * LOG2E).reshape(HEADS, 1, BAND_F)


def _rope_tables(pos):
    half = MLA_ROPE // 2
    inv = 1.0 / (ROPE_THETA ** (jnp.arange(half, dtype=F32) / half))
    ang = pos.astype(F32)[:, None] * inv[None, :]
    c, s = jnp.cos(ang), jnp.sin(ang)
    z = jnp.zeros((pos.shape[0], 128 - MLA_ROPE), F32)
    return jnp.concatenate([c, c, z], axis=1), jnp.concatenate([-s, s, z], axis=1)


PROMPT_TM_QKV = 1024
MLA_PROJ_TM = 256


def kernel(x_prompt, x_sample, cache_a_k, cache_a_v, cache_mla_ckv, cache_mla_kr, ln_mix_pre, ln_mix_post, ln_ffn_pre, ln_ffn_post, a_w_qkv, a_w_o, a_rel_bias, mla_w_dq, mla_q_norm, mla_w_uq, mla_w_dkv, mla_kv_norm, mla_w_uk, mla_w_uv, mla_w_o, ffn_w1, ffn_w2):
    xp = x_prompt.reshape(T_PROMPT, D_MODEL)
    xs = x_sample.reshape(T_SAMPLE, D_MODEL)
    w1 = ffn_w1.astype(BF16)
    w2 = ffn_w2.astype(BF16)

    def mix_out_and_ffn(i, o, w_o, x):
        x, h = proj_residual(o, w_o, ln_mix_post[i], ln_ffn_pre[i], x, tm=512)
        return ffn(x, h, w1, w2, ln_ffn_post[i], layer=i, tm=512, tf=1024)

    w_qkv = a_w_qkv[0].astype(BF16)
    w_o = a_w_o[0].astype(BF16)
    f_rows = _bias_rows(a_rel_bias[0])
    hm_p, k_p, v_p = norm_qkv(xp, ln_mix_pre[0], w_qkv, tm=PROMPT_TM_QKV, keep=BAND_ROWS,
                              tiles_per_keep=SEQ // PROMPT_TM_QKV)
    hm_s, k_s, v_s = norm_qkv(xs, ln_mix_pre[0], w_qkv, tm=512, keep=512, tiles_per_keep=1)
    o_p = band_prompt(hm_p, f_rows)
    o_s = band_sample(hm_s,
                      cache_a_k[0].reshape(DEC_BATCH, BAND_ROWS * HEADS, HEAD_DIM),
                      cache_a_v[0].reshape(DEC_BATCH, BAND_ROWS * HEADS, HEAD_DIM), f_rows)
    xp = mix_out_and_ffn(0, o_p, w_o, xp)
    xs = mix_out_and_ffn(0, o_s, w_o, xs)

    pad = jnp.zeros((D_MODEL, 128 - MLA_ROPE), F32)
    w_d = jnp.concatenate([mla_w_dq[0], mla_w_dkv[0], pad], axis=1).astype(BF16)
    wq = jnp.pad(mla_w_uq[0], ((0, 0), (0, 0), (0, MLA_QK - MLA_NOPE - MLA_ROPE)))
    wq = wq.reshape(MLA_RANK, HEADS * MLA_QK).astype(BF16)
    wuk = mla_w_uk[0].reshape(MLA_RANK, HEADS * MLA_NOPE).astype(BF16)
    wuv = mla_w_uv[0].reshape(MLA_RANK, HEADS * HEAD_DIM).astype(BF16)
    w_o = mla_w_o[0].astype(BF16)
    cos_p, sin_p = _rope_tables(jnp.arange(SEQ))
    cos_s, sin_s = _rope_tables(jnp.tile(PAST_LEN + jnp.arange(DEC_SEQ), MLA_PROJ_TM // DEC_SEQ))

    d_p = norm_matmul(xp, ln_mix_pre[1], w_d, tm=512)
    d_s = norm_matmul(xs, ln_mix_pre[1], w_d, tm=512)
    ckv_p, kr_p, q_p, kk_p, vv_p = mla_proj(d_p, mla_q_norm[0], mla_kv_norm[0], wq, wuk, wuv,
                                            cos_p, sin_p, tm=MLA_PROJ_TM, with_kv=True)
    ckv_s, kr_s, q_s = mla_proj(d_s, mla_q_norm[0], mla_kv_norm[0], wq, wuk, wuv,
                                cos_s, sin_s, tm=MLA_PROJ_TM, with_kv=False)
    o_p = mla_prompt(q_p, kk_p, vv_p)
    o_s = mla_sample(q_s, ckv_s, kr_s, cache_mla_ckv[0], cache_mla_kr[0], wuk, wuv)
    xp = mix_out_and_ffn(1, o_p, w_o, xp)
    xs = mix_out_and_ffn(1, o_s, w_o, xs)

    return (
        xp.reshape(BATCH, SEQ, D_MODEL),
        xs.reshape(DEC_BATCH, DEC_SEQ, D_MODEL),
        k_p.reshape(1, BATCH, BAND_ROWS, HEADS, HEAD_DIM),
        v_p.reshape(1, BATCH, BAND_ROWS, HEADS, HEAD_DIM),
        k_s.reshape(1, DEC_BATCH, DEC_SEQ, HEADS, HEAD_DIM),
        v_s.reshape(1, DEC_BATCH, DEC_SEQ, HEADS, HEAD_DIM),
        ckv_p.reshape(1, BATCH, SEQ, MLA_RANK),
        kr_p.reshape(1, BATCH, SEQ, MLA_ROPE),
        ckv_s.reshape(1, DEC_BATCH, DEC_SEQ, MLA_RANK),
        kr_s.reshape(1, DEC_BATCH, DEC_SEQ, MLA_ROPE),
    )
```

```python
import functools
import math

import jax
import jax.numpy as jnp
from jax import lax
from jax.experimental import pallas as pl
from jax.experimental.pallas import tpu as pltpu

F32 = jnp.float32
BF16 = jnp.bfloat16

D_MODEL = 2048
BATCH = 4
SEQ = 4096
DEC_BATCH = 16
DEC_SEQ = 64
PAST_LEN = 2048
CHUNK = 64
HEADS = 16
HEAD_DIM = 128
BAND_ROWS = 512
REL_CLIP = 256
MLA_RANK = 512
MLA_NOPE = 128
MLA_ROPE = 64
MLA_QK = 256
ROPE_THETA = 10000.0
D_FF = 8192
EPS = 1e-6
NEG = -1e30
LOG2E = math.log2(math.e)

T_PROMPT = BATCH * SEQ
T_SAMPLE = DEC_BATCH * DEC_SEQ

VMEM_LIMIT_BYTES = 58 * 1024 * 1024

BAND_TQ = 256
BAND_TK = BAND_ROWS + BAND_TQ
BAND_F = 1024
BAND_NQ = SEQ // BAND_TQ
BAND_SUB = 32
BAND_Q_SCALE = HEAD_DIM ** -0.5 * LOG2E

MLA_TQ = 512
MLA_TK = 512
MLA_TK_MAX = 1024
MLA_NQ = SEQ // MLA_TQ
MLA_SUB = 32
MLA_HALF = MLA_TQ // 2
MLA_Q_SCALE = (MLA_NOPE + MLA_ROPE) ** -0.5 * LOG2E

MLA_SAMPLE_HEADS = 16
QKV_TN = 1024
QKV_CHUNK = 256
PROJ_SUB = 128
FFN_CHUNK = 512

_NT = (((1,), (1,)), ((), ()))


def _params(*sem):
    return pltpu.CompilerParams(dimension_semantics=sem, vmem_limit_bytes=VMEM_LIMIT_BYTES)


def _rms(x, g):
    ms = jnp.mean(x * x, axis=-1, keepdims=True)
    return x * lax.rsqrt(ms + EPS) * g


def _norm_qkv_kernel(x_ref, g_ref, w_ref, hm_ref, kf_ref, vf_ref, h_ref, *, keep):
    j = pl.program_id(1)

    @pl.when(j == 0)
    def _():
        h_ref[...] = _rms(x_ref[...], g_ref[...]).astype(BF16)

    h = h_ref[...]
    tm = h.shape[0]

    def project(scale, f_ref, f_col):
        for c in range(QKV_TN // QKV_CHUNK):
            cols = slice(c * QKV_CHUNK, (c + 1) * QKV_CHUNK)
            acc = jnp.dot(h, w_ref[:, cols], preferred_element_type=F32)
            if f_ref is not None:
                f_ref[:, f_col + c * QKV_CHUNK:f_col + (c + 1) * QKV_CHUNK] = acc[tm - keep:, :]
            if scale is not None:
                acc = acc * scale
            for hh in range(QKV_CHUNK // HEAD_DIM):
                hm_ref[c * (QKV_CHUNK // HEAD_DIM) + hh] = (
                    acc[:, hh * HEAD_DIM:(hh + 1) * HEAD_DIM].astype(BF16))

    for jj, f_ref in enumerate((kf_ref, kf_ref, vf_ref, vf_ref)):
        pl.when(j == jj)(functools.partial(project, None, f_ref, (jj % 2) * QKV_TN))

    pl.when(j >= 4)(functools.partial(project, BAND_Q_SCALE, None, 0))


def norm_qkv(x, g, w, *, tm, keep, tiles_per_keep):
    m, k = x.shape
    nb = 3 * D_MODEL // QKV_TN
    heads_per_block = QKV_TN // HEAD_DIM
    n_keep = m // (tm * tiles_per_keep) * keep

    def col(j):
        return (j + 2) % nb

    return pl.pallas_call(
        functools.partial(_norm_qkv_kernel, keep=keep),
        grid=(m // tm, nb),
        in_specs=[
            pl.BlockSpec((tm, k), lambda i, j: (i, 0)),
            pl.BlockSpec((1, k), lambda i, j: (0, 0)),
            pl.BlockSpec((k, QKV_TN), lambda i, j: (0, col(j))),
        ],
        out_specs=[
            pl.BlockSpec((heads_per_block, tm, HEAD_DIM), lambda i, j: (col(j), i, 0)),
            pl.BlockSpec((keep, D_MODEL), lambda i, j: (i // tiles_per_keep, 0)),
            pl.BlockSpec((keep, D_MODEL), lambda i, j: (i // tiles_per_keep, 0)),
        ],
        out_shape=[
            jax.ShapeDtypeStruct((3 * HEADS, m, HEAD_DIM), BF16),
            jax.ShapeDtypeStruct((n_keep, D_MODEL), F32),
            jax.ShapeDtypeStruct((n_keep, D_MODEL), F32),
        ],
        scratch_shapes=[pltpu.VMEM((tm, k), BF16)],
        compiler_params=_params("arbitrary", "arbitrary"),
        name="norm_qkv",
    )(x, g.reshape(1, k), w)


def _toeplitz_bias(f_row, rows):
    x = jnp.broadcast_to(f_row, (rows, BAND_F))
    return pltpu.roll(x, 0, 1, stride=1, stride_axis=0)


def _band_prompt_kernel(q_ref, k_ref, v_ref, f_ref, o_ref, kpad_ref, vpad_ref, bias_ref, s_ref, s2_ref,
                        p_ref, p2_ref):
    @pl.when((pl.program_id(0) == 0) & (pl.program_id(1) == 0))
    def _():
        kpad_ref[0:BAND_ROWS, :] = jnp.zeros((BAND_ROWS, HEAD_DIM), BF16)
        vpad_ref[0:BAND_ROWS, 0:HEAD_DIM] = jnp.zeros((BAND_ROWS, HEAD_DIM), BF16)
        vpad_ref[:, HEAD_DIM:] = jnp.ones((BAND_ROWS + SEQ, HEAD_DIM), BF16)

    @pl.when(pl.program_id(1) == 0)
    def _():
        r = lax.broadcasted_iota(jnp.int32, (BAND_TQ, BAND_TK), 0) // CHUNK
        j = lax.broadcasted_iota(jnp.int32, (BAND_TQ, BAND_TK), 1)
        jc = j // CHUNK
        allowed = (jc >= r) & (jc <= r + BAND_ROWS // CHUNK)
        base = jnp.where(allowed, _toeplitz_bias(f_ref[0], BAND_TQ)[:, :BAND_TK], NEG)
        bias_ref[2] = base
        bias_ref[1] = jnp.where(j >= BAND_ROWS - BAND_TQ, base, NEG)
        bias_ref[0] = jnp.where(j >= BAND_ROWS, base, NEG)

    kpad_ref[BAND_ROWS:, :] = k_ref[0]
    vpad_ref[BAND_ROWS:, 0:HEAD_DIM] = v_ref[0]

    def scores(qi, dst_ref):
        start = qi * BAND_TQ
        dst_ref[...] = lax.dot_general(q_ref[0, start:start + BAND_TQ, :],
                                       kpad_ref[start:start + BAND_TK, :], _NT,
                                       preferred_element_type=F32)

    def softmax(qi, cur_ref, p_dst):
        for u in range(BAND_TQ // BAND_SUB):
            rows = slice(u * BAND_SUB, (u + 1) * BAND_SUB)
            s = cur_ref[rows, :] + bias_ref[min(qi, 2), rows, :]
            p_dst[rows, :] = jnp.exp2(s - jnp.max(s, axis=-1, keepdims=True)).astype(BF16)

    def values(qi, p_src):
        start = qi * BAND_TQ
        pv = jnp.dot(p_src[...], vpad_ref[start:start + BAND_TK, :], preferred_element_type=F32)
        o_ref[start:start + BAND_TQ, :] = (pv[:, :HEAD_DIM] / pv[:, HEAD_DIM:]).astype(BF16)

    s_bufs = (s_ref, s2_ref)
    p_bufs = (p_ref, p2_ref)
    scores(0, s_bufs[0])
    scores(1, s_bufs[1])
    softmax(0, s_bufs[0], p_bufs[0])
    for t in range(BAND_NQ):
        if t + 2 < BAND_NQ:
            scores(t + 2, s_bufs[t % 2])
        if t + 1 < BAND_NQ:
            softmax(t + 1, s_bufs[(t + 1) % 2], p_bufs[(t + 1) % 2])
        values(t, p_bufs[t % 2])


def band_prompt(qkv_hm, f_rows):
    return pl.pallas_call(
        _band_prompt_kernel,
        grid=(HEADS, BATCH),
        in_specs=[
            pl.BlockSpec((1, SEQ, HEAD_DIM), lambda h, b: (h, b, 0)),
            pl.BlockSpec((1, SEQ, HEAD_DIM), lambda h, b: (HEADS + h, b, 0)),
            pl.BlockSpec((1, SEQ, HEAD_DIM), lambda h, b: (2 * HEADS + h, b, 0)),
            pl.BlockSpec((1, 1, BAND_F), lambda h, b: (h, 0, 0)),
        ],
        out_specs=pl.BlockSpec((SEQ, HEAD_DIM), lambda h, b: (b, h)),
        out_shape=jax.ShapeDtypeStruct((T_PROMPT, D_MODEL), BF16),
        scratch_shapes=[
            pltpu.VMEM((BAND_ROWS + SEQ, HEAD_DIM), BF16),
            pltpu.VMEM((BAND_ROWS + SEQ, 2 * HEAD_DIM), BF16),
            pltpu.VMEM((3, BAND_TQ, BAND_TK), F32),
            pltpu.VMEM((BAND_TQ, BAND_TK), F32),
            pltpu.VMEM((BAND_TQ, BAND_TK), F32),
            pltpu.VMEM((BAND_TQ, BAND_TK), BF16),
            pltpu.VMEM((BAND_TQ, BAND_TK), BF16),
        ],
        compiler_params=_params("arbitrary", "arbitrary"),
        name="band_prompt",
    )(qkv_hm, qkv_hm, qkv_hm, f_rows)


def _band_sample_kernel(q_ref, kn_ref, vn_ref, ck_ref, cv_ref, f_ref, o_ref):
    for h in range(HEADS):
        cols = slice(h * HEAD_DIM, (h + 1) * HEAD_DIM)
        bias = _toeplitz_bias(f_ref[h], DEC_SEQ)
        q = q_ref[h]
        kc = ck_ref[0, pl.ds(h, BAND_ROWS, stride=HEADS), :].astype(BF16)
        vc = cv_ref[0, pl.ds(h, BAND_ROWS, stride=HEADS), :].astype(BF16)
        s_c = lax.dot_general(q, kc, _NT, preferred_element_type=F32) + bias[:, :BAND_ROWS]
        s_n = (lax.dot_general(q, kn_ref[h], _NT, preferred_element_type=F32)
               + bias[:, BAND_ROWS:BAND_ROWS + DEC_SEQ])
        m = jnp.maximum(jnp.max(s_c, axis=-1, keepdims=True), jnp.max(s_n, axis=-1, keepdims=True))
        p_c = jnp.exp2(s_c - m)
        p_n = jnp.exp2(s_n - m)
        l = jnp.sum(p_c, axis=-1, keepdims=True) + jnp.sum(p_n, axis=-1, keepdims=True)
        o = (jnp.dot(p_c.astype(BF16), vc, preferred_element_type=F32)
             + jnp.dot(p_n.astype(BF16), vn_ref[h], preferred_element_type=F32)) / l
        o_ref[:, cols] = o.astype(BF16)


def band_sample(qkv_hm, cache_k, cache_v, f_rows):
    return pl.pallas_call(
        _band_sample_kernel,
        grid=(DEC_BATCH,),
        in_specs=[
            pl.BlockSpec((HEADS, DEC_SEQ, HEAD_DIM), lambda s: (0, s, 0)),
            pl.BlockSpec((HEADS, DEC_SEQ, HEAD_DIM), lambda s: (1, s, 0)),
            pl.BlockSpec((HEADS, DEC_SEQ, HEAD_DIM), lambda s: (2, s, 0)),
            pl.BlockSpec((1, BAND_ROWS * HEADS, HEAD_DIM), lambda s: (s, 0, 0)),
            pl.BlockSpec((1, BAND_ROWS * HEADS, HEAD_DIM), lambda s: (s, 0, 0)),
            pl.BlockSpec((HEADS, 1, BAND_F), lambda s: (0, 0, 0)),
        ],
        out_specs=pl.BlockSpec((DEC_SEQ, D_MODEL), lambda s: (s, 0)),
        out_shape=jax.ShapeDtypeStruct((T_SAMPLE, D_MODEL), BF16),
        compiler_params=_params("arbitrary"),
        name="band_sample",
    )(qkv_hm, qkv_hm, qkv_hm, cache_k, cache_v, f_rows)


def _proj_residual_kernel(a_ref, w_ref, g_ref, gn_ref, x_ref, o_ref, h_ref):
    w = w_ref[...]
    g = g_ref[...]
    gn = gn_ref[...]
    for u in range(a_ref.shape[0] // PROJ_SUB):
        rows = slice(u * PROJ_SUB, (u + 1) * PROJ_SUB)
        y = jnp.dot(a_ref[rows, :], w, preferred_element_type=F32)
        x1 = x_ref[rows, :] + _rms(y, g)
        o_ref[rows, :] = x1
        h_ref[rows, :] = _rms(x1, gn).astype(BF16)


def proj_residual(a, w, g, g_next, x, *, tm):
    m, k = a.shape
    n = w.shape[1]
    return pl.pallas_call(
        _proj_residual_kernel,
        grid=(m // tm,),
        in_specs=[
            pl.BlockSpec((tm, k), lambda i: (i, 0)),
            pl.BlockSpec((k, n), lambda i: (0, 0)),
            pl.BlockSpec((1, n), lambda i: (0, 0)),
            pl.BlockSpec((1, n), lambda i: (0, 0)),
            pl.BlockSpec((tm, n), lambda i: (i, 0)),
        ],
        out_specs=[
            pl.BlockSpec((tm, n), lambda i: (i, 0)),
            pl.BlockSpec((tm, n), lambda i: (i, 0)),
        ],
        out_shape=[
            jax.ShapeDtypeStruct((m, n), F32),
            jax.ShapeDtypeStruct((m, n), BF16),
        ],
        compiler_params=_params("arbitrary"),
        name="proj_residual",
    )(a, w, g.reshape(1, n), g_next.reshape(1, n), x)


def _ffn_kernel(x_ref, h_ref, w1_ref, w2_ref, g2_ref, o_ref, ssq_ref):
    j = pl.program_id(1)

    @pl.when(j == 0)
    def _():
        o_ref[...] = jnp.zeros(o_ref.shape, F32)

    a = jnp.maximum(jnp.dot(h_ref[...], w1_ref[...], preferred_element_type=F32), 0.0)
    a = (a * a).astype(BF16)
    for c in range(o_ref.shape[1] // FFN_CHUNK):
        cols = slice(c * FFN_CHUNK, (c + 1) * FFN_CHUNK)
        y = o_ref[:, cols] + jnp.dot(a, w2_ref[:, cols], preferred_element_type=F32)
        o_ref[:, cols] = y
        sq = y * y
        ssq_ref[:, c * 128:(c + 1) * 128] = functools.reduce(
            jnp.add, [sq[:, b * 128:(b + 1) * 128] for b in range(FFN_CHUNK // 128)])

    @pl.when(j == pl.num_programs(1) - 1)
    def _():
        ms = jnp.sum(ssq_ref[...], axis=-1, keepdims=True) * (1.0 / o_ref.shape[1])
        o_ref[...] = x_ref[...] + o_ref[...] * lax.rsqrt(ms + EPS) * g2_ref[...]


def ffn(x, h, w1, w2, g2, *, layer, tm, tf):
    m, d = x.shape
    f = w1.shape[2]
    return pl.pallas_call(
        _ffn_kernel,
        grid=(m // tm, f // tf),
        in_specs=[
            pl.BlockSpec((tm, d), lambda i, j: (i, 0)),
            pl.BlockSpec((tm, d), lambda i, j: (i, 0)),
            pl.BlockSpec((None, d, tf), lambda i, j: (layer, 0, j)),
            pl.BlockSpec((None, tf, d), lambda i, j: (layer, j, 0)),
            pl.BlockSpec((1, d), lambda i, j: (0, 0)),
        ],
        out_specs=pl.BlockSpec((tm, d), lambda i, j: (i, 0)),
        out_shape=jax.ShapeDtypeStruct((m, d), F32),
        scratch_shapes=[pltpu.VMEM((tm, 128 * (d // FFN_CHUNK)), F32)],
        compiler_params=_params("arbitrary", "arbitrary"),
        name="ffn",
    )(x, h, w1, w2, g2.reshape(1, d))


def _rope128(x, cos, sin):
    lane = lax.broadcasted_iota(jnp.int32, x.shape, 1)
    half = MLA_ROPE // 2
    swapped = jnp.where(lane < half, pltpu.roll(x, 128 - half, 1), pltpu.roll(x, half, 1))
    return x * cos + swapped * sin


def _mla_proj_kernel(x_ref, g_ref, wd_ref, qn_ref, kvn_ref, wq_ref, wuk_ref, wuv_ref, cos_ref, sin_ref,
                     ckv_ref, kr_ref, q_ref, *kv_refs):
    d = jnp.dot(_rms(x_ref[...], g_ref[...]).astype(BF16), wd_ref[...], preferred_element_type=F32)
    cos = cos_ref[...]
    sin = sin_ref[...]
    cq = _rms(d[:, :MLA_RANK], qn_ref[...]).astype(BF16)
    ckv = _rms(d[:, MLA_RANK:2 * MLA_RANK], kvn_ref[...])
    ckv_ref[...] = ckv
    kr = _rope128(d[:, 2 * MLA_RANK:], cos, sin)
    kr_ref[...] = kr[:, :MLA_ROPE]

    q = jnp.dot(cq, wq_ref[...], preferred_element_type=F32) * MLA_Q_SCALE
    for h in range(HEADS):
        base = h * MLA_QK
        q_ref[h, :, 0:MLA_NOPE] = q[:, base:base + MLA_NOPE].astype(BF16)
        q_ref[h, :, MLA_NOPE:] = _rope128(q[:, base + MLA_NOPE:base + MLA_QK], cos, sin).astype(BF16)

    if kv_refs:
        k_ref, v_ref = kv_refs
        kr_b = kr.astype(BF16)
        ckv_b = ckv.astype(BF16)
        kn = jnp.dot(ckv_b, wuk_ref[...], preferred_element_type=F32)
        vv = jnp.dot(ckv_b, wuv_ref[...], preferred_element_type=F32)
        ones = jnp.ones((d.shape[0], HEAD_DIM), BF16)
        for h in range(HEADS):
            k_ref[h, :, 0:MLA_NOPE] = kn[:, h * MLA_NOPE:(h + 1) * MLA_NOPE].astype(BF16)
            k_ref[h, :, MLA_NOPE:] = kr_b
            v_ref[h, :, 0:HEAD_DIM] = vv[:, h * HEAD_DIM:(h + 1) * HEAD_DIM].astype(BF16)
            v_ref[h, :, HEAD_DIM:] = ones


def mla_proj(x, g, w_d, q_norm, kv_norm, wq, wuk, wuv, cos, sin, *, tm, with_kv):
    m = x.shape[0]
    full = lambda i: (0, 0)
    rope_tiles = cos.shape[0] // tm
    hm_spec = pl.BlockSpec((HEADS, tm, MLA_QK), lambda i: (0, i, 0))
    hm_shape = jax.ShapeDtypeStruct((HEADS, m, MLA_QK), BF16)
    n_hm = 3 if with_kv else 1
    return pl.pallas_call(
        _mla_proj_kernel,
        grid=(m // tm,),
        in_specs=[
            pl.BlockSpec((tm, D_MODEL), lambda i: (i, 0)),
            pl.BlockSpec((1, D_MODEL), full),
            pl.BlockSpec(w_d.shape, full),
            pl.BlockSpec((1, MLA_RANK), full),
            pl.BlockSpec((1, MLA_RANK), full),
            pl.BlockSpec(wq.shape, full),
            pl.BlockSpec(wuk.shape, full),
            pl.BlockSpec(wuv.shape, full),
            pl.BlockSpec((tm, 128), lambda i: (i % rope_tiles, 0)),
            pl.BlockSpec((tm, 128), lambda i: (i % rope_tiles, 0)),
        ],
        out_specs=[
            pl.BlockSpec((tm, MLA_RANK), lambda i: (i, 0)),
            pl.BlockSpec((tm, MLA_ROPE), lambda i: (i, 0)),
        ] + [hm_spec] * n_hm,
        out_shape=[
            jax.ShapeDtypeStruct((m, MLA_RANK), F32),
            jax.ShapeDtypeStruct((m, MLA_ROPE), F32),
        ] + [hm_shape] * n_hm,
        compiler_params=_params("arbitrary"),
        name="mla_proj",
    )(x, g.reshape(1, -1), w_d, q_norm.reshape(1, -1), kv_norm.reshape(1, -1), wq, wuk, wuv, cos, sin)


def _mla_prompt_kernel(q_ref, k_ref, v_ref, o_ref, s_ref, s2_ref, p_ref, p2_ref, alpha_ref, alpha2_ref,
                       bias_ref, m_ref, acc_ref):
    @pl.when((pl.program_id(0) == 0) & (pl.program_id(1) == 0))
    def _():
        rc = lax.broadcasted_iota(jnp.int32, (MLA_TQ, MLA_TK), 0) // CHUNK
        cc = lax.broadcasted_iota(jnp.int32, (MLA_TQ, MLA_TK), 1) // CHUNK
        bias_ref[...] = jnp.where(cc <= rc, 0.0, NEG)

    def is_last(qi, k0, klen):
        return k0 + klen == (qi + 1) * MLA_TQ

    def width(qi, k0, klen, row0):
        return klen - MLA_HALF if is_last(qi, k0, klen) and row0 < MLA_HALF else klen

    def scores(qi, k0, klen, dst_ref):
        for row0 in range(0, MLA_TQ, MLA_HALF):
            w = width(qi, k0, klen, row0)
            q = q_ref[0, qi * MLA_TQ + row0:qi * MLA_TQ + row0 + MLA_HALF, :]
            dst_ref[row0:row0 + MLA_HALF, :w] = lax.dot_general(
                q, k_ref[0, k0:k0 + w, :], _NT, preferred_element_type=F32)

    def softmax(qi, k0, klen, cur_ref, p_dst, alpha_dst):
        first = k0 == 0
        last = is_last(qi, k0, klen)
        for u in range(MLA_TQ // MLA_SUB):
            rows = slice(u * MLA_SUB, (u + 1) * MLA_SUB)
            if last:
                w = width(qi, k0, klen, u * MLA_SUB)
                d0 = klen - MLA_TK
                parts = [cur_ref[rows, d0:w] + bias_ref[rows, :w - d0]]
                if d0 > 0:
                    parts.insert(0, cur_ref[rows, :d0])
            else:
                parts = [cur_ref[rows, :klen]]
            m_cur = functools.reduce(jnp.maximum, [jnp.max(x, axis=-1, keepdims=True) for x in parts])
            if first:
                m_new = jnp.broadcast_to(m_cur, (MLA_SUB, 128))
            else:
                m_prev = m_ref[rows, :]
                m_new = jnp.maximum(m_prev, m_cur)
                alpha_dst[rows, :] = jnp.exp2(m_prev - m_new)
            if not last:
                m_ref[rows, :] = m_new
            col = 0
            for x in parts:
                p_dst[rows, col:col + x.shape[1]] = jnp.exp2(
                    x - jnp.tile(m_new, (1, x.shape[1] // 128))).astype(BF16)
                col += x.shape[1]

    def values(qi, k0, klen, p_src, alpha_src):
        first = k0 == 0
        last = is_last(qi, k0, klen)
        for u in range(MLA_TQ // MLA_HALF):
            rows = slice(u * MLA_HALF, (u + 1) * MLA_HALF)
            w = width(qi, k0, klen, u * MLA_HALF)
            acc = jnp.dot(p_src[rows, :w], v_ref[0, k0:k0 + w, :], preferred_element_type=F32)
            if not first:
                acc = acc + jnp.tile(alpha_src[rows, :], (1, 2)) * acc_ref[rows, :]
            if last:
                o_ref[qi * MLA_TQ + u * MLA_HALF:qi * MLA_TQ + (u + 1) * MLA_HALF, :] = (
                    acc[:, :HEAD_DIM] / acc[:, HEAD_DIM:]).astype(BF16)
            else:
                acc_ref[rows, :] = acc

    blocks = [(qi, k0, min(MLA_TK_MAX, (qi + 1) * MLA_TQ - k0))
              for qi in range(MLA_NQ) for k0 in range(0, (qi + 1) * MLA_TQ, MLA_TK_MAX)]
    n = len(blocks)
    s_bufs = (s_ref, s2_ref)
    p_bufs = (p_ref, p2_ref)
    a_bufs = (alpha_ref, alpha2_ref)
    scores(*blocks[0], s_bufs[0])
    scores(*blocks[1], s_bufs[1])
    softmax(*blocks[0], s_bufs[0], p_bufs[0], a_bufs[0])
    for t in range(n):
        if t + 2 < n:
            scores(*blocks[t + 2], s_bufs[t % 2])
        if t + 1 < n:
            softmax(*blocks[t + 1], s_bufs[(t + 1) % 2], p_bufs[(t + 1) % 2], a_bufs[(t + 1) % 2])
        values(*blocks[t], p_bufs[t % 2], a_bufs[t % 2])


def mla_prompt(q, k, v):
    return pl.pallas_call(
        _mla_prompt_kernel,
        grid=(BATCH, HEADS),
        in_specs=[
            pl.BlockSpec((1, SEQ, MLA_QK), lambda b, h: (h, b, 0)),
            pl.BlockSpec((1, SEQ, MLA_QK), lambda b, h: (h, b, 0)),
            pl.BlockSpec((1, SEQ, 2 * HEAD_DIM), lambda b, h: (h, b, 0)),
        ],
        out_specs=pl.BlockSpec((SEQ, HEAD_DIM), lambda b, h: (b, h)),
        out_shape=jax.ShapeDtypeStruct((T_PROMPT, D_MODEL), BF16),
        scratch_shapes=[
            pltpu.VMEM((MLA_TQ, MLA_TK_MAX), F32),
            pltpu.VMEM((MLA_TQ, MLA_TK_MAX), F32),
            pltpu.VMEM((MLA_TQ, MLA_TK_MAX), BF16),
            pltpu.VMEM((MLA_TQ, MLA_TK_MAX), BF16),
            pltpu.VMEM((MLA_TQ, 128), F32),
            pltpu.VMEM((MLA_TQ, 128), F32),
            pltpu.VMEM((MLA_TQ, MLA_TK), F32),
            pltpu.VMEM((MLA_TQ, 128), F32),
            pltpu.VMEM((MLA_TQ, 2 * HEAD_DIM), F32),
        ],
        compiler_params=_params("arbitrary", "arbitrary"),
        name="mla_prompt",
    )(q, k, v)


def _mla_sample_kernel(q_ref, cn_ref, rn_ref, cc_ref, cr_ref, wuk_ref, wuv_ref, o_ref,
                       ql_ref, qr_ref):
    cc = cc_ref[0].astype(BF16)
    cr = cr_ref[0].astype(BF16)
    cn = cn_ref[...].astype(BF16)
    rn = rn_ref[...].astype(BF16)
    for g in range(HEADS // MLA_SAMPLE_HEADS):
        heads = range(g * MLA_SAMPLE_HEADS, (g + 1) * MLA_SAMPLE_HEADS)
        grp = slice(g * MLA_SAMPLE_HEADS * DEC_SEQ, (g + 1) * MLA_SAMPLE_HEADS * DEC_SEQ)
        for h in heads:
            rows = slice(h * DEC_SEQ, (h + 1) * DEC_SEQ)
            ql_ref[rows, :] = lax.dot_general(
                q_ref[h, :, 0:MLA_NOPE], wuk_ref[:, h * MLA_NOPE:(h + 1) * MLA_NOPE], _NT,
                preferred_element_type=F32).astype(BF16)
            qr_ref[rows, :] = q_ref[h, :, MLA_NOPE:]
        ql = ql_ref[grp, :]
        qr = qr_ref[grp, :][:, :MLA_ROPE]
        s_c = (lax.dot_general(ql, cc, _NT, preferred_element_type=F32)
               + lax.dot_general(qr, cr, _NT, preferred_element_type=F32))
        s_n = (lax.dot_general(ql, cn, _NT, preferred_element_type=F32)
               + lax.dot_general(qr, rn, _NT, preferred_element_type=F32))
        m = jnp.maximum(jnp.max(s_c, axis=-1, keepdims=True), jnp.max(s_n, axis=-1, keepdims=True))
        p_c = jnp.exp2(s_c - m)
        p_n = jnp.exp2(s_n - m)
        l = jnp.sum(p_c, axis=-1, keepdims=True) + jnp.sum(p_n, axis=-1, keepdims=True)
        o_lat = ((jnp.dot(p_c.astype(BF16), cc, preferred_element_type=F32)
                  + jnp.dot(p_n.astype(BF16), cn, preferred_element_type=F32)) / l).astype(BF16)
        for i, h in enumerate(heads):
            cols = slice(h * HEAD_DIM, (h + 1) * HEAD_DIM)
            o_ref[:, cols] = jnp.dot(o_lat[i * DEC_SEQ:(i + 1) * DEC_SEQ], wuv_ref[:, cols],
                                     preferred_element_type=F32).astype(BF16)


def mla_sample(q, ckv, kr, cache_ckv, cache_kr, wuk, wuv):
    full = lambda s: (0, 0)
    return pl.pallas_call(
        _mla_sample_kernel,
        grid=(DEC_BATCH,),
        in_specs=[
            pl.BlockSpec((HEADS, DEC_SEQ, MLA_QK), lambda s: (0, s, 0)),
            pl.BlockSpec((DEC_SEQ, MLA_RANK), lambda s: (s, 0)),
            pl.BlockSpec((DEC_SEQ, MLA_ROPE), lambda s: (s, 0)),
            pl.BlockSpec((1, PAST_LEN, MLA_RANK), lambda s: (s, 0, 0)),
            pl.BlockSpec((1, PAST_LEN, MLA_ROPE), lambda s: (s, 0, 0)),
            pl.BlockSpec(wuk.shape, full),
            pl.BlockSpec(wuv.shape, full),
        ],
        out_specs=pl.BlockSpec((DEC_SEQ, D_MODEL), lambda s: (s, 0)),
        out_shape=jax.ShapeDtypeStruct((T_SAMPLE, D_MODEL), BF16),
        scratch_shapes=[
            pltpu.VMEM((HEADS * DEC_SEQ, MLA_RANK), BF16),
            pltpu.VMEM((HEADS * DEC_SEQ, 128), BF16),
        ],
        compiler_params=_params("arbitrary"),
        name="mla_sample",
    )(q, ckv, kr, cache_ckv, cache_kr, wuk, wuv)


def _bias_rows(rel_bias):
    far = jnp.broadcast_to(rel_bias[:, 2 * REL_CLIP:], (HEADS, BAND_F // 4))
    mid = rel_bias[:, 1:][:, ::-1]
    return (jnp.concatenate([far, mid, far], axis=1) * LOG2E).reshape(HEADS, 1, BAND_F)


def _rope_tables(pos):
    half = MLA_ROPE // 2
    inv = 1.0 / (ROPE_THETA ** (jnp.arange(half, dtype=F32) / half))
    ang = pos.astype(F32)[:, None] * inv[None, :]
    c, s = jnp.cos(ang), jnp.sin(ang)
    z = jnp.zeros((pos.shape[0], 128 - MLA_ROPE), F32)
    return jnp.concatenate([c, c, z], axis=1), jnp.concatenate([-s, s, z], axis=1)


PROMPT_TM_QKV = 1024
MLA_PROJ_TM = 256


def kernel(x_prompt, x_sample, cache_a_k, cache_a_v, cache_mla_ckv, cache_mla_kr, ln_mix_pre, ln_mix_post, ln_ffn_pre, ln_ffn_post, a_w_qkv, a_w_o, a_rel_bias, mla_w_dq, mla_q_norm, mla_w_uq, mla_w_dkv, mla_kv_norm, mla_w_uk, mla_w_uv, mla_w_o, ffn_w1, ffn_w2):
    xp = x_prompt.reshape(T_PROMPT, D_MODEL)
    xs = x_sample.reshape(T_SAMPLE, D_MODEL)
    w1 = ffn_w1.astype(BF16)
    w2 = ffn_w2.astype(BF16)

    def mix_out_and_ffn(i, o, w_o, x):
        x, h = proj_residual(o, w_o, ln_mix_post[i], ln_ffn_pre[i], x, tm=512)
        return ffn(x, h, w1, w2, ln_ffn_post[i], layer=i, tm=512, tf=1024)

    w_qkv = a_w_qkv[0].astype(BF16)
    w_o = a_w_o[0].astype(BF16)
    f_rows = _bias_rows(a_rel_bias[0])
    hm_p, k_p, v_p = norm_qkv(xp, ln_mix_pre[0], w_qkv, tm=PROMPT_TM_QKV, keep=BAND_ROWS,
                              tiles_per_keep=SEQ // PROMPT_TM_QKV)
    hm_s, k_s, v_s = norm_qkv(xs, ln_mix_pre[0], w_qkv, tm=512, keep=512, tiles_per_keep=1)
    o_p = band_prompt(hm_p, f_rows)
    o_s = band_sample(hm_s,
                      cache_a_k[0].reshape(DEC_BATCH, BAND_ROWS * HEADS, HEAD_DIM),
                      cache_a_v[0].reshape(DEC_BATCH, BAND_ROWS * HEADS, HEAD_DIM), f_rows)
    xp = mix_out_and_ffn(0, o_p, w_o, xp)
    xs = mix_out_and_ffn(0, o_s, w_o, xs)

    pad = jnp.zeros((D_MODEL, 128 - MLA_ROPE), F32)
    w_d = jnp.concatenate([mla_w_dq[0], mla_w_dkv[0], pad], axis=1).astype(BF16)
    wq = jnp.pad(mla_w_uq[0], ((0, 0), (0, 0), (0, MLA_QK - MLA_NOPE - MLA_ROPE)))
    wq = wq.reshape(MLA_RANK, HEADS * MLA_QK).astype(BF16)
    wuk = mla_w_uk[0].reshape(MLA_RANK, HEADS * MLA_NOPE).astype(BF16)
    wuv = mla_w_uv[0].reshape(MLA_RANK, HEADS * HEAD_DIM).astype(BF16)
    w_o = mla_w_o[0].astype(BF16)
    cos_p, sin_p = _rope_tables(jnp.arange(SEQ))
    cos_s, sin_s = _rope_tables(jnp.tile(PAST_LEN + jnp.arange(DEC_SEQ), MLA_PROJ_TM // DEC_SEQ))

    ckv_p, kr_p, q_p, kk_p, vv_p = mla_proj(xp, ln_mix_pre[1], w_d, mla_q_norm[0], mla_kv_norm[0],
                                            wq, wuk, wuv, cos_p, sin_p, tm=MLA_PROJ_TM, with_kv=True)
    ckv_s, kr_s, q_s = mla_proj(xs, ln_mix_pre[1], w_d, mla_q_norm[0], mla_kv_norm[0],
                                wq, wuk, wuv, cos_s, sin_s, tm=MLA_PROJ_TM, with_kv=False)
    o_p = mla_prompt(q_p, kk_p, vv_p)
    o_s = mla_sample(q_s, ckv_s, kr_s, cache_mla_ckv[0], cache_mla_kr[0], wuk, wuv)
    xp = mix_out_and_ffn(1, o_p, w_o, xp)
    xs = mix_out_and_ffn(1, o_s, w_o, xs)

    return (
        xp.reshape(BATCH, SEQ, D_MODEL),
        xs.reshape(DEC_BATCH, DEC_SEQ, D_MODEL),
        k_p.reshape(1, BATCH, BAND_ROWS, HEADS, HEAD_DIM),
        v_p.reshape(1, BATCH, BAND_ROWS, HEADS, HEAD_DIM),
        k_s.reshape(1, DEC_BATCH, DEC_SEQ, HEADS, HEAD_DIM),
        v_s.reshape(1, DEC_BATCH, DEC_SEQ, HEADS, HEAD_DIM),
        ckv_p.reshape(1, BATCH, SEQ, MLA_RANK),
        kr_p.reshape(1, BATCH, SEQ, MLA_ROPE),
        ckv_s.reshape(1, DEC_BATCH, DEC_SEQ, MLA_RANK),
        kr_s.reshape(1, DEC_BATCH, DEC_SEQ, MLA_ROPE),
    )
```

```python
import functools
import math

import jax
import jax.numpy as jnp
from jax import lax
from jax.experimental import pallas as pl
from jax.experimental.pallas import tpu as pltpu

F32 = jnp.float32
BF16 = jnp.bfloat16

D_MODEL = 2048
BATCH = 4
SEQ = 4096
DEC_BATCH = 16
DEC_SEQ = 64
PAST_LEN = 2048
CHUNK = 64
HEADS = 16
HEAD_DIM = 128
BAND_ROWS = 512
REL_CLIP = 256
MLA_RANK = 512
MLA_NOPE = 128
MLA_ROPE = 64
MLA_QK = 256
ROPE_THETA = 10000.0
D_FF = 8192
EPS = 1e-6
NEG = -1e30
LOG2E = math.log2(math.e)

T_PROMPT = BATCH * SEQ
T_SAMPLE = DEC_BATCH * DEC_SEQ

VMEM_LIMIT_BYTES = 58 * 1024 * 1024
FFN_VMEM_LIMIT_BYTES = 62 * 1024 * 1024

BAND_TQ = 256
BAND_TK = BAND_ROWS + BAND_TQ
BAND_F = 1024
BAND_NQ = SEQ // BAND_TQ
BAND_SUB = 32
BAND_Q_SCALE = HEAD_DIM ** -0.5 * LOG2E

MLA_TQ = 512
MLA_TK = 512
MLA_TK_MAX = 1024
MLA_NQ = SEQ // MLA_TQ
MLA_SUB = 32
MLA_HALF = MLA_TQ // 2
MLA_Q_SCALE = (MLA_NOPE + MLA_ROPE) ** -0.5 * LOG2E

MLA_SAMPLE_HEADS = 16
QKV_TN = 1024
QKV_CHUNK = 256
QKV_ROWS = 256
PROJ_SUB = 128
FFN_CHUNK = 512

_NT = (((1,), (1,)), ((), ()))


def _params(*sem):
    return pltpu.CompilerParams(dimension_semantics=sem, vmem_limit_bytes=VMEM_LIMIT_BYTES)


def _rms(x, g):
    ms = jnp.mean(x * x, axis=-1, keepdims=True)
    return x * lax.rsqrt(ms + EPS) * g


def _norm_qkv_kernel(x_ref, g_ref, w_ref, hm_ref, kf_ref, vf_ref, h_ref, *, keep):
    j = pl.program_id(1)
    tm = x_ref.shape[0]
    heads_per_chunk = QKV_CHUNK // HEAD_DIM

    def project_first():
        g = g_ref[...]
        for u in range(tm // QKV_ROWS):
            rows = slice(u * QKV_ROWS, (u + 1) * QKV_ROWS)
            hu = _rms(x_ref[rows, :], g).astype(BF16)
            h_ref[rows, :] = hu
            kept = u * QKV_ROWS - (tm - keep)
            for c in range(QKV_TN // QKV_CHUNK):
                cols = slice(c * QKV_CHUNK, (c + 1) * QKV_CHUNK)
                acc = jnp.dot(hu, w_ref[:, cols], preferred_element_type=F32)
                if kept >= 0:
                    kf_ref[kept:kept + QKV_ROWS, cols] = acc
                for hh in range(heads_per_chunk):
                    hm_ref[c * heads_per_chunk + hh, rows, :] = (
                        acc[:, hh * HEAD_DIM:(hh + 1) * HEAD_DIM].astype(BF16))

    def project(scale, f_ref, f_col):
        h = h_ref[...]
        for c in range(QKV_TN // QKV_CHUNK):
            cols = slice(c * QKV_CHUNK, (c + 1) * QKV_CHUNK)
            acc = jnp.dot(h, w_ref[:, cols], preferred_element_type=F32)
            if f_ref is not None:
                f_ref[:, f_col + c * QKV_CHUNK:f_col + (c + 1) * QKV_CHUNK] = acc[tm - keep:, :]
            if scale is not None:
                acc = acc * scale
            for hh in range(heads_per_chunk):
                hm_ref[c * heads_per_chunk + hh] = (
                    acc[:, hh * HEAD_DIM:(hh + 1) * HEAD_DIM].astype(BF16))

    pl.when(j == 0)(project_first)
    for jj, f_ref in ((1, kf_ref), (2, vf_ref), (3, vf_ref)):
        pl.when(j == jj)(functools.partial(project, None, f_ref, (jj % 2) * QKV_TN))

    pl.when(j >= 4)(functools.partial(project, BAND_Q_SCALE, None, 0))


def norm_qkv(x, g, w, *, tm, keep, tiles_per_keep):
    m, k = x.shape
    nb = 3 * D_MODEL // QKV_TN
    heads_per_block = QKV_TN // HEAD_DIM
    n_keep = m // (tm * tiles_per_keep) * keep

    def col(j):
        return (j + 2) % nb

    return pl.pallas_call(
        functools.partial(_norm_qkv_kernel, keep=keep),
        grid=(m // tm, nb),
        in_specs=[
            pl.BlockSpec((tm, k), lambda i, j: (i, 0)),
            pl.BlockSpec((1, k), lambda i, j: (0, 0)),
            pl.BlockSpec((k, QKV_TN), lambda i, j: (0, col(j))),
        ],
        out_specs=[
            pl.BlockSpec((heads_per_block, tm, HEAD_DIM), lambda i, j: (col(j), i, 0)),
            pl.BlockSpec((keep, D_MODEL), lambda i, j: (i // tiles_per_keep, 0)),
            pl.BlockSpec((keep, D_MODEL), lambda i, j: (i // tiles_per_keep, 0)),
        ],
        out_shape=[
            jax.ShapeDtypeStruct((3 * HEADS, m, HEAD_DIM), BF16),
            jax.ShapeDtypeStruct((n_keep, D_MODEL), F32),
            jax.ShapeDtypeStruct((n_keep, D_MODEL), F32),
        ],
        scratch_shapes=[pltpu.VMEM((tm, k), BF16)],
        compiler_params=_params("arbitrary", "arbitrary"),
        name="norm_qkv",
    )(x, g.reshape(1, k), w)


def _toeplitz_bias(f_row, rows):
    x = jnp.broadcast_to(f_row, (rows, BAND_F))
    return pltpu.roll(x, 0, 1, stride=1, stride_axis=0)


def _band_prompt_kernel(q_ref, k_ref, v_ref, f_ref, o_ref, kpad_ref, vpad_ref, bias_ref, s_ref, s2_ref,
                        p_ref, p2_ref):
    @pl.when((pl.program_id(0) == 0) & (pl.program_id(1) == 0))
    def _():
        kpad_ref[0:BAND_ROWS, :] = jnp.zeros((BAND_ROWS, HEAD_DIM), BF16)
        vpad_ref[0:BAND_ROWS, 0:HEAD_DIM] = jnp.zeros((BAND_ROWS, HEAD_DIM), BF16)
        vpad_ref[:, HEAD_DIM:] = jnp.ones((BAND_ROWS + SEQ, HEAD_DIM), BF16)

    @pl.when(pl.program_id(1) == 0)
    def _():
        r = lax.broadcasted_iota(jnp.int32, (BAND_TQ, BAND_TK), 0) // CHUNK
        j = lax.broadcasted_iota(jnp.int32, (BAND_TQ, BAND_TK), 1)
        jc = j // CHUNK
        allowed = (jc >= r) & (jc <= r + BAND_ROWS // CHUNK)
        base = jnp.where(allowed, _toeplitz_bias(f_ref[0], BAND_TQ)[:, :BAND_TK], NEG)
        bias_ref[2] = base
        bias_ref[1] = jnp.where(j >= BAND_ROWS - BAND_TQ, base, NEG)
        bias_ref[0] = jnp.where(j >= BAND_ROWS, base, NEG)

    kpad_ref[BAND_ROWS:, :] = k_ref[0]
    vpad_ref[BAND_ROWS:, 0:HEAD_DIM] = v_ref[0]

    def scores(qi, dst_ref):
        start = qi * BAND_TQ
        dst_ref[...] = lax.dot_general(q_ref[0, start:start + BAND_TQ, :],
                                       kpad_ref[start:start + BAND_TK, :], _NT,
                                       preferred_element_type=F32)

    def softmax(qi, cur_ref, p_dst):
        for u in range(BAND_TQ // BAND_SUB):
            rows = slice(u * BAND_SUB, (u + 1) * BAND_SUB)
            s = cur_ref[rows, :] + bias_ref[min(qi, 2), rows, :]
            p_dst[rows, :] = jnp.exp2(s - jnp.max(s, axis=-1, keepdims=True)).astype(BF16)

    def values(qi, p_src):
        start = qi * BAND_TQ
        pv = jnp.dot(p_src[...], vpad_ref[start:start + BAND_TK, :], preferred_element_type=F32)
        o_ref[start:start + BAND_TQ, :] = (pv[:, :HEAD_DIM] / pv[:, HEAD_DIM:]).astype(BF16)

    s_bufs = (s_ref, s2_ref)
    p_bufs = (p_ref, p2_ref)
    scores(0, s_bufs[0])
    scores(1, s_bufs[1])
    softmax(0, s_bufs[0], p_bufs[0])
    for t in range(BAND_NQ):
        if t + 2 < BAND_NQ:
            scores(t + 2, s_bufs[t % 2])
        if t + 1 < BAND_NQ:
            softmax(t + 1, s_bufs[(t + 1) % 2], p_bufs[(t + 1) % 2])
        values(t, p_bufs[t % 2])


def band_prompt(qkv_hm, f_rows):
    return pl.pallas_call(
        _band_prompt_kernel,
        grid=(HEADS, BATCH),
        in_specs=[
            pl.BlockSpec((1, SEQ, HEAD_DIM), lambda h, b: (h, b, 0)),
            pl.BlockSpec((1, SEQ, HEAD_DIM), lambda h, b: (HEADS + h, b, 0)),
            pl.BlockSpec((1, SEQ, HEAD_DIM), lambda h, b: (2 * HEADS + h, b, 0)),
            pl.BlockSpec((1, 1, BAND_F), lambda h, b: (h, 0, 0)),
        ],
        out_specs=pl.BlockSpec((SEQ, HEAD_DIM), lambda h, b: (b, h)),
        out_shape=jax.ShapeDtypeStruct((T_PROMPT, D_MODEL), BF16),
        scratch_shapes=[
            pltpu.VMEM((BAND_ROWS + SEQ, HEAD_DIM), BF16),
            pltpu.VMEM((BAND_ROWS + SEQ, 2 * HEAD_DIM), BF16),
            pltpu.VMEM((3, BAND_TQ, BAND_TK), F32),
            pltpu.VMEM((BAND_TQ, BAND_TK), F32),
            pltpu.VMEM((BAND_TQ, BAND_TK), F32),
            pltpu.VMEM((BAND_TQ, BAND_TK), BF16),
            pltpu.VMEM((BAND_TQ, BAND_TK), BF16),
        ],
        compiler_params=_params("arbitrary", "arbitrary"),
        name="band_prompt",
    )(qkv_hm, qkv_hm, qkv_hm, f_rows)


def _band_sample_kernel(q_ref, kn_ref, vn_ref, ck_ref, cv_ref, f_ref, o_ref):
    for h in range(HEADS):
        cols = slice(h * HEAD_DIM, (h + 1) * HEAD_DIM)
        bias = _toeplitz_bias(f_ref[h], DEC_SEQ)
        q = q_ref[h]
        kc = ck_ref[0, pl.ds(h, BAND_ROWS, stride=HEADS), :].astype(BF16)
        vc = cv_ref[0, pl.ds(h, BAND_ROWS, stride=HEADS), :].astype(BF16)
        s_c = lax.dot_general(q, kc, _NT, preferred_element_type=F32) + bias[:, :BAND_ROWS]
        s_n = (lax.dot_general(q, kn_ref[h], _NT, preferred_element_type=F32)
               + bias[:, BAND_ROWS:BAND_ROWS + DEC_SEQ])
        m = jnp.maximum(jnp.max(s_c, axis=-1, keepdims=True), jnp.max(s_n, axis=-1, keepdims=True))
        p_c = jnp.exp2(s_c - m)
        p_n = jnp.exp2(s_n - m)
        l = jnp.sum(p_c, axis=-1, keepdims=True) + jnp.sum(p_n, axis=-1, keepdims=True)
        o = (jnp.dot(p_c.astype(BF16), vc, preferred_element_type=F32)
             + jnp.dot(p_n.astype(BF16), vn_ref[h], preferred_element_type=F32)) / l
        o_ref[:, cols] = o.astype(BF16)


def band_sample(qkv_hm, cache_k, cache_v, f_rows):
    return pl.pallas_call(
        _band_sample_kernel,
        grid=(DEC_BATCH,),
        in_specs=[
            pl.BlockSpec((HEADS, DEC_SEQ, HEAD_DIM), lambda s: (0, s, 0)),
            pl.BlockSpec((HEADS, DEC_SEQ, HEAD_DIM), lambda s: (1, s, 0)),
            pl.BlockSpec((HEADS, DEC_SEQ, HEAD_DIM), lambda s: (2, s, 0)),
            pl.BlockSpec((1, BAND_ROWS * HEADS, HEAD_DIM), lambda s: (s, 0, 0)),
            pl.BlockSpec((1, BAND_ROWS * HEADS, HEAD_DIM), lambda s: (s, 0, 0)),
            pl.BlockSpec((HEADS, 1, BAND_F), lambda s: (0, 0, 0)),
        ],
        out_specs=pl.BlockSpec((DEC_SEQ, D_MODEL), lambda s: (s, 0)),
        out_shape=jax.ShapeDtypeStruct((T_SAMPLE, D_MODEL), BF16),
        compiler_params=_params("arbitrary"),
        name="band_sample",
    )(qkv_hm, qkv_hm, qkv_hm, cache_k, cache_v, f_rows)


def _proj_residual_kernel(a_ref, w_ref, g_ref, gn_ref, x_ref, o_ref, h_ref):
    w = w_ref[...]
    g = g_ref[...]
    gn = gn_ref[...]
    for u in range(a_ref.shape[0] // PROJ_SUB):
        rows = slice(u * PROJ_SUB, (u + 1) * PROJ_SUB)
        y = jnp.dot(a_ref[rows, :], w, preferred_element_type=F32)
        x1 = x_ref[rows, :] + _rms(y, g)
        o_ref[rows, :] = x1
        h_ref[rows, :] = _rms(x1, gn).astype(BF16)


def proj_residual(a, w, g, g_next, x, *, tm):
    m, k = a.shape
    n = w.shape[1]
    return pl.pallas_call(
        _proj_residual_kernel,
        grid=(m // tm,),
        in_specs=[
            pl.BlockSpec((tm, k), lambda i: (i, 0)),
            pl.BlockSpec((k, n), lambda i: (0, 0)),
            pl.BlockSpec((1, n), lambda i: (0, 0)),
            pl.BlockSpec((1, n), lambda i: (0, 0)),
            pl.BlockSpec((tm, n), lambda i: (i, 0)),
        ],
        out_specs=[
            pl.BlockSpec((tm, n), lambda i: (i, 0)),
            pl.BlockSpec((tm, n), lambda i: (i, 0)),
        ],
        out_shape=[
            jax.ShapeDtypeStruct((m, n), F32),
            jax.ShapeDtypeStruct((m, n), BF16),
        ],
        compiler_params=_params("arbitrary"),
        name="proj_residual",
    )(a, w, g.reshape(1, n), g_next.reshape(1, n), x)


def _ffn_kernel(x_ref, h_ref, w1a_ref, w1b_ref, w2a_ref, w2b_ref, g2_ref, o_ref, ssq_ref):
    j = pl.program_id(1)

    @pl.when(j == 0)
    def _():
        o_ref[...] = jnp.zeros(o_ref.shape, F32)

    def hidden(w1_ref):
        a = jnp.maximum(jnp.dot(h_ref[...], w1_ref[...], preferred_element_type=F32), 0.0)
        return (a * a).astype(BF16)

    aa = hidden(w1a_ref)
    ab = hidden(w1b_ref)
    for c in range(o_ref.shape[1] // FFN_CHUNK):
        cols = slice(c * FFN_CHUNK, (c + 1) * FFN_CHUNK)
        y = (o_ref[:, cols] + jnp.dot(aa, w2a_ref[:, cols], preferred_element_type=F32)
             + jnp.dot(ab, w2b_ref[:, cols], preferred_element_type=F32))
        o_ref[:, cols] = y
        sq = y * y
        ssq_ref[:, c * 128:(c + 1) * 128] = functools.reduce(
            jnp.add, [sq[:, b * 128:(b + 1) * 128] for b in range(FFN_CHUNK // 128)])

    @pl.when(j == pl.num_programs(1) - 1)
    def _():
        ms = jnp.sum(ssq_ref[...], axis=-1, keepdims=True) * (1.0 / o_ref.shape[1])
        o_ref[...] = x_ref[...] + o_ref[...] * lax.rsqrt(ms + EPS) * g2_ref[...]


def ffn(x, h, w1, w2, g2, *, layer, tm, tf):
    m, d = x.shape
    f = w1.shape[2]
    return pl.pallas_call(
        _ffn_kernel,
        grid=(m // tm, f // (2 * tf)),
        in_specs=[
            pl.BlockSpec((tm, d), lambda i, j: (i, 0)),
            pl.BlockSpec((tm, d), lambda i, j: (i, 0)),
            pl.BlockSpec((None, d, tf), lambda i, j: (layer, 0, 2 * j)),
            pl.BlockSpec((None, d, tf), lambda i, j: (layer, 0, 2 * j + 1)),
            pl.BlockSpec((None, tf, d), lambda i, j: (layer, 2 * j, 0)),
            pl.BlockSpec((None, tf, d), lambda i, j: (layer, 2 * j + 1, 0)),
            pl.BlockSpec((1, d), lambda i, j: (0, 0)),
        ],
        out_specs=pl.BlockSpec((tm, d), lambda i, j: (i, 0)),
        out_shape=jax.ShapeDtypeStruct((m, d), F32),
        scratch_shapes=[pltpu.VMEM((tm, 128 * (d // FFN_CHUNK)), F32)],
        compiler_params=pltpu.CompilerParams(dimension_semantics=("arbitrary", "arbitrary"),
                                             vmem_limit_bytes=FFN_VMEM_LIMIT_BYTES),
        name="ffn",
    )(x, h, w1, w1, w2, w2, g2.reshape(1, d))


def _rope128(x, cos, sin):
    lane = lax.broadcasted_iota(jnp.int32, x.shape, 1)
    half = MLA_ROPE // 2
    swapped = jnp.where(lane < half, pltpu.roll(x, 128 - half, 1), pltpu.roll(x, half, 1))
    return x * cos + swapped * sin


def _mla_proj_kernel(x_ref, g_ref, wd_ref, qn_ref, kvn_ref, wq_ref, wuk_ref, wuv_ref, cos_ref, sin_ref,
                     ckv_ref, kr_ref, q_ref, *kv_refs):
    d = jnp.dot(_rms(x_ref[...], g_ref[...]).astype(BF16), wd_ref[...], preferred_element_type=F32)
    cos = cos_ref[...]
    sin = sin_ref[...]
    cq = _rms(d[:, :MLA_RANK], qn_ref[...]).astype(BF16)
    ckv = _rms(d[:, MLA_RANK:2 * MLA_RANK], kvn_ref[...])
    ckv_ref[...] = ckv
    kr = _rope128(d[:, 2 * MLA_RANK:], cos, sin)
    kr_ref[...] = kr[:, :MLA_ROPE]

    q = jnp.dot(cq, wq_ref[...], preferred_element_type=F32) * MLA_Q_SCALE
    for h in range(HEADS):
        base = h * MLA_QK
        q_ref[h, :, 0:MLA_NOPE] = q[:, base:base + MLA_NOPE].astype(BF16)
        q_ref[h, :, MLA_NOPE:] = _rope128(q[:, base + MLA_NOPE:base + MLA_QK], cos, sin).astype(BF16)

    if kv_refs:
        k_ref, v_ref = kv_refs
        kr_b = kr.astype(BF16)
        ckv_b = ckv.astype(BF16)
        kn = jnp.dot(ckv_b, wuk_ref[...], preferred_element_type=F32)
        vv = jnp.dot(ckv_b, wuv_ref[...], preferred_element_type=F32)
        ones = jnp.ones((d.shape[0], HEAD_DIM), BF16)
        for h in range(HEADS):
            k_ref[h, :, 0:MLA_NOPE] = kn[:, h * MLA_NOPE:(h + 1) * MLA_NOPE].astype(BF16)
            k_ref[h, :, MLA_NOPE:] = kr_b
            v_ref[h, :, 0:HEAD_DIM] = vv[:, h * HEAD_DIM:(h + 1) * HEAD_DIM].astype(BF16)
            v_ref[h, :, HEAD_DIM:] = ones


def mla_proj(x, g, w_d, q_norm, kv_norm, wq, wuk, wuv, cos, sin, *, tm, with_kv):
    m = x.shape[0]
    full = lambda i: (0, 0)
    rope_tiles = cos.shape[0] // tm
    hm_spec = pl.BlockSpec((HEADS, tm, MLA_QK), lambda i: (0, i, 0))
    hm_shape = jax.ShapeDtypeStruct((HEADS, m, MLA_QK), BF16)
    n_hm = 3 if with_kv else 1
    return pl.pallas_call(
        _mla_proj_kernel,
        grid=(m // tm,),
        in_specs=[
            pl.BlockSpec((tm, D_MODEL), lambda i: (i, 0)),
            pl.BlockSpec((1, D_MODEL), full),
            pl.BlockSpec(w_d.shape, full),
            pl.BlockSpec((1, MLA_RANK), full),
            pl.BlockSpec((1, MLA_RANK), full),
            pl.BlockSpec(wq.shape, full),
            pl.BlockSpec(wuk.shape, full),
            pl.BlockSpec(wuv.shape, full),
            pl.BlockSpec((tm, 128), lambda i: (i % rope_tiles, 0)),
            pl.BlockSpec((tm, 128), lambda i: (i % rope_tiles, 0)),
        ],
        out_specs=[
            pl.BlockSpec((tm, MLA_RANK), lambda i: (i, 0)),
            pl.BlockSpec((tm, MLA_ROPE), lambda i: (i, 0)),
        ] + [hm_spec] * n_hm,
        out_shape=[
            jax.ShapeDtypeStruct((m, MLA_RANK), F32),
            jax.ShapeDtypeStruct((m, MLA_ROPE), F32),
        ] + [hm_shape] * n_hm,
        compiler_params=_params("arbitrary"),
        name="mla_proj",
    )(x, g.reshape(1, -1), w_d, q_norm.reshape(1, -1), kv_norm.reshape(1, -1), wq, wuk, wuv, cos, sin)


def _mla_prompt_kernel(q_ref, k_ref, v_ref, o_ref, s_ref, s2_ref, p_ref, p2_ref, alpha_ref, alpha2_ref,
                       bias_ref, m_ref, acc_ref):
    @pl.when((pl.program_id(0) == 0) & (pl.program_id(1) == 0))
    def _():
        rc = lax.broadcasted_iota(jnp.int32, (MLA_TQ, MLA_TK), 0) // CHUNK
        cc = lax.broadcasted_iota(jnp.int32, (MLA_TQ, MLA_TK), 1) // CHUNK
        bias_ref[...] = jnp.where(cc <= rc, 0.0, NEG)

    def is_last(qi, k0, klen):
        return k0 + klen == (qi + 1) * MLA_TQ

    def width(qi, k0, klen, row0):
        return klen - MLA_HALF if is_last(qi, k0, klen) and row0 < MLA_HALF else klen

    def scores(qi, k0, klen, dst_ref):
        for row0 in range(0, MLA_TQ, MLA_HALF):
            w = width(qi, k0, klen, row0)
            q = q_ref[0, qi * MLA_TQ + row0:qi * MLA_TQ + row0 + MLA_HALF, :]
            dst_ref[row0:row0 + MLA_HALF, :w] = lax.dot_general(
                q, k_ref[0, k0:k0 + w, :], _NT, preferred_element_type=F32)

    def softmax(qi, k0, klen, cur_ref, p_dst, alpha_dst):
        first = k0 == 0
        last = is_last(qi, k0, klen)
        for u in range(MLA_TQ // MLA_SUB):
            rows = slice(u * MLA_SUB, (u + 1) * MLA_SUB)
            if last:
                w = width(qi, k0, klen, u * MLA_SUB)
                d0 = klen - MLA_TK
                parts = [cur_ref[rows, d0:w] + bias_ref[rows, :w - d0]]
                if d0 > 0:
                    parts.insert(0, cur_ref[rows, :d0])
            else:
                parts = [cur_ref[rows, :klen]]
            m_cur = functools.reduce(jnp.maximum, [jnp.max(x, axis=-1, keepdims=True) for x in parts])
            if first:
                m_new = jnp.broadcast_to(m_cur, (MLA_SUB, 128))
            else:
                m_prev = m_ref[rows, :]
                m_new = jnp.maximum(m_prev, m_cur)
                alpha_dst[rows, :] = jnp.exp2(m_prev - m_new)
            if not last:
                m_ref[rows, :] = m_new
            col = 0
            for x in parts:
                p_dst[rows, col:col + x.shape[1]] = jnp.exp2(
                    x - jnp.tile(m_new, (1, x.shape[1] // 128))).astype(BF16)
                col += x.shape[1]

    def values(qi, k0, klen, p_src, alpha_src):
        first = k0 == 0
        last = is_last(qi, k0, klen)
        for u in range(MLA_TQ // MLA_HALF):
            rows = slice(u * MLA_HALF, (u + 1) * MLA_HALF)
            w = width(qi, k0, klen, u * MLA_HALF)
            acc = jnp.dot(p_src[rows, :w], v_ref[0, k0:k0 + w, :], preferred_element_type=F32)
            if not first:
                acc = acc + jnp.tile(alpha_src[rows, :], (1, 2)) * acc_ref[rows, :]
            if last:
                o_ref[qi * MLA_TQ + u * MLA_HALF:qi * MLA_TQ + (u + 1) * MLA_HALF, :] = (
                    acc[:, :HEAD_DIM] / acc[:, HEAD_DIM:]).astype(BF16)
            else:
                acc_ref[rows, :] = acc

    blocks = [(qi, k0, min(MLA_TK_MAX, (qi + 1) * MLA_TQ - k0))
              for qi in range(MLA_NQ) for k0 in range(0, (qi + 1) * MLA_TQ, MLA_TK_MAX)]
    n = len(blocks)
    s_bufs = (s_ref, s2_ref)
    p_bufs = (p_ref, p2_ref)
    a_bufs = (alpha_ref, alpha2_ref)
    scores(*blocks[0], s_bufs[0])
    scores(*blocks[1], s_bufs[1])
    softmax(*blocks[0], s_bufs[0], p_bufs[0], a_bufs[0])
    for t in range(n):
        if t + 2 < n:
            scores(*blocks[t + 2], s_bufs[t % 2])
        if t + 1 < n:
            softmax(*blocks[t + 1], s_bufs[(t + 1) % 2], p_bufs[(t + 1) % 2], a_bufs[(t + 1) % 2])
        values(*blocks[t], p_bufs[t % 2], a_bufs[t % 2])


def mla_prompt(q, k, v):
    return pl.pallas_call(
        _mla_prompt_kernel,
        grid=(BATCH, HEADS),
        in_specs=[
            pl.BlockSpec((1, SEQ, MLA_QK), lambda b, h: (h, b, 0)),
            pl.BlockSpec((1, SEQ, MLA_QK), lambda b, h: (h, b, 0)),
            pl.BlockSpec((1, SEQ, 2 * HEAD_DIM), lambda b, h: (h, b, 0)),
        ],
        out_specs=pl.BlockSpec((SEQ, HEAD_DIM), lambda b, h: (b, h)),
        out_shape=jax.ShapeDtypeStruct((T_PROMPT, D_MODEL), BF16),
        scratch_shapes=[
            pltpu.VMEM((MLA_TQ, MLA_TK_MAX), F32),
            pltpu.VMEM((MLA_TQ, MLA_TK_MAX), F32),
            pltpu.VMEM((MLA_TQ, MLA_TK_MAX), BF16),
            pltpu.VMEM((MLA_TQ, MLA_TK_MAX), BF16),
            pltpu.VMEM((MLA_TQ, 128), F32),
            pltpu.VMEM((MLA_TQ, 128), F32),
            pltpu.VMEM((MLA_TQ, MLA_TK), F32),
            pltpu.VMEM((MLA_TQ, 128), F32),
            pltpu.VMEM((MLA_TQ, 2 * HEAD_DIM), F32),
        ],
        compiler_params=_params("arbitrary", "arbitrary"),
        name="mla_prompt",
    )(q, k, v)


def _mla_sample_kernel(q_ref, cn_ref, rn_ref, cc_ref, cr_ref, wuk_ref, wuv_ref, o_ref,
                       ql_ref, qr_ref):
    cc = cc_ref[0].astype(BF16)
    cr = cr_ref[0].astype(BF16)
    cn = cn_ref[...].astype(BF16)
    rn = rn_ref[...].astype(BF16)
    for g in range(HEADS // MLA_SAMPLE_HEADS):
        heads = range(g * MLA_SAMPLE_HEADS, (g + 1) * MLA_SAMPLE_HEADS)
        grp = slice(g * MLA_SAMPLE_HEADS * DEC_SEQ, (g + 1) * MLA_SAMPLE_HEADS * DEC_SEQ)
        for h in heads:
            rows = slice(h * DEC_SEQ, (h + 1) * DEC_SEQ)
            ql_ref[rows, :] = lax.dot_general(
                q_ref[h, :, 0:MLA_NOPE], wuk_ref[:, h * MLA_NOPE:(h + 1) * MLA_NOPE], _NT,
                preferred_element_type=F32).astype(BF16)
            qr_ref[rows, :] = q_ref[h, :, MLA_NOPE:]
        ql = ql_ref[grp, :]
        qr = qr_ref[grp, :][:, :MLA_ROPE]
        s_c = (lax.dot_general(ql, cc, _NT, preferred_element_type=F32)
               + lax.dot_general(qr, cr, _NT, preferred_element_type=F32))
        s_n = (lax.dot_general(ql, cn, _NT, preferred_element_type=F32)
               + lax.dot_general(qr, rn, _NT, preferred_element_type=F32))
        m = jnp.maximum(jnp.max(s_c, axis=-1, keepdims=True), jnp.max(s_n, axis=-1, keepdims=True))
        p_c = jnp.exp2(s_c - m)
        p_n = jnp.exp2(s_n - m)
        l = jnp.sum(p_c, axis=-1, keepdims=True) + jnp.sum(p_n, axis=-1, keepdims=True)
        o_lat = ((jnp.dot(p_c.astype(BF16), cc, preferred_element_type=F32)
                  + jnp.dot(p_n.astype(BF16), cn, preferred_element_type=F32)) / l).astype(BF16)
        for i, h in enumerate(heads):
            cols = slice(h * HEAD_DIM, (h + 1) * HEAD_DIM)
            o_ref[:, cols] = jnp.dot(o_lat[i * DEC_SEQ:(i + 1) * DEC_SEQ], wuv_ref[:, cols],
                                     preferred_element_type=F32).astype(BF16)


def mla_sample(q, ckv, kr, cache_ckv, cache_kr, wuk, wuv):
    full = lambda s: (0, 0)
    return pl.pallas_call(
        _mla_sample_kernel,
        grid=(DEC_BATCH,),
        in_specs=[
            pl.BlockSpec((HEADS, DEC_SEQ, MLA_QK), lambda s: (0, s, 0)),
            pl.BlockSpec((DEC_SEQ, MLA_RANK), lambda s: (s, 0)),
            pl.BlockSpec((DEC_SEQ, MLA_ROPE), lambda s: (s, 0)),
            pl.BlockSpec((1, PAST_LEN, MLA_RANK), lambda s: (s, 0, 0)),
            pl.BlockSpec((1, PAST_LEN, MLA_ROPE), lambda s: (s, 0, 0)),
            pl.BlockSpec(wuk.shape, full),
            pl.BlockSpec(wuv.shape, full),
        ],
        out_specs=pl.BlockSpec((DEC_SEQ, D_MODEL), lambda s: (s, 0)),
        out_shape=jax.ShapeDtypeStruct((T_SAMPLE, D_MODEL), BF16),
        scratch_shapes=[
            pltpu.VMEM((HEADS * DEC_SEQ, MLA_RANK), BF16),
            pltpu.VMEM((HEADS * DEC_SEQ, 128), BF16),
        ],
        compiler_params=_params("arbitrary"),
        name="mla_sample",
    )(q, ckv, kr, cache_ckv, cache_kr, wuk, wuv)


def _bias_rows(rel_bias):
    far = jnp.broadcast_to(rel_bias[:, 2 * REL_CLIP:], (HEADS, BAND_F // 4))
    mid = rel_bias[:, 1:][:, ::-1]
    return (jnp.concatenate([far, mid, far], axis=1) * LOG2E).reshape(HEADS, 1, BAND_F)


def _rope_tables(pos):
    half = MLA_ROPE // 2
    inv = 1.0 / (ROPE_THETA ** (jnp.arange(half, dtype=F32) / half))
    ang = pos.astype(F32)[:, None] * inv[None, :]
    c, s = jnp.cos(ang), jnp.sin(ang)
    z = jnp.zeros((pos.shape[0], 128 - MLA_ROPE), F32)
    return jnp.concatenate([c, c, z], axis=1), jnp.concatenate([-s, s, z], axis=1)


PROMPT_TM_QKV = 1024
MLA_PROJ_TM = 256


def kernel(x_prompt, x_sample, cache_a_k, cache_a_v, cache_mla_ckv, cache_mla_kr, ln_mix_pre, ln_mix_post, ln_ffn_pre, ln_ffn_post, a_w_qkv, a_w_o, a_rel_bias, mla_w_dq, mla_q_norm, mla_w_uq, mla_w_dkv, mla_kv_norm, mla_w_uk, mla_w_uv, mla_w_o, ffn_w1, ffn_w2):
    xp = x_prompt.reshape(T_PROMPT, D_MODEL)
    xs = x_sample.reshape(T_SAMPLE, D_MODEL)
    w1 = ffn_w1.astype(BF16)
    w2 = ffn_w2.astype(BF16)

    def mix_out_and_ffn(i, o, w_o, x):
        x, h = proj_residual(o, w_o, ln_mix_post[i], ln_ffn_pre[i], x, tm=512)
        return ffn(x, h, w1, w2, ln_ffn_post[i], layer=i, tm=512, tf=1024)

    w_qkv = a_w_qkv[0].astype(BF16)
    w_o = a_w_o[0].astype(BF16)
    f_rows = _bias_rows(a_rel_bias[0])
    hm_p, k_p, v_p = norm_qkv(xp, ln_mix_pre[0], w_qkv, tm=PROMPT_TM_QKV, keep=BAND_ROWS,
                              tiles_per_keep=SEQ // PROMPT_TM_QKV)
    hm_s, k_s, v_s = norm_qkv(xs, ln_mix_pre[0], w_qkv, tm=512, keep=512, tiles_per_keep=1)
    o_p = band_prompt(hm_p, f_rows)
    o_s = band_sample(hm_s,
                      cache_a_k[0].reshape(DEC_BATCH, BAND_ROWS * HEADS, HEAD_DIM),
                      cache_a_v[0].reshape(DEC_BATCH, BAND_ROWS * HEADS, HEAD_DIM), f_rows)
    xp = mix_out_and_ffn(0, o_p, w_o, xp)
    xs = mix_out_and_ffn(0, o_s, w_o, xs)

    pad = jnp.zeros((D_MODEL, 128 - MLA_ROPE), F32)
    w_d = jnp.concatenate([mla_w_dq[0], mla_w_dkv[0], pad], axis=1).astype(BF16)
    wq = jnp.pad(mla_w_uq[0], ((0, 0), (0, 0), (0, MLA_QK - MLA_NOPE - MLA_ROPE)))
    wq = wq.reshape(MLA_RANK, HEADS * MLA_QK).astype(BF16)
    wuk = mla_w_uk[0].reshape(MLA_RANK, HEADS * MLA_NOPE).astype(BF16)
    wuv = mla_w_uv[0].reshape(MLA_RANK, HEADS * HEAD_DIM).astype(BF16)
    w_o = mla_w_o[0].astype(BF16)
    cos_p, sin_p = _rope_tables(jnp.arange(SEQ))
    cos_s, sin_s = _rope_tables(jnp.tile(PAST_LEN + jnp.arange(DEC_SEQ), MLA_PROJ_TM // DEC_SEQ))

    ckv_p, kr_p, q_p, kk_p, vv_p = mla_proj(xp, ln_mix_pre[1], w_d, mla_q_norm[0], mla_kv_norm[0],
                                            wq, wuk, wuv, cos_p, sin_p, tm=MLA_PROJ_TM, with_kv=True)
    ckv_s, kr_s, q_s = mla_proj(xs, ln_mix_pre[1], w_d, mla_q_norm[0], mla_kv_norm[0],
                                wq, wuk, wuv, cos_s, sin_s, tm=MLA_PROJ_TM, with_kv=False)
    o_p = mla_prompt(q_p, kk_p, vv_p)
    o_s = mla_sample(q_s, ckv_s, kr_s, cache_mla_ckv[0], cache_mla_kr[0], wuk, wuv)
    xp = mix_out_and_ffn(1, o_p, w_o, xp)
    xs = mix_out_and_ffn(1, o_s, w_o, xs)

    return (
        xp.reshape(BATCH, SEQ, D_MODEL),
        xs.reshape(DEC_BATCH, DEC_SEQ, D_MODEL),
        k_p.reshape(1, BATCH, BAND_ROWS, HEADS, HEAD_DIM),
        v_p.reshape(1, BATCH, BAND_ROWS, HEADS, HEAD_DIM),
        k_s.reshape(1, DEC_BATCH, DEC_SEQ, HEADS, HEAD_DIM),
        v_s.reshape(1, DEC_BATCH, DEC_SEQ, HEADS, HEAD_DIM),
        ckv_p.reshape(1, BATCH, SEQ, MLA_RANK),
        kr_p.reshape(1, BATCH, SEQ, MLA_ROPE),
        ckv_s.reshape(1, DEC_BATCH, DEC_SEQ, MLA_RANK),
        kr_s.reshape(1, DEC_BATCH, DEC_SEQ, MLA_ROPE),
    )
```

```python
import functools
import math

import jax
import jax.numpy as jnp
from jax import lax
from jax.experimental import pallas as pl
from jax.experimental.pallas import tpu as pltpu

F32 = jnp.float32
BF16 = jnp.bfloat16

D_MODEL = 2048
BATCH = 4
SEQ = 4096
DEC_BATCH = 16
DEC_SEQ = 64
PAST_LEN = 2048
CHUNK = 64
HEADS = 16
HEAD_DIM = 128
BAND_ROWS = 512
REL_CLIP = 256
MLA_RANK = 512
MLA_NOPE = 128
MLA_ROPE = 64
MLA_QK = 256
ROPE_THETA = 10000.0
D_FF = 8192
EPS = 1e-6
NEG = -1e30
LOG2E = math.log2(math.e)

T_PROMPT = BATCH * SEQ
T_SAMPLE = DEC_BATCH * DEC_SEQ

VMEM_LIMIT_BYTES = 58 * 1024 * 1024
FFN_VMEM_LIMIT_BYTES = 62 * 1024 * 1024

BAND_TQ = 256
BAND_TK = BAND_ROWS + BAND_TQ
BAND_F = 1024
BAND_NQ = SEQ // BAND_TQ
BAND_SUB = 32
BAND_Q_SCALE = HEAD_DIM ** -0.5 * LOG2E

MLA_TQ = 512
MLA_TK = 512
MLA_TK_MAX = 1024
MLA_NQ = SEQ // MLA_TQ
MLA_SUB = 32
MLA_HALF = MLA_TQ // 2
MLA_Q_SCALE = (MLA_NOPE + MLA_ROPE) ** -0.5 * LOG2E

MLA_SAMPLE_HEADS = 16
QKV_TN = 1024
QKV_CHUNK = 256
QKV_ROWS = 256
PROJ_SUB = 128
FFN_CHUNK = 512

_NT = (((1,), (1,)), ((), ()))


def _params(*sem):
    return pltpu.CompilerParams(dimension_semantics=sem, vmem_limit_bytes=VMEM_LIMIT_BYTES)


def _rms(x, g):
    ms = jnp.mean(x * x, axis=-1, keepdims=True)
    return x * lax.rsqrt(ms + EPS) * g


def _norm_qkv_kernel(x_ref, g_ref, w_ref, hm_ref, kf_ref, vf_ref, h_ref, *, keep):
    j = pl.program_id(1)
    tm = x_ref.shape[0]
    heads_per_chunk = QKV_CHUNK // HEAD_DIM

    def project_first():
        g = g_ref[...]
        for u in range(tm // QKV_ROWS):
            rows = slice(u * QKV_ROWS, (u + 1) * QKV_ROWS)
            hu = _rms(x_ref[rows, :], g).astype(BF16)
            h_ref[rows, :] = hu
            kept = u * QKV_ROWS - (tm - keep)
            for c in range(QKV_TN // QKV_CHUNK):
                cols = slice(c * QKV_CHUNK, (c + 1) * QKV_CHUNK)
                acc = jnp.dot(hu, w_ref[:, cols], preferred_element_type=F32)
                if kept >= 0:
                    kf_ref[kept:kept + QKV_ROWS, cols] = acc
                for hh in range(heads_per_chunk):
                    hm_ref[c * heads_per_chunk + hh, rows, :] = (
                        acc[:, hh * HEAD_DIM:(hh + 1) * HEAD_DIM].astype(BF16))

    def project(scale, f_ref, f_col):
        h = h_ref[...]
        for c in range(QKV_TN // QKV_CHUNK):
            cols = slice(c * QKV_CHUNK, (c + 1) * QKV_CHUNK)
            acc = jnp.dot(h, w_ref[:, cols], preferred_element_type=F32)
            if f_ref is not None:
                f_ref[:, f_col + c * QKV_CHUNK:f_col + (c + 1) * QKV_CHUNK] = acc[tm - keep:, :]
            if scale is not None:
                acc = acc * scale
            for hh in range(heads_per_chunk):
                hm_ref[c * heads_per_chunk + hh] = (
                    acc[:, hh * HEAD_DIM:(hh + 1) * HEAD_DIM].astype(BF16))

    pl.when(j == 0)(project_first)
    for jj, f_ref in ((1, kf_ref), (2, vf_ref), (3, vf_ref)):
        pl.when(j == jj)(functools.partial(project, None, f_ref, (jj % 2) * QKV_TN))

    pl.when(j >= 4)(functools.partial(project, BAND_Q_SCALE, None, 0))


def norm_qkv(x, g, w, *, tm, keep, tiles_per_keep):
    m, k = x.shape
    nb = 3 * D_MODEL // QKV_TN
    heads_per_block = QKV_TN // HEAD_DIM
    n_keep = m // (tm * tiles_per_keep) * keep

    def col(j):
        return (j + 2) % nb

    return pl.pallas_call(
        functools.partial(_norm_qkv_kernel, keep=keep),
        grid=(m // tm, nb),
        in_specs=[
            pl.BlockSpec((tm, k), lambda i, j: (i, 0)),
            pl.BlockSpec((1, k), lambda i, j: (0, 0)),
            pl.BlockSpec((k, QKV_TN), lambda i, j: (0, col(j))),
        ],
        out_specs=[
            pl.BlockSpec((heads_per_block, tm, HEAD_DIM), lambda i, j: (col(j), i, 0)),
            pl.BlockSpec((keep, D_MODEL), lambda i, j: (i // tiles_per_keep, 0)),
            pl.BlockSpec((keep, D_MODEL), lambda i, j: (i // tiles_per_keep, 0)),
        ],
        out_shape=[
            jax.ShapeDtypeStruct((3 * HEADS, m, HEAD_DIM), BF16),
            jax.ShapeDtypeStruct((n_keep, D_MODEL), F32),
            jax.ShapeDtypeStruct((n_keep, D_MODEL), F32),
        ],
        scratch_shapes=[pltpu.VMEM((tm, k), BF16)],
        compiler_params=_params("arbitrary", "arbitrary"),
        name="norm_qkv",
    )(x, g.reshape(1, k), w)


def _toeplitz_bias(f_row, rows):
    x = jnp.broadcast_to(f_row, (rows, BAND_F))
    return pltpu.roll(x, 0, 1, stride=1, stride_axis=0)


def _band_prompt_kernel(q_ref, k_ref, v_ref, f_ref, o_ref, vcat_ref, bias_ref, s_ref, s2_ref,
                        p_ref, p2_ref):
    @pl.when((pl.program_id(0) == 0) & (pl.program_id(1) == 0))
    def _():
        vcat_ref[:, HEAD_DIM:] = jnp.ones((SEQ, HEAD_DIM), BF16)

    @pl.when(pl.program_id(1) == 0)
    def _():
        r = lax.broadcasted_iota(jnp.int32, (BAND_TQ, BAND_TK), 0) // CHUNK
        jc = lax.broadcasted_iota(jnp.int32, (BAND_TQ, BAND_TK), 1) // CHUNK
        allowed = (jc >= r) & (jc <= r + BAND_ROWS // CHUNK)
        bias_ref[...] = jnp.where(allowed, _toeplitz_bias(f_ref[0], BAND_TQ)[:, :BAND_TK], NEG)

    vcat_ref[:, 0:HEAD_DIM] = v_ref[0]

    def window(qi):
        end = (qi + 1) * BAND_TQ
        start = max(end - BAND_TK, 0)
        return start, end - start

    def scores(qi, dst_ref):
        k0, w = window(qi)
        dst_ref[:, :w] = lax.dot_general(q_ref[0, qi * BAND_TQ:(qi + 1) * BAND_TQ, :],
                                         k_ref[0, k0:k0 + w, :], _NT, preferred_element_type=F32)

    def softmax(qi, cur_ref, p_dst):
        _, w = window(qi)
        for u in range(BAND_TQ // BAND_SUB):
            rows = slice(u * BAND_SUB, (u + 1) * BAND_SUB)
            s = cur_ref[rows, :w] + bias_ref[rows, BAND_TK - w:]
            p_dst[rows, :w] = jnp.exp2(s - jnp.max(s, axis=-1, keepdims=True)).astype(BF16)

    def values(qi, p_src):
        k0, w = window(qi)
        pv = jnp.dot(p_src[:, :w], vcat_ref[k0:k0 + w, :], preferred_element_type=F32)
        o_ref[qi * BAND_TQ:(qi + 1) * BAND_TQ, :] = (
            pv[:, :HEAD_DIM] / pv[:, HEAD_DIM:]).astype(BF16)

    s_bufs = (s_ref, s2_ref)
    p_bufs = (p_ref, p2_ref)
    scores(0, s_bufs[0])
    scores(1, s_bufs[1])
    softmax(0, s_bufs[0], p_bufs[0])
    for t in range(BAND_NQ):
        if t + 2 < BAND_NQ:
            scores(t + 2, s_bufs[t % 2])
        if t + 1 < BAND_NQ:
            softmax(t + 1, s_bufs[(t + 1) % 2], p_bufs[(t + 1) % 2])
        values(t, p_bufs[t % 2])


def band_prompt(qkv_hm, f_rows):
    return pl.pallas_call(
        _band_prompt_kernel,
        grid=(HEADS, BATCH),
        in_specs=[
            pl.BlockSpec((1, SEQ, HEAD_DIM), lambda h, b: (h, b, 0)),
            pl.BlockSpec((1, SEQ, HEAD_DIM), lambda h, b: (HEADS + h, b, 0)),
            pl.BlockSpec((1, SEQ, HEAD_DIM), lambda h, b: (2 * HEADS + h, b, 0)),
            pl.BlockSpec((1, 1, BAND_F), lambda h, b: (h, 0, 0)),
        ],
        out_specs=pl.BlockSpec((SEQ, HEAD_DIM), lambda h, b: (b, h)),
        out_shape=jax.ShapeDtypeStruct((T_PROMPT, D_MODEL), BF16),
        scratch_shapes=[
            pltpu.VMEM((SEQ, 2 * HEAD_DIM), BF16),
            pltpu.VMEM((BAND_TQ, BAND_TK), F32),
            pltpu.VMEM((BAND_TQ, BAND_TK), F32),
            pltpu.VMEM((BAND_TQ, BAND_TK), F32),
            pltpu.VMEM((BAND_TQ, BAND_TK), BF16),
            pltpu.VMEM((BAND_TQ, BAND_TK), BF16),
        ],
        compiler_params=_params("arbitrary", "arbitrary"),
        name="band_prompt",
    )(qkv_hm, qkv_hm, qkv_hm, f_rows)


def _band_sample_kernel(q_ref, kn_ref, vn_ref, ck_ref, cv_ref, f_ref, o_ref):
    for h in range(HEADS):
        cols = slice(h * HEAD_DIM, (h + 1) * HEAD_DIM)
        bias = _toeplitz_bias(f_ref[h], DEC_SEQ)
        q = q_ref[h]
        kc = ck_ref[0, pl.ds(h, BAND_ROWS, stride=HEADS), :].astype(BF16)
        vc = cv_ref[0, pl.ds(h, BAND_ROWS, stride=HEADS), :].astype(BF16)
        s_c = lax.dot_general(q, kc, _NT, preferred_element_type=F32) + bias[:, :BAND_ROWS]
        s_n = (lax.dot_general(q, kn_ref[h], _NT, preferred_element_type=F32)
               + bias[:, BAND_ROWS:BAND_ROWS + DEC_SEQ])
        m = jnp.maximum(jnp.max(s_c, axis=-1, keepdims=True), jnp.max(s_n, axis=-1, keepdims=True))
        p_c = jnp.exp2(s_c - m)
        p_n = jnp.exp2(s_n - m)
        l = jnp.sum(p_c, axis=-1, keepdims=True) + jnp.sum(p_n, axis=-1, keepdims=True)
        o = (jnp.dot(p_c.astype(BF16), vc, preferred_element_type=F32)
             + jnp.dot(p_n.astype(BF16), vn_ref[h], preferred_element_type=F32)) / l
        o_ref[:, cols] = o.astype(BF16)


def band_sample(qkv_hm, cache_k, cache_v, f_rows):
    return pl.pallas_call(
        _band_sample_kernel,
        grid=(DEC_BATCH,),
        in_specs=[
            pl.BlockSpec((HEADS, DEC_SEQ, HEAD_DIM), lambda s: (0, s, 0)),
            pl.BlockSpec((HEADS, DEC_SEQ, HEAD_DIM), lambda s: (1, s, 0)),
            pl.BlockSpec((HEADS, DEC_SEQ, HEAD_DIM), lambda s: (2, s, 0)),
            pl.BlockSpec((1, BAND_ROWS * HEADS, HEAD_DIM), lambda s: (s, 0, 0)),
            pl.BlockSpec((1, BAND_ROWS * HEADS, HEAD_DIM), lambda s: (s, 0, 0)),
            pl.BlockSpec((HEADS, 1, BAND_F), lambda s: (0, 0, 0)),
        ],
        out_specs=pl.BlockSpec((DEC_SEQ, D_MODEL), lambda s: (s, 0)),
        out_shape=jax.ShapeDtypeStruct((T_SAMPLE, D_MODEL), BF16),
        compiler_params=_params("arbitrary"),
        name="band_sample",
    )(qkv_hm, qkv_hm, qkv_hm, cache_k, cache_v, f_rows)


def _proj_residual_kernel(a_ref, w_ref, g_ref, gn_ref, x_ref, o_ref, h_ref):
    w = w_ref[...]
    g = g_ref[...]
    gn = gn_ref[...]
    for u in range(a_ref.shape[0] // PROJ_SUB):
        rows = slice(u * PROJ_SUB, (u + 1) * PROJ_SUB)
        y = jnp.dot(a_ref[rows, :], w, preferred_element_type=F32)
        x1 = x_ref[rows, :] + _rms(y, g)
        o_ref[rows, :] = x1
        h_ref[rows, :] = _rms(x1, gn).astype(BF16)


def proj_residual(a, w, g, g_next, x, *, tm):
    m, k = a.shape
    n = w.shape[1]
    return pl.pallas_call(
        _proj_residual_kernel,
        grid=(m // tm,),
        in_specs=[
            pl.BlockSpec((tm, k), lambda i: (i, 0)),
            pl.BlockSpec((k, n), lambda i: (0, 0)),
            pl.BlockSpec((1, n), lambda i: (0, 0)),
            pl.BlockSpec((1, n), lambda i: (0, 0)),
            pl.BlockSpec((tm, n), lambda i: (i, 0)),
        ],
        out_specs=[
            pl.BlockSpec((tm, n), lambda i: (i, 0)),
            pl.BlockSpec((tm, n), lambda i: (i, 0)),
        ],
        out_shape=[
            jax.ShapeDtypeStruct((m, n), F32),
            jax.ShapeDtypeStruct((m, n), BF16),
        ],
        compiler_params=_params("arbitrary"),
        name="proj_residual",
    )(a, w, g.reshape(1, n), g_next.reshape(1, n), x)


def _ffn_kernel(x_ref, h_ref, w1a_ref, w1b_ref, w2a_ref, w2b_ref, g2_ref, o_ref, ssq_ref):
    j = pl.program_id(1)

    @pl.when(j == 0)
    def _():
        o_ref[...] = jnp.zeros(o_ref.shape, F32)

    def hidden(w1_ref):
        a = jnp.maximum(jnp.dot(h_ref[...], w1_ref[...], preferred_element_type=F32), 0.0)
        return (a * a).astype(BF16)

    aa = hidden(w1a_ref)
    ab = hidden(w1b_ref)
    for c in range(o_ref.shape[1] // FFN_CHUNK):
        cols = slice(c * FFN_CHUNK, (c + 1) * FFN_CHUNK)
        y = (o_ref[:, cols] + jnp.dot(aa, w2a_ref[:, cols], preferred_element_type=F32)
             + jnp.dot(ab, w2b_ref[:, cols], preferred_element_type=F32))
        o_ref[:, cols] = y
        sq = y * y
        ssq_ref[:, c * 128:(c + 1) * 128] = functools.reduce(
            jnp.add, [sq[:, b * 128:(b + 1) * 128] for b in range(FFN_CHUNK // 128)])

    @pl.when(j == pl.num_programs(1) - 1)
    def _():
        ms = jnp.sum(ssq_ref[...], axis=-1, keepdims=True) * (1.0 / o_ref.shape[1])
        o_ref[...] = x_ref[...] + o_ref[...] * lax.rsqrt(ms + EPS) * g2_ref[...]


def ffn(x, h, w1, w2, g2, *, layer, tm, tf):
    m, d = x.shape
    f = w1.shape[2]
    return pl.pallas_call(
        _ffn_kernel,
        grid=(m // tm, f // (2 * tf)),
        in_specs=[
            pl.BlockSpec((tm, d), lambda i, j: (i, 0)),
            pl.BlockSpec((tm, d), lambda i, j: (i, 0)),
            pl.BlockSpec((None, d, tf), lambda i, j: (layer, 0, 2 * j)),
            pl.BlockSpec((None, d, tf), lambda i, j: (layer, 0, 2 * j + 1)),
            pl.BlockSpec((None, tf, d), lambda i, j: (layer, 2 * j, 0)),
            pl.BlockSpec((None, tf, d), lambda i, j: (layer, 2 * j + 1, 0)),
            pl.BlockSpec((1, d), lambda i, j: (0, 0)),
        ],
        out_specs=pl.BlockSpec((tm, d), lambda i, j: (i, 0)),
        out_shape=jax.ShapeDtypeStruct((m, d), F32),
        scratch_shapes=[pltpu.VMEM((tm, 128 * (d // FFN_CHUNK)), F32)],
        compiler_params=pltpu.CompilerParams(dimension_semantics=("arbitrary", "arbitrary"),
                                             vmem_limit_bytes=FFN_VMEM_LIMIT_BYTES),
        name="ffn",
    )(x, h, w1, w1, w2, w2, g2.reshape(1, d))


def _rope128(x, cos, sin):
    lane = lax.broadcasted_iota(jnp.int32, x.shape, 1)
    half = MLA_ROPE // 2
    swapped = jnp.where(lane < half, pltpu.roll(x, 128 - half, 1), pltpu.roll(x, half, 1))
    return x * cos + swapped * sin


def _mla_proj_kernel(x_ref, g_ref, wd_ref, qn_ref, kvn_ref, wq_ref, wuk_ref, wuv_ref, cos_ref, sin_ref,
                     ckv_ref, kr_ref, q_ref, *kv_refs):
    for u in range(x_ref.shape[0] // MLA_PROJ_SUB):
        rows = slice(u * MLA_PROJ_SUB, (u + 1) * MLA_PROJ_SUB)
        d = jnp.dot(_rms(x_ref[rows, :], g_ref[...]).astype(BF16), wd_ref[...],
                    preferred_element_type=F32)
        cos = cos_ref[rows, :]
        sin = sin_ref[rows, :]
        cq = _rms(d[:, :MLA_RANK], qn_ref[...]).astype(BF16)
        ckv = _rms(d[:, MLA_RANK:2 * MLA_RANK], kvn_ref[...])
        ckv_ref[rows, :] = ckv
        kr = _rope128(d[:, 2 * MLA_RANK:], cos, sin)
        kr_ref[rows, :] = kr[:, :MLA_ROPE]

        q = jnp.dot(cq, wq_ref[...], preferred_element_type=F32) * MLA_Q_SCALE
        for h in range(HEADS):
            base = h * MLA_QK
            q_ref[h, rows, 0:MLA_NOPE] = q[:, base:base + MLA_NOPE].astype(BF16)
            q_ref[h, rows, MLA_NOPE:] = _rope128(
                q[:, base + MLA_NOPE:base + MLA_QK], cos, sin).astype(BF16)

        if kv_refs:
            k_ref, v_ref = kv_refs
            kr_b = kr.astype(BF16)
            ckv_b = ckv.astype(BF16)
            kn = jnp.dot(ckv_b, wuk_ref[...], preferred_element_type=F32)
            vv = jnp.dot(ckv_b, wuv_ref[...], preferred_element_type=F32)
            ones = jnp.ones((MLA_PROJ_SUB, HEAD_DIM), BF16)
            for h in range(HEADS):
                k_ref[h, rows, 0:MLA_NOPE] = kn[:, h * MLA_NOPE:(h + 1) * MLA_NOPE].astype(BF16)
                k_ref[h, rows, MLA_NOPE:] = kr_b
                v_ref[h, rows, 0:HEAD_DIM] = vv[:, h * HEAD_DIM:(h + 1) * HEAD_DIM].astype(BF16)
                v_ref[h, rows, HEAD_DIM:] = ones


def mla_proj(x, g, w_d, q_norm, kv_norm, wq, wuk, wuv, cos, sin, *, tm, with_kv):
    m = x.shape[0]
    full = lambda i: (0, 0)
    rope_tiles = cos.shape[0] // tm
    hm_spec = pl.BlockSpec((HEADS, tm, MLA_QK), lambda i: (0, i, 0))
    hm_shape = jax.ShapeDtypeStruct((HEADS, m, MLA_QK), BF16)
    n_hm = 3 if with_kv else 1
    return pl.pallas_call(
        _mla_proj_kernel,
        grid=(m // tm,),
        in_specs=[
            pl.BlockSpec((tm, D_MODEL), lambda i: (i, 0)),
            pl.BlockSpec((1, D_MODEL), full),
            pl.BlockSpec(w_d.shape, full),
            pl.BlockSpec((1, MLA_RANK), full),
            pl.BlockSpec((1, MLA_RANK), full),
            pl.BlockSpec(wq.shape, full),
            pl.BlockSpec(wuk.shape, full),
            pl.BlockSpec(wuv.shape, full),
            pl.BlockSpec((tm, 128), lambda i: (i % rope_tiles, 0)),
            pl.BlockSpec((tm, 128), lambda i: (i % rope_tiles, 0)),
        ],
        out_specs=[
            pl.BlockSpec((tm, MLA_RANK), lambda i: (i, 0)),
            pl.BlockSpec((tm, MLA_ROPE), lambda i: (i, 0)),
        ] + [hm_spec] * n_hm,
        out_shape=[
            jax.ShapeDtypeStruct((m, MLA_RANK), F32),
            jax.ShapeDtypeStruct((m, MLA_ROPE), F32),
        ] + [hm_shape] * n_hm,
        compiler_params=_params("arbitrary"),
        name="mla_proj",
    )(x, g.reshape(1, -1), w_d, q_norm.reshape(1, -1), kv_norm.reshape(1, -1), wq, wuk, wuv, cos, sin)


def _mla_prompt_kernel(q_ref, k_ref, v_ref, o_ref, s_ref, s2_ref, p_ref, p2_ref, alpha_ref, alpha2_ref,
                       bias_ref, m_ref, acc_ref):
    @pl.when((pl.program_id(0) == 0) & (pl.program_id(1) == 0))
    def _():
        rc = lax.broadcasted_iota(jnp.int32, (MLA_TQ, MLA_TK), 0) // CHUNK
        cc = lax.broadcasted_iota(jnp.int32, (MLA_TQ, MLA_TK), 1) // CHUNK
        bias_ref[...] = jnp.where(cc <= rc, 0.0, NEG)

    def is_last(qi, k0, klen):
        return k0 + klen == (qi + 1) * MLA_TQ

    def width(qi, k0, klen, row0):
        return klen - MLA_HALF if is_last(qi, k0, klen) and row0 < MLA_HALF else klen

    def scores(qi, k0, klen, dst_ref):
        for row0 in range(0, MLA_TQ, MLA_HALF):
            w = width(qi, k0, klen, row0)
            q = q_ref[0, qi * MLA_TQ + row0:qi * MLA_TQ + row0 + MLA_HALF, :]
            dst_ref[row0:row0 + MLA_HALF, :w] = lax.dot_general(
                q, k_ref[0, k0:k0 + w, :], _NT, preferred_element_type=F32)

    def softmax(qi, k0, klen, cur_ref, p_dst, alpha_dst):
        first = k0 == 0
        last = is_last(qi, k0, klen)
        for u in range(MLA_TQ // MLA_SUB):
            rows = slice(u * MLA_SUB, (u + 1) * MLA_SUB)
            if last:
                w = width(qi, k0, klen, u * MLA_SUB)
                d0 = klen - MLA_TK
                parts = [cur_ref[rows, d0:w] + bias_ref[rows, :w - d0]]
                if d0 > 0:
                    parts.insert(0, cur_ref[rows, :d0])
            else:
                parts = [cur_ref[rows, :klen]]
            m_cur = functools.reduce(jnp.maximum, [jnp.max(x, axis=-1, keepdims=True) for x in parts])
            if first:
                m_new = jnp.broadcast_to(m_cur, (MLA_SUB, 128))
            else:
                m_prev = m_ref[rows, :]
                m_new = jnp.maximum(m_prev, m_cur)
                alpha_dst[rows, :] = jnp.exp2(m_prev - m_new)
            if not last:
                m_ref[rows, :] = m_new
            col = 0
            for x in parts:
                p_dst[rows, col:col + x.shape[1]] = jnp.exp2(
                    x - jnp.tile(m_new, (1, x.shape[1] // 128))).astype(BF16)
                col += x.shape[1]

    def values(qi, k0, klen, p_src, alpha_src):
        first = k0 == 0
        last = is_last(qi, k0, klen)
        for u in range(MLA_TQ // MLA_HALF):
            rows = slice(u * MLA_HALF, (u + 1) * MLA_HALF)
            w = width(qi, k0, klen, u * MLA_HALF)
            acc = jnp.dot(p_src[rows, :w], v_ref[0, k0:k0 + w, :], preferred_element_type=F32)
            if not first:
                acc = acc + jnp.tile(alpha_src[rows, :], (1, 2)) * acc_ref[rows, :]
            if last:
                o_ref[qi * MLA_TQ + u * MLA_HALF:qi * MLA_TQ + (u + 1) * MLA_HALF, :] = (
                    acc[:, :HEAD_DIM] / acc[:, HEAD_DIM:]).astype(BF16)
            else:
                acc_ref[rows, :] = acc

    blocks = [(qi, k0, min(MLA_TK_MAX, (qi + 1) * MLA_TQ - k0))
              for qi in range(MLA_NQ) for k0 in range(0, (qi + 1) * MLA_TQ, MLA_TK_MAX)]
    n = len(blocks)
    s_bufs = (s_ref, s2_ref)
    p_bufs = (p_ref, p2_ref)
    a_bufs = (alpha_ref, alpha2_ref)
    scores(*blocks[0], s_bufs[0])
    scores(*blocks[1], s_bufs[1])
    softmax(*blocks[0], s_bufs[0], p_bufs[0], a_bufs[0])
    for t in range(n):
        if t + 2 < n:
            scores(*blocks[t + 2], s_bufs[t % 2])
        if t + 1 < n:
            softmax(*blocks[t + 1], s_bufs[(t + 1) % 2], p_bufs[(t + 1) % 2], a_bufs[(t + 1) % 2])
        values(*blocks[t], p_bufs[t % 2], a_bufs[t % 2])


def mla_prompt(q, k, v):
    return pl.pallas_call(
        _mla_prompt_kernel,
        grid=(BATCH, HEADS),
        in_specs=[
            pl.BlockSpec((1, SEQ, MLA_QK), lambda b, h: (h, b, 0)),
            pl.BlockSpec((1, SEQ, MLA_QK), lambda b, h: (h, b, 0)),
            pl.BlockSpec((1, SEQ, 2 * HEAD_DIM), lambda b, h: (h, b, 0)),
        ],
        out_specs=pl.BlockSpec((SEQ, HEAD_DIM), lambda b, h: (b, h)),
        out_shape=jax.ShapeDtypeStruct((T_PROMPT, D_MODEL), BF16),
        scratch_shapes=[
            pltpu.VMEM((MLA_TQ, MLA_TK_MAX), F32),
            pltpu.VMEM((MLA_TQ, MLA_TK_MAX), F32),
            pltpu.VMEM((MLA_TQ, MLA_TK_MAX), BF16),
            pltpu.VMEM((MLA_TQ, MLA_TK_MAX), BF16),
            pltpu.VMEM((MLA_TQ, 128), F32),
            pltpu.VMEM((MLA_TQ, 128), F32),
            pltpu.VMEM((MLA_TQ, MLA_TK), F32),
            pltpu.VMEM((MLA_TQ, 128), F32),
            pltpu.VMEM((MLA_TQ, 2 * HEAD_DIM), F32),
        ],
        compiler_params=_params("arbitrary", "arbitrary"),
        name="mla_prompt",
    )(q, k, v)


def _mla_sample_kernel(q_ref, cn_ref, rn_ref, cc_ref, cr_ref, wuk_ref, wuv_ref, o_ref,
                       ql_ref, qr_ref):
    cc = cc_ref[0].astype(BF16)
    cr = cr_ref[0].astype(BF16)
    cn = cn_ref[...].astype(BF16)
    rn = rn_ref[...].astype(BF16)
    for g in range(HEADS // MLA_SAMPLE_HEADS):
        heads = range(g * MLA_SAMPLE_HEADS, (g + 1) * MLA_SAMPLE_HEADS)
        grp = slice(g * MLA_SAMPLE_HEADS * DEC_SEQ, (g + 1) * MLA_SAMPLE_HEADS * DEC_SEQ)
        for h in heads:
            rows = slice(h * DEC_SEQ, (h + 1) * DEC_SEQ)
            ql_ref[rows, :] = lax.dot_general(
                q_ref[h, :, 0:MLA_NOPE], wuk_ref[:, h * MLA_NOPE:(h + 1) * MLA_NOPE], _NT,
                preferred_element_type=F32).astype(BF16)
            qr_ref[rows, :] = q_ref[h, :, MLA_NOPE:]
        ql = ql_ref[grp, :]
        qr = qr_ref[grp, :][:, :MLA_ROPE]
        s_c = (lax.dot_general(ql, cc, _NT, preferred_element_type=F32)
               + lax.dot_general(qr, cr, _NT, preferred_element_type=F32))
        s_n = (lax.dot_general(ql, cn, _NT, preferred_element_type=F32)
               + lax.dot_general(qr, rn, _NT, preferred_element_type=F32))
        m = jnp.maximum(jnp.max(s_c, axis=-1, keepdims=True), jnp.max(s_n, axis=-1, keepdims=True))
        p_c = jnp.exp2(s_c - m)
        p_n = jnp.exp2(s_n - m)
        l = jnp.sum(p_c, axis=-1, keepdims=True) + jnp.sum(p_n, axis=-1, keepdims=True)
        o_lat = ((jnp.dot(p_c.astype(BF16), cc, preferred_element_type=F32)
                  + jnp.dot(p_n.astype(BF16), cn, preferred_element_type=F32)) / l).astype(BF16)
        for i, h in enumerate(heads):
            cols = slice(h * HEAD_DIM, (h + 1) * HEAD_DIM)
            o_ref[:, cols] = jnp.dot(o_lat[i * DEC_SEQ:(i + 1) * DEC_SEQ], wuv_ref[:, cols],
                                     preferred_element_type=F32).astype(BF16)


def mla_sample(q, ckv, kr, cache_ckv, cache_kr, wuk, wuv):
    full = lambda s: (0, 0)
    return pl.pallas_call(
        _mla_sample_kernel,
        grid=(DEC_BATCH,),
        in_specs=[
            pl.BlockSpec((HEADS, DEC_SEQ, MLA_QK), lambda s: (0, s, 0)),
            pl.BlockSpec((DEC_SEQ, MLA_RANK), lambda s: (s, 0)),
            pl.BlockSpec((DEC_SEQ, MLA_ROPE), lambda s: (s, 0)),
            pl.BlockSpec((1, PAST_LEN, MLA_RANK), lambda s: (s, 0, 0)),
            pl.BlockSpec((1, PAST_LEN, MLA_ROPE), lambda s: (s, 0, 0)),
            pl.BlockSpec(wuk.shape, full),
            pl.BlockSpec(wuv.shape, full),
        ],
        out_specs=pl.BlockSpec((DEC_SEQ, D_MODEL), lambda s: (s, 0)),
        out_shape=jax.ShapeDtypeStruct((T_SAMPLE, D_MODEL), BF16),
        scratch_shapes=[
            pltpu.VMEM((HEADS * DEC_SEQ, MLA_RANK), BF16),
            pltpu.VMEM((HEADS * DEC_SEQ, 128), BF16),
        ],
        compiler_params=_params("arbitrary"),
        name="mla_sample",
    )(q, ckv, kr, cache_ckv, cache_kr, wuk, wuv)


def _bias_rows(rel_bias):
    far = jnp.broadcast_to(rel_bias[:, 2 * REL_CLIP:], (HEADS, BAND_F // 4))
    mid = rel_bias[:, 1:][:, ::-1]
    return (jnp.concatenate([far, mid, far], axis=1) * LOG2E).reshape(HEADS, 1, BAND_F)


def _rope_tables(pos):
    half = MLA_ROPE // 2
    inv = 1.0 / (ROPE_THETA ** (jnp.arange(half, dtype=F32) / half))
    ang = pos.astype(F32)[:, None] * inv[None, :]
    c, s = jnp.cos(ang), jnp.sin(ang)
    z = jnp.zeros((pos.shape[0], 128 - MLA_ROPE), F32)
    return jnp.concatenate([c, c, z], axis=1), jnp.concatenate([-s, s, z], axis=1)


PROMPT_TM_QKV = 1024
MLA_PROJ_TM = 256
MLA_PROJ_SUB = 256


def kernel(x_prompt, x_sample, cache_a_k, cache_a_v, cache_mla_ckv, cache_mla_kr, ln_mix_pre, ln_mix_post, ln_ffn_pre, ln_ffn_post, a_w_qkv, a_w_o, a_rel_bias, mla_w_dq, mla_q_norm, mla_w_uq, mla_w_dkv, mla_kv_norm, mla_w_uk, mla_w_uv, mla_w_o, ffn_w1, ffn_w2):
    xp = x_prompt.reshape(T_PROMPT, D_MODEL)
    xs = x_sample.reshape(T_SAMPLE, D_MODEL)
    w1 = ffn_w1.astype(BF16)
    w2 = ffn_w2.astype(BF16)

    def mix_out_and_ffn(i, o, w_o, x):
        x, h = proj_residual(o, w_o, ln_mix_post[i], ln_ffn_pre[i], x, tm=512)
        return ffn(x, h, w1, w2, ln_ffn_post[i], layer=i, tm=512, tf=1024)

    w_qkv = a_w_qkv[0].astype(BF16)
    w_o = a_w_o[0].astype(BF16)
    f_rows = _bias_rows(a_rel_bias[0])
    hm_p, k_p, v_p = norm_qkv(xp, ln_mix_pre[0], w_qkv, tm=PROMPT_TM_QKV, keep=BAND_ROWS,
                              tiles_per_keep=SEQ // PROMPT_TM_QKV)
    hm_s, k_s, v_s = norm_qkv(xs, ln_mix_pre[0], w_qkv, tm=512, keep=512, tiles_per_keep=1)
    o_p = band_prompt(hm_p, f_rows)
    o_s = band_sample(hm_s,
                      cache_a_k[0].reshape(DEC_BATCH, BAND_ROWS * HEADS, HEAD_DIM),
                      cache_a_v[0].reshape(DEC_BATCH, BAND_ROWS * HEADS, HEAD_DIM), f_rows)
    xp = mix_out_and_ffn(0, o_p, w_o, xp)
    xs = mix_out_and_ffn(0, o_s, w_o, xs)

    pad = jnp.zeros((D_MODEL, 128 - MLA_ROPE), F32)
    w_d = jnp.concatenate([mla_w_dq[0], mla_w_dkv[0], pad], axis=1).astype(BF16)
    wq = jnp.pad(mla_w_uq[0], ((0, 0), (0, 0), (0, MLA_QK - MLA_NOPE - MLA_ROPE)))
    wq = wq.reshape(MLA_RANK, HEADS * MLA_QK).astype(BF16)
    wuk = mla_w_uk[0].reshape(MLA_RANK, HEADS * MLA_NOPE).astype(BF16)
    wuv = mla_w_uv[0].reshape(MLA_RANK, HEADS * HEAD_DIM).astype(BF16)
    w_o = mla_w_o[0].astype(BF16)
    cos_p, sin_p = _rope_tables(jnp.arange(SEQ))
    cos_s, sin_s = _rope_tables(jnp.tile(PAST_LEN + jnp.arange(DEC_SEQ), MLA_PROJ_TM // DEC_SEQ))

    ckv_p, kr_p, q_p, kk_p, vv_p = mla_proj(xp, ln_mix_pre[1], w_d, mla_q_norm[0], mla_kv_norm[0],
                                            wq, wuk, wuv, cos_p, sin_p, tm=MLA_PROJ_TM, with_kv=True)
    ckv_s, kr_s, q_s = mla_proj(xs, ln_mix_pre[1], w_d, mla_q_norm[0], mla_kv_norm[0],
                                wq, wuk, wuv, cos_s, sin_s, tm=MLA_PROJ_TM, with_kv=False)
    o_p = mla_prompt(q_p, kk_p, vv_p)
    o_s = mla_sample(q_s, ckv_s, kr_s, cache_mla_ckv[0], cache_mla_kr[0], wuk, wuv)
    xp = mix_out_and_ffn(1, o_p, w_o, xp)
    xs = mix_out_and_ffn(1, o_s, w_o, xs)

    return (
        xp.reshape(BATCH, SEQ, D_MODEL),
        xs.reshape(DEC_BATCH, DEC_SEQ, D_MODEL),
        k_p.reshape(1, BATCH, BAND_ROWS, HEADS, HEAD_DIM),
        v_p.reshape(1, BATCH, BAND_ROWS, HEADS, HEAD_DIM),
        k_s.reshape(1, DEC_BATCH, DEC_SEQ, HEADS, HEAD_DIM),
        v_s.reshape(1, DEC_BATCH, DEC_SEQ, HEADS, HEAD_DIM),
        ckv_p.reshape(1, BATCH, SEQ, MLA_RANK),
        kr_p.reshape(1, BATCH, SEQ, MLA_ROPE),
        ckv_s.reshape(1, DEC_BATCH, DEC_SEQ, MLA_RANK),
        kr_s.reshape(1, DEC_BATCH, DEC_SEQ, MLA_ROPE),
    )
```

```python
import functools
import math

import jax
import jax.numpy as jnp
from jax import lax
from jax.experimental import pallas as pl
from jax.experimental.pallas import tpu as pltpu

F32 = jnp.float32
BF16 = jnp.bfloat16

D_MODEL = 2048
BATCH = 4
SEQ = 4096
DEC_BATCH = 16
DEC_SEQ = 64
PAST_LEN = 2048
CHUNK = 64
HEADS = 16
HEAD_DIM = 128
BAND_ROWS = 512
REL_CLIP = 256
MLA_RANK = 512
MLA_NOPE = 128
MLA_ROPE = 64
MLA_QK = 256
ROPE_THETA = 10000.0
D_FF = 8192
EPS = 1e-6
NEG = -1e30
LOG2E = math.log2(math.e)

T_PROMPT = BATCH * SEQ
T_SAMPLE = DEC_BATCH * DEC_SEQ

VMEM_LIMIT_BYTES = 58 * 1024 * 1024
FFN_VMEM_LIMIT_BYTES = 62 * 1024 * 1024

BAND_TQ = 256
BAND_TK = BAND_ROWS + BAND_TQ
BAND_F = 1024
BAND_NQ = SEQ // BAND_TQ
BAND_SUB = 32
BAND_Q_SCALE = HEAD_DIM ** -0.5 * LOG2E

MLA_TQ = 512
MLA_TK = 512
MLA_TK_MAX = 1024
MLA_NQ = SEQ // MLA_TQ
MLA_SUB = 32
MLA_HALF = MLA_TQ // 2
MLA_Q_SCALE = (MLA_NOPE + MLA_ROPE) ** -0.5 * LOG2E

MLA_SAMPLE_HEADS = 16
QKV_TN = 1024
QKV_CHUNK = 256
QKV_ROWS = 256
PROJ_SUB = 128
FFN_CHUNK = 512

_NT = (((1,), (1,)), ((), ()))


def _params(*sem):
    return pltpu.CompilerParams(dimension_semantics=sem, vmem_limit_bytes=VMEM_LIMIT_BYTES)


def _rms(x, g):
    ms = jnp.mean(x * x, axis=-1, keepdims=True)
    return x * lax.rsqrt(ms + EPS) * g


def _norm_qkv_kernel(x_ref, g_ref, w_ref, hm_ref, kf_ref, vf_ref, h_ref, *, keep):
    j = pl.program_id(1)
    tm = x_ref.shape[0]
    heads_per_chunk = QKV_CHUNK // HEAD_DIM

    def project_first():
        g = g_ref[...]
        for u in range(tm // QKV_ROWS):
            rows = slice(u * QKV_ROWS, (u + 1) * QKV_ROWS)
            hu = _rms(x_ref[rows, :], g).astype(BF16)
            h_ref[rows, :] = hu
            kept = u * QKV_ROWS - (tm - keep)
            for c in range(QKV_TN // QKV_CHUNK):
                cols = slice(c * QKV_CHUNK, (c + 1) * QKV_CHUNK)
                acc = jnp.dot(hu, w_ref[:, cols], preferred_element_type=F32)
                if kept >= 0:
                    kf_ref[kept:kept + QKV_ROWS, cols] = acc
                for hh in range(heads_per_chunk):
                    hm_ref[c * heads_per_chunk + hh, rows, :] = (
                        acc[:, hh * HEAD_DIM:(hh + 1) * HEAD_DIM].astype(BF16))

    def project(scale, f_ref, f_col):
        h = h_ref[...]
        for c in range(QKV_TN // QKV_CHUNK):
            cols = slice(c * QKV_CHUNK, (c + 1) * QKV_CHUNK)
            acc = jnp.dot(h, w_ref[:, cols], preferred_element_type=F32)
            if f_ref is not None:
                f_ref[:, f_col + c * QKV_CHUNK:f_col + (c + 1) * QKV_CHUNK] = acc[tm - keep:, :]
            if scale is not None:
                acc = acc * scale
            for hh in range(heads_per_chunk):
                hm_ref[c * heads_per_chunk + hh] = (
                    acc[:, hh * HEAD_DIM:(hh + 1) * HEAD_DIM].astype(BF16))

    pl.when(j == 0)(project_first)
    for jj, f_ref in ((1, kf_ref), (2, vf_ref), (3, vf_ref)):
        pl.when(j == jj)(functools.partial(project, None, f_ref, (jj % 2) * QKV_TN))

    pl.when(j >= 4)(functools.partial(project, BAND_Q_SCALE, None, 0))


def norm_qkv(x, g, w, *, tm, keep, tiles_per_keep):
    m, k = x.shape
    nb = 3 * D_MODEL // QKV_TN
    heads_per_block = QKV_TN // HEAD_DIM
    n_keep = m // (tm * tiles_per_keep) * keep

    def col(j):
        return (j + 2) % nb

    return pl.pallas_call(
        functools.partial(_norm_qkv_kernel, keep=keep),
        grid=(m // tm, nb),
        in_specs=[
            pl.BlockSpec((tm, k), lambda i, j: (i, 0)),
            pl.BlockSpec((1, k), lambda i, j: (0, 0)),
            pl.BlockSpec((k, QKV_TN), lambda i, j: (0, col(j))),
        ],
        out_specs=[
            pl.BlockSpec((heads_per_block, tm, HEAD_DIM), lambda i, j: (col(j), i, 0)),
            pl.BlockSpec((keep, D_MODEL), lambda i, j: (i // tiles_per_keep, 0)),
            pl.BlockSpec((keep, D_MODEL), lambda i, j: (i // tiles_per_keep, 0)),
        ],
        out_shape=[
            jax.ShapeDtypeStruct((3 * HEADS, m, HEAD_DIM), BF16),
            jax.ShapeDtypeStruct((n_keep, D_MODEL), F32),
            jax.ShapeDtypeStruct((n_keep, D_MODEL), F32),
        ],
        scratch_shapes=[pltpu.VMEM((tm, k), BF16)],
        compiler_params=_params("arbitrary", "arbitrary"),
        name="norm_qkv",
    )(x, g.reshape(1, k), w)


def _toeplitz_bias(f_row, rows):
    x = jnp.broadcast_to(f_row, (rows, BAND_F))
    return pltpu.roll(x, 0, 1, stride=1, stride_axis=0)


def _band_prompt_kernel(q_ref, k_ref, v_ref, f_ref, w1f_ref, w2f_ref, o_ref, w1b_ref, w2b_ref,
                        vcat_ref, bias_ref, s_ref, s2_ref, p_ref, p2_ref):
    w1b_ref[...] = w1f_ref[...].astype(BF16)
    w2b_ref[...] = w2f_ref[...].astype(BF16)

    @pl.when((pl.program_id(0) == 0) & (pl.program_id(1) == 0))
    def _():
        vcat_ref[:, HEAD_DIM:] = jnp.ones((SEQ, HEAD_DIM), BF16)

    @pl.when(pl.program_id(1) == 0)
    def _():
        r = lax.broadcasted_iota(jnp.int32, (BAND_TQ, BAND_TK), 0) // CHUNK
        jc = lax.broadcasted_iota(jnp.int32, (BAND_TQ, BAND_TK), 1) // CHUNK
        allowed = (jc >= r) & (jc <= r + BAND_ROWS // CHUNK)
        bias_ref[...] = jnp.where(allowed, _toeplitz_bias(f_ref[0], BAND_TQ)[:, :BAND_TK], NEG)

    vcat_ref[:, 0:HEAD_DIM] = v_ref[0]

    def window(qi):
        end = (qi + 1) * BAND_TQ
        start = max(end - BAND_TK, 0)
        return start, end - start

    def scores(qi, dst_ref):
        k0, w = window(qi)
        dst_ref[:, :w] = lax.dot_general(q_ref[0, qi * BAND_TQ:(qi + 1) * BAND_TQ, :],
                                         k_ref[0, k0:k0 + w, :], _NT, preferred_element_type=F32)

    def softmax(qi, cur_ref, p_dst):
        _, w = window(qi)
        for u in range(BAND_TQ // BAND_SUB):
            rows = slice(u * BAND_SUB, (u + 1) * BAND_SUB)
            s = cur_ref[rows, :w] + bias_ref[rows, BAND_TK - w:]
            p_dst[rows, :w] = jnp.exp2(s - jnp.max(s, axis=-1, keepdims=True)).astype(BF16)

    def values(qi, p_src):
        k0, w = window(qi)
        pv = jnp.dot(p_src[:, :w], vcat_ref[k0:k0 + w, :], preferred_element_type=F32)
        o_ref[qi * BAND_TQ:(qi + 1) * BAND_TQ, :] = (
            pv[:, :HEAD_DIM] / pv[:, HEAD_DIM:]).astype(BF16)

    s_bufs = (s_ref, s2_ref)
    p_bufs = (p_ref, p2_ref)
    scores(0, s_bufs[0])
    scores(1, s_bufs[1])
    softmax(0, s_bufs[0], p_bufs[0])
    for t in range(BAND_NQ):
        if t + 2 < BAND_NQ:
            scores(t + 2, s_bufs[t % 2])
        if t + 1 < BAND_NQ:
            softmax(t + 1, s_bufs[(t + 1) % 2], p_bufs[(t + 1) % 2])
        values(t, p_bufs[t % 2])


def _weight_cast_specs(w1, w2, layer, step):
    n = HEADS * BATCH
    _, d, f = w1.shape
    in_specs = [
        pl.BlockSpec((None, d // n, f), lambda *g: (layer, step(*g), 0)),
        pl.BlockSpec((None, f // n, d), lambda *g: (layer, step(*g), 0)),
    ]
    out_specs = [
        pl.BlockSpec((d // n, f), lambda *g: (step(*g), 0)),
        pl.BlockSpec((f // n, d), lambda *g: (step(*g), 0)),
    ]
    out_shape = [jax.ShapeDtypeStruct((d, f), BF16), jax.ShapeDtypeStruct((f, d), BF16)]
    return in_specs, out_specs, out_shape


def band_prompt(qkv_hm, f_rows, w1, w2, layer):
    w_in, w_out, w_shape = _weight_cast_specs(w1, w2, layer, lambda h, b: h * BATCH + b)
    return pl.pallas_call(
        _band_prompt_kernel,
        grid=(HEADS, BATCH),
        in_specs=[
            pl.BlockSpec((1, SEQ, HEAD_DIM), lambda h, b: (h, b, 0)),
            pl.BlockSpec((1, SEQ, HEAD_DIM), lambda h, b: (HEADS + h, b, 0)),
            pl.BlockSpec((1, SEQ, HEAD_DIM), lambda h, b: (2 * HEADS + h, b, 0)),
            pl.BlockSpec((1, 1, BAND_F), lambda h, b: (h, 0, 0)),
        ] + w_in,
        out_specs=[pl.BlockSpec((SEQ, HEAD_DIM), lambda h, b: (b, h))] + w_out,
        out_shape=[jax.ShapeDtypeStruct((T_PROMPT, D_MODEL), BF16)] + w_shape,
        scratch_shapes=[
            pltpu.VMEM((SEQ, 2 * HEAD_DIM), BF16),
            pltpu.VMEM((BAND_TQ, BAND_TK), F32),
            pltpu.VMEM((BAND_TQ, BAND_TK), F32),
            pltpu.VMEM((BAND_TQ, BAND_TK), F32),
            pltpu.VMEM((BAND_TQ, BAND_TK), BF16),
            pltpu.VMEM((BAND_TQ, BAND_TK), BF16),
        ],
        compiler_params=_params("arbitrary", "arbitrary"),
        name="band_prompt",
    )(qkv_hm, qkv_hm, qkv_hm, f_rows, w1, w2)


def _band_sample_kernel(q_ref, kn_ref, vn_ref, ck_ref, cv_ref, f_ref, o_ref):
    for h in range(HEADS):
        cols = slice(h * HEAD_DIM, (h + 1) * HEAD_DIM)
        bias = _toeplitz_bias(f_ref[h], DEC_SEQ)
        q = q_ref[h]
        kc = ck_ref[0, pl.ds(h, BAND_ROWS, stride=HEADS), :].astype(BF16)
        vc = cv_ref[0, pl.ds(h, BAND_ROWS, stride=HEADS), :].astype(BF16)
        s_c = lax.dot_general(q, kc, _NT, preferred_element_type=F32) + bias[:, :BAND_ROWS]
        s_n = (lax.dot_general(q, kn_ref[h], _NT, preferred_element_type=F32)
               + bias[:, BAND_ROWS:BAND_ROWS + DEC_SEQ])
        m = jnp.maximum(jnp.max(s_c, axis=-1, keepdims=True), jnp.max(s_n, axis=-1, keepdims=True))
        p_c = jnp.exp2(s_c - m)
        p_n = jnp.exp2(s_n - m)
        l = jnp.sum(p_c, axis=-1, keepdims=True) + jnp.sum(p_n, axis=-1, keepdims=True)
        o = (jnp.dot(p_c.astype(BF16), vc, preferred_element_type=F32)
             + jnp.dot(p_n.astype(BF16), vn_ref[h], preferred_element_type=F32)) / l
        o_ref[:, cols] = o.astype(BF16)


def band_sample(qkv_hm, cache_k, cache_v, f_rows):
    return pl.pallas_call(
        _band_sample_kernel,
        grid=(DEC_BATCH,),
        in_specs=[
            pl.BlockSpec((HEADS, DEC_SEQ, HEAD_DIM), lambda s: (0, s, 0)),
            pl.BlockSpec((HEADS, DEC_SEQ, HEAD_DIM), lambda s: (1, s, 0)),
            pl.BlockSpec((HEADS, DEC_SEQ, HEAD_DIM), lambda s: (2, s, 0)),
            pl.BlockSpec((1, BAND_ROWS * HEADS, HEAD_DIM), lambda s: (s, 0, 0)),
            pl.BlockSpec((1, BAND_ROWS * HEADS, HEAD_DIM), lambda s: (s, 0, 0)),
            pl.BlockSpec((HEADS, 1, BAND_F), lambda s: (0, 0, 0)),
        ],
        out_specs=pl.BlockSpec((DEC_SEQ, D_MODEL), lambda s: (s, 0)),
        out_shape=jax.ShapeDtypeStruct((T_SAMPLE, D_MODEL), BF16),
        compiler_params=_params("arbitrary"),
        name="band_sample",
    )(qkv_hm, qkv_hm, qkv_hm, cache_k, cache_v, f_rows)


def _proj_residual_kernel(a_ref, w_ref, g_ref, gn_ref, x_ref, o_ref, h_ref):
    w = w_ref[...]
    g = g_ref[...]
    gn = gn_ref[...]
    for u in range(a_ref.shape[0] // PROJ_SUB):
        rows = slice(u * PROJ_SUB, (u + 1) * PROJ_SUB)
        y = jnp.dot(a_ref[rows, :], w, preferred_element_type=F32)
        x1 = x_ref[rows, :] + _rms(y, g)
        o_ref[rows, :] = x1
        h_ref[rows, :] = _rms(x1, gn).astype(BF16)


def proj_residual(a, w, g, g_next, x, *, tm):
    m, k = a.shape
    n = w.shape[1]
    return pl.pallas_call(
        _proj_residual_kernel,
        grid=(m // tm,),
        in_specs=[
            pl.BlockSpec((tm, k), lambda i: (i, 0)),
            pl.BlockSpec((k, n), lambda i: (0, 0)),
            pl.BlockSpec((1, n), lambda i: (0, 0)),
            pl.BlockSpec((1, n), lambda i: (0, 0)),
            pl.BlockSpec((tm, n), lambda i: (i, 0)),
        ],
        out_specs=[
            pl.BlockSpec((tm, n), lambda i: (i, 0)),
            pl.BlockSpec((tm, n), lambda i: (i, 0)),
        ],
        out_shape=[
            jax.ShapeDtypeStruct((m, n), F32),
            jax.ShapeDtypeStruct((m, n), BF16),
        ],
        compiler_params=_params("arbitrary"),
        name="proj_residual",
    )(a, w, g.reshape(1, n), g_next.reshape(1, n), x)


def _ffn_kernel(x_ref, h_ref, w1a_ref, w1b_ref, w2a_ref, w2b_ref, g2_ref, o_ref, ssq_ref):
    j = pl.program_id(1)

    @pl.when(j == 0)
    def _():
        o_ref[...] = jnp.zeros(o_ref.shape, F32)

    def hidden(w1_ref):
        a = jnp.maximum(jnp.dot(h_ref[...], w1_ref[...], preferred_element_type=F32), 0.0)
        return (a * a).astype(BF16)

    aa = hidden(w1a_ref)
    ab = hidden(w1b_ref)
    for c in range(o_ref.shape[1] // FFN_CHUNK):
        cols = slice(c * FFN_CHUNK, (c + 1) * FFN_CHUNK)
        y = (o_ref[:, cols] + jnp.dot(aa, w2a_ref[:, cols], preferred_element_type=F32)
             + jnp.dot(ab, w2b_ref[:, cols], preferred_element_type=F32))
        o_ref[:, cols] = y
        sq = y * y
        ssq_ref[:, c * 128:(c + 1) * 128] = functools.reduce(
            jnp.add, [sq[:, b * 128:(b + 1) * 128] for b in range(FFN_CHUNK // 128)])

    @pl.when(j == pl.num_programs(1) - 1)
    def _():
        ms = jnp.sum(ssq_ref[...], axis=-1, keepdims=True) * (1.0 / o_ref.shape[1])
        o_ref[...] = x_ref[...] + o_ref[...] * lax.rsqrt(ms + EPS) * g2_ref[...]


def ffn(x, h, w1, w2, g2, *, tm, tf):
    m, d = x.shape
    f = w1.shape[1]
    return pl.pallas_call(
        _ffn_kernel,
        grid=(m // tm, f // (2 * tf)),
        in_specs=[
            pl.BlockSpec((tm, d), lambda i, j: (i, 0)),
            pl.BlockSpec((tm, d), lambda i, j: (i, 0)),
            pl.BlockSpec((d, tf), lambda i, j: (0, 2 * j)),
            pl.BlockSpec((d, tf), lambda i, j: (0, 2 * j + 1)),
            pl.BlockSpec((tf, d), lambda i, j: (2 * j, 0)),
            pl.BlockSpec((tf, d), lambda i, j: (2 * j + 1, 0)),
            pl.BlockSpec((1, d), lambda i, j: (0, 0)),
        ],
        out_specs=pl.BlockSpec((tm, d), lambda i, j: (i, 0)),
        out_shape=jax.ShapeDtypeStruct((m, d), F32),
        scratch_shapes=[pltpu.VMEM((tm, 128 * (d // FFN_CHUNK)), F32)],
        compiler_params=pltpu.CompilerParams(dimension_semantics=("arbitrary", "arbitrary"),
                                             vmem_limit_bytes=FFN_VMEM_LIMIT_BYTES),
        name="ffn",
    )(x, h, w1, w1, w2, w2, g2.reshape(1, d))


def _rope128(x, cos, sin):
    lane = lax.broadcasted_iota(jnp.int32, x.shape, 1)
    half = MLA_ROPE // 2
    swapped = jnp.where(lane < half, pltpu.roll(x, 128 - half, 1), pltpu.roll(x, half, 1))
    return x * cos + swapped * sin


def _mla_proj_kernel(x_ref, g_ref, wd_ref, qn_ref, kvn_ref, wq_ref, wuk_ref, wuv_ref, cos_ref, sin_ref,
                     ckv_ref, kr_ref, q_ref, *kv_refs):
    for u in range(x_ref.shape[0] // MLA_PROJ_SUB):
        rows = slice(u * MLA_PROJ_SUB, (u + 1) * MLA_PROJ_SUB)
        d = jnp.dot(_rms(x_ref[rows, :], g_ref[...]).astype(BF16), wd_ref[...],
                    preferred_element_type=F32)
        cos = cos_ref[rows, :]
        sin = sin_ref[rows, :]
        cq = _rms(d[:, :MLA_RANK], qn_ref[...]).astype(BF16)
        ckv = _rms(d[:, MLA_RANK:2 * MLA_RANK], kvn_ref[...])
        ckv_ref[rows, :] = ckv
        kr = _rope128(d[:, 2 * MLA_RANK:], cos, sin)
        kr_ref[rows, :] = kr[:, :MLA_ROPE]

        q = jnp.dot(cq, wq_ref[...], preferred_element_type=F32) * MLA_Q_SCALE
        for h in range(HEADS):
            base = h * MLA_QK
            q_ref[h, rows, 0:MLA_NOPE] = q[:, base:base + MLA_NOPE].astype(BF16)
            q_ref[h, rows, MLA_NOPE:] = _rope128(
                q[:, base + MLA_NOPE:base + MLA_QK], cos, sin).astype(BF16)

        if kv_refs:
            k_ref, v_ref = kv_refs
            kr_b = kr.astype(BF16)
            ckv_b = ckv.astype(BF16)
            kn = jnp.dot(ckv_b, wuk_ref[...], preferred_element_type=F32)
            vv = jnp.dot(ckv_b, wuv_ref[...], preferred_element_type=F32)
            ones = jnp.ones((MLA_PROJ_SUB, HEAD_DIM), BF16)
            for h in range(HEADS):
                k_ref[h, rows, 0:MLA_NOPE] = kn[:, h * MLA_NOPE:(h + 1) * MLA_NOPE].astype(BF16)
                k_ref[h, rows, MLA_NOPE:] = kr_b
                v_ref[h, rows, 0:HEAD_DIM] = vv[:, h * HEAD_DIM:(h + 1) * HEAD_DIM].astype(BF16)
                v_ref[h, rows, HEAD_DIM:] = ones


def mla_proj(x, g, w_d, q_norm, kv_norm, wq, wuk, wuv, cos, sin, *, tm, with_kv):
    m = x.shape[0]
    full = lambda i: (0, 0)
    rope_tiles = cos.shape[0] // tm
    hm_spec = pl.BlockSpec((HEADS, tm, MLA_QK), lambda i: (0, i, 0))
    hm_shape = jax.ShapeDtypeStruct((HEADS, m, MLA_QK), BF16)
    n_hm = 3 if with_kv else 1
    return pl.pallas_call(
        _mla_proj_kernel,
        grid=(m // tm,),
        in_specs=[
            pl.BlockSpec((tm, D_MODEL), lambda i: (i, 0)),
            pl.BlockSpec((1, D_MODEL), full),
            pl.BlockSpec(w_d.shape, full),
            pl.BlockSpec((1, MLA_RANK), full),
            pl.BlockSpec((1, MLA_RANK), full),
            pl.BlockSpec(wq.shape, full),
            pl.BlockSpec(wuk.shape, full),
            pl.BlockSpec(wuv.shape, full),
            pl.BlockSpec((tm, 128), lambda i: (i % rope_tiles, 0)),
            pl.BlockSpec((tm, 128), lambda i: (i % rope_tiles, 0)),
        ],
        out_specs=[
            pl.BlockSpec((tm, MLA_RANK), lambda i: (i, 0)),
            pl.BlockSpec((tm, MLA_ROPE), lambda i: (i, 0)),
        ] + [hm_spec] * n_hm,
        out_shape=[
            jax.ShapeDtypeStruct((m, MLA_RANK), F32),
            jax.ShapeDtypeStruct((m, MLA_ROPE), F32),
        ] + [hm_shape] * n_hm,
        compiler_params=_params("arbitrary"),
        name="mla_proj",
    )(x, g.reshape(1, -1), w_d, q_norm.reshape(1, -1), kv_norm.reshape(1, -1), wq, wuk, wuv, cos, sin)


def _mla_prompt_kernel(q_ref, k_ref, v_ref, w1f_ref, w2f_ref, o_ref, w1b_ref, w2b_ref,
                       s_ref, s2_ref, p_ref, p2_ref, alpha_ref, alpha2_ref, bias_ref, m_ref, acc_ref):
    w1b_ref[...] = w1f_ref[...].astype(BF16)
    w2b_ref[...] = w2f_ref[...].astype(BF16)

    @pl.when((pl.program_id(0) == 0) & (pl.program_id(1) == 0))
    def _():
        rc = lax.broadcasted_iota(jnp.int32, (MLA_TQ, MLA_TK), 0) // CHUNK
        cc = lax.broadcasted_iota(jnp.int32, (MLA_TQ, MLA_TK), 1) // CHUNK
        bias_ref[...] = jnp.where(cc <= rc, 0.0, NEG)

    def is_last(qi, k0, klen):
        return k0 + klen == (qi + 1) * MLA_TQ

    def width(qi, k0, klen, row0):
        return klen - MLA_HALF if is_last(qi, k0, klen) and row0 < MLA_HALF else klen

    def scores(qi, k0, klen, dst_ref):
        for row0 in range(0, MLA_TQ, MLA_HALF):
            w = width(qi, k0, klen, row0)
            q = q_ref[0, qi * MLA_TQ + row0:qi * MLA_TQ + row0 + MLA_HALF, :]
            dst_ref[row0:row0 + MLA_HALF, :w] = lax.dot_general(
                q, k_ref[0, k0:k0 + w, :], _NT, preferred_element_type=F32)

    def softmax(qi, k0, klen, cur_ref, p_dst, alpha_dst):
        first = k0 == 0
        last = is_last(qi, k0, klen)
        for u in range(MLA_TQ // MLA_SUB):
            rows = slice(u * MLA_SUB, (u + 1) * MLA_SUB)
            if last:
                w = width(qi, k0, klen, u * MLA_SUB)
                d0 = klen - MLA_TK
                parts = [cur_ref[rows, d0:w] + bias_ref[rows, :w - d0]]
                if d0 > 0:
                    parts.insert(0, cur_ref[rows, :d0])
            else:
                parts = [cur_ref[rows, :klen]]
            m_cur = functools.reduce(jnp.maximum, [jnp.max(x, axis=-1, keepdims=True) for x in parts])
            if first:
                m_new = jnp.broadcast_to(m_cur, (MLA_SUB, 128))
            else:
                m_prev = m_ref[rows, :]
                m_new = jnp.maximum(m_prev, m_cur)
                alpha_dst[rows, :] = jnp.exp2(m_prev - m_new)
            if not last:
                m_ref[rows, :] = m_new
            col = 0
            for x in parts:
                p_dst[rows, col:col + x.shape[1]] = jnp.exp2(
                    x - jnp.tile(m_new, (1, x.shape[1] // 128))).astype(BF16)
                col += x.shape[1]

    def values(qi, k0, klen, p_src, alpha_src):
        first = k0 == 0
        last = is_last(qi, k0, klen)
        for u in range(MLA_TQ // MLA_HALF):
            rows = slice(u * MLA_HALF, (u + 1) * MLA_HALF)
            w = width(qi, k0, klen, u * MLA_HALF)
            acc = jnp.dot(p_src[rows, :w], v_ref[0, k0:k0 + w, :], preferred_element_type=F32)
            if not first:
                acc = acc + jnp.tile(alpha_src[rows, :], (1, 2)) * acc_ref[rows, :]
            if last:
                o_ref[qi * MLA_TQ + u * MLA_HALF:qi * MLA_TQ + (u + 1) * MLA_HALF, :] = (
                    acc[:, :HEAD_DIM] / acc[:, HEAD_DIM:]).astype(BF16)
            else:
                acc_ref[rows, :] = acc

    blocks = [(qi, k0, min(MLA_TK_MAX, (qi + 1) * MLA_TQ - k0))
              for qi in range(MLA_NQ) for k0 in range(0, (qi + 1) * MLA_TQ, MLA_TK_MAX)]
    n = len(blocks)
    s_bufs = (s_ref, s2_ref)
    p_bufs = (p_ref, p2_ref)
    a_bufs = (alpha_ref, alpha2_ref)
    scores(*blocks[0], s_bufs[0])
    scores(*blocks[1], s_bufs[1])
    softmax(*blocks[0], s_bufs[0], p_bufs[0], a_bufs[0])
    for t in range(n):
        if t + 2 < n:
            scores(*blocks[t + 2], s_bufs[t % 2])
        if t + 1 < n:
            softmax(*blocks[t + 1], s_bufs[(t + 1) % 2], p_bufs[(t + 1) % 2], a_bufs[(t + 1) % 2])
        values(*blocks[t], p_bufs[t % 2], a_bufs[t % 2])


def mla_prompt(q, k, v, w1, w2, layer):
    w_in, w_out, w_shape = _weight_cast_specs(w1, w2, layer, lambda b, h: b * HEADS + h)
    return pl.pallas_call(
        _mla_prompt_kernel,
        grid=(BATCH, HEADS),
        in_specs=[
            pl.BlockSpec((1, SEQ, MLA_QK), lambda b, h: (h, b, 0)),
            pl.BlockSpec((1, SEQ, MLA_QK), lambda b, h: (h, b, 0)),
            pl.BlockSpec((1, SEQ, 2 * HEAD_DIM), lambda b, h: (h, b, 0)),
        ] + w_in,
        out_specs=[pl.BlockSpec((SEQ, HEAD_DIM), lambda b, h: (b, h))] + w_out,
        out_shape=[jax.ShapeDtypeStruct((T_PROMPT, D_MODEL), BF16)] + w_shape,
        scratch_shapes=[
            pltpu.VMEM((MLA_TQ, MLA_TK_MAX), F32),
            pltpu.VMEM((MLA_TQ, MLA_TK_MAX), F32),
            pltpu.VMEM((MLA_TQ, MLA_TK_MAX), BF16),
            pltpu.VMEM((MLA_TQ, MLA_TK_MAX), BF16),
            pltpu.VMEM((MLA_TQ, 128), F32),
            pltpu.VMEM((MLA_TQ, 128), F32),
            pltpu.VMEM((MLA_TQ, MLA_TK), F32),
            pltpu.VMEM((MLA_TQ, 128), F32),
            pltpu.VMEM((MLA_TQ, 2 * HEAD_DIM), F32),
        ],
        compiler_params=_params("arbitrary", "arbitrary"),
        name="mla_prompt",
    )(q, k, v, w1, w2)


def _mla_sample_kernel(q_ref, cn_ref, rn_ref, cc_ref, cr_ref, wuk_ref, wuv_ref, o_ref,
                       ql_ref, qr_ref):
    cc = cc_ref[0].astype(BF16)
    cr = cr_ref[0].astype(BF16)
    cn = cn_ref[...].astype(BF16)
    rn = rn_ref[...].astype(BF16)
    for g in range(HEADS // MLA_SAMPLE_HEADS):
        heads = range(g * MLA_SAMPLE_HEADS, (g + 1) * MLA_SAMPLE_HEADS)
        grp = slice(g * MLA_SAMPLE_HEADS * DEC_SEQ, (g + 1) * MLA_SAMPLE_HEADS * DEC_SEQ)
        for h in heads:
            rows = slice(h * DEC_SEQ, (h + 1) * DEC_SEQ)
            ql_ref[rows, :] = lax.dot_general(
                q_ref[h, :, 0:MLA_NOPE], wuk_ref[:, h * MLA_NOPE:(h + 1) * MLA_NOPE], _NT,
                preferred_element_type=F32).astype(BF16)
            qr_ref[rows, :] = q_ref[h, :, MLA_NOPE:]
        ql = ql_ref[grp, :]
        qr = qr_ref[grp, :][:, :MLA_ROPE]
        s_c = (lax.dot_general(ql, cc, _NT, preferred_element_type=F32)
               + lax.dot_general(qr, cr, _NT, preferred_element_type=F32))
        s_n = (lax.dot_general(ql, cn, _NT, preferred_element_type=F32)
               + lax.dot_general(qr, rn, _NT, preferred_element_type=F32))
        m = jnp.maximum(jnp.max(s_c, axis=-1, keepdims=True), jnp.max(s_n, axis=-1, keepdims=True))
        p_c = jnp.exp2(s_c - m)
        p_n = jnp.exp2(s_n - m)
        l = jnp.sum(p_c, axis=-1, keepdims=True) + jnp.sum(p_n, axis=-1, keepdims=True)
        o_lat = ((jnp.dot(p_c.astype(BF16), cc, preferred_element_type=F32)
                  + jnp.dot(p_n.astype(BF16), cn, preferred_element_type=F32)) / l).astype(BF16)
        for i, h in enumerate(heads):
            cols = slice(h * HEAD_DIM, (h + 1) * HEAD_DIM)
            o_ref[:, cols] = jnp.dot(o_lat[i * DEC_SEQ:(i + 1) * DEC_SEQ], wuv_ref[:, cols],
                                     preferred_element_type=F32).astype(BF16)


def mla_sample(q, ckv, kr, cache_ckv, cache_kr, wuk, wuv):
    full = lambda s: (0, 0)
    return pl.pallas_call(
        _mla_sample_kernel,
        grid=(DEC_BATCH,),
        in_specs=[
            pl.BlockSpec((HEADS, DEC_SEQ, MLA_QK), lambda s: (0, s, 0)),
            pl.BlockSpec((DEC_SEQ, MLA_RANK), lambda s: (s, 0)),
            pl.BlockSpec((DEC_SEQ, MLA_ROPE), lambda s: (s, 0)),
            pl.BlockSpec((1, PAST_LEN, MLA_RANK), lambda s: (s, 0, 0)),
            pl.BlockSpec((1, PAST_LEN, MLA_ROPE), lambda s: (s, 0, 0)),
            pl.BlockSpec(wuk.shape, full),
            pl.BlockSpec(wuv.shape, full),
        ],
        out_specs=pl.BlockSpec((DEC_SEQ, D_MODEL), lambda s: (s, 0)),
        out_shape=jax.ShapeDtypeStruct((T_SAMPLE, D_MODEL), BF16),
        scratch_shapes=[
            pltpu.VMEM((HEADS * DEC_SEQ, MLA_RANK), BF16),
            pltpu.VMEM((HEADS * DEC_SEQ, 128), BF16),
        ],
        compiler_params=_params("arbitrary"),
        name="mla_sample",
    )(q, ckv, kr, cache_ckv, cache_kr, wuk, wuv)


def _bias_rows(rel_bias):
    far = jnp.broadcast_to(rel_bias[:, 2 * REL_CLIP:], (HEADS, BAND_F // 4))
    mid = rel_bias[:, 1:][:, ::-1]
    return (jnp.concatenate([far, mid, far], axis=1) * LOG2E).reshape(HEADS, 1, BAND_F)


def _rope_tables(pos):
    half = MLA_ROPE // 2
    inv = 1.0 / (ROPE_THETA ** (jnp.arange(half, dtype=F32) / half))
    ang = pos.astype(F32)[:, None] * inv[None, :]
    c, s = jnp.cos(ang), jnp.sin(ang)
    z = jnp.zeros((pos.shape[0], 128 - MLA_ROPE), F32)
    return jnp.concatenate([c, c, z], axis=1), jnp.concatenate([-s, s, z], axis=1)


PROMPT_TM_QKV = 1024
MLA_PROJ_TM = 256
MLA_PROJ_SUB = 256


def kernel(x_prompt, x_sample, cache_a_k, cache_a_v, cache_mla_ckv, cache_mla_kr, ln_mix_pre, ln_mix_post, ln_ffn_pre, ln_ffn_post, a_w_qkv, a_w_o, a_rel_bias, mla_w_dq, mla_q_norm, mla_w_uq, mla_w_dkv, mla_kv_norm, mla_w_uk, mla_w_uv, mla_w_o, ffn_w1, ffn_w2):
    xp = x_prompt.reshape(T_PROMPT, D_MODEL)
    xs = x_sample.reshape(T_SAMPLE, D_MODEL)
    def mix_out_and_ffn(i, o, w_o, x, w1, w2):
        x, h = proj_residual(o, w_o, ln_mix_post[i], ln_ffn_pre[i], x, tm=512)
        return ffn(x, h, w1, w2, ln_ffn_post[i], tm=512, tf=1024)

    w_qkv = a_w_qkv[0].astype(BF16)
    w_o = a_w_o[0].astype(BF16)
    f_rows = _bias_rows(a_rel_bias[0])
    hm_p, k_p, v_p = norm_qkv(xp, ln_mix_pre[0], w_qkv, tm=PROMPT_TM_QKV, keep=BAND_ROWS,
                              tiles_per_keep=SEQ // PROMPT_TM_QKV)
    hm_s, k_s, v_s = norm_qkv(xs, ln_mix_pre[0], w_qkv, tm=512, keep=512, tiles_per_keep=1)
    o_p, w1, w2 = band_prompt(hm_p, f_rows, ffn_w1, ffn_w2, 0)
    o_s = band_sample(hm_s,
                      cache_a_k[0].reshape(DEC_BATCH, BAND_ROWS * HEADS, HEAD_DIM),
                      cache_a_v[0].reshape(DEC_BATCH, BAND_ROWS * HEADS, HEAD_DIM), f_rows)
    xp = mix_out_and_ffn(0, o_p, w_o, xp, w1, w2)
    xs = mix_out_and_ffn(0, o_s, w_o, xs, w1, w2)

    pad = jnp.zeros((D_MODEL, 128 - MLA_ROPE), F32)
    w_d = jnp.concatenate([mla_w_dq[0], mla_w_dkv[0], pad], axis=1).astype(BF16)
    wq = jnp.pad(mla_w_uq[0], ((0, 0), (0, 0), (0, MLA_QK - MLA_NOPE - MLA_ROPE)))
    wq = wq.reshape(MLA_RANK, HEADS * MLA_QK).astype(BF16)
    wuk = mla_w_uk[0].reshape(MLA_RANK, HEADS * MLA_NOPE).astype(BF16)
    wuv = mla_w_uv[0].reshape(MLA_RANK, HEADS * HEAD_DIM).astype(BF16)
    w_o = mla_w_o[0].astype(BF16)
    cos_p, sin_p = _rope_tables(jnp.arange(SEQ))
    cos_s, sin_s = _rope_tables(jnp.tile(PAST_LEN + jnp.arange(DEC_SEQ), MLA_PROJ_TM // DEC_SEQ))

    ckv_p, kr_p, q_p, kk_p, vv_p = mla_proj(xp, ln_mix_pre[1], w_d, mla_q_norm[0], mla_kv_norm[0],
                                            wq, wuk, wuv, cos_p, sin_p, tm=MLA_PROJ_TM, with_kv=True)
    ckv_s, kr_s, q_s = mla_proj(xs, ln_mix_pre[1], w_d, mla_q_norm[0], mla_kv_norm[0],
                                wq, wuk, wuv, cos_s, sin_s, tm=MLA_PROJ_TM, with_kv=False)
    o_p, w1, w2 = mla_prompt(q_p, kk_p, vv_p, ffn_w1, ffn_w2, 1)
    o_s = mla_sample(q_s, ckv_s, kr_s, cache_mla_ckv[0], cache_mla_kr[0], wuk, wuv)
    xp = mix_out_and_ffn(1, o_p, w_o, xp, w1, w2)
    xs = mix_out_and_ffn(1, o_s, w_o, xs, w1, w2)

    return (
        xp.reshape(BATCH, SEQ, D_MODEL),
        xs.reshape(DEC_BATCH, DEC_SEQ, D_MODEL),
        k_p.reshape(1, BATCH, BAND_ROWS, HEADS, HEAD_DIM),
        v_p.reshape(1, BATCH, BAND_ROWS, HEADS, HEAD_DIM),
        k_s.reshape(1, DEC_BATCH, DEC_SEQ, HEADS, HEAD_DIM),
        v_s.reshape(1, DEC_BATCH, DEC_SEQ, HEADS, HEAD_DIM),
        ckv_p.reshape(1, BATCH, SEQ, MLA_RANK),
        kr_p.reshape(1, BATCH, SEQ, MLA_ROPE),
        ckv_s.reshape(1, DEC_BATCH, DEC_SEQ, MLA_RANK),
        kr_s.reshape(1, DEC_BATCH, DEC_SEQ, MLA_ROPE),
    )
```

```python
import functools
import math

import jax
import jax.numpy as jnp
from jax import lax
from jax.experimental import pallas as pl
from jax.experimental.pallas import tpu as pltpu

F32 = jnp.float32
BF16 = jnp.bfloat16

D_MODEL = 2048
BATCH = 4
SEQ = 4096
DEC_BATCH = 16
DEC_SEQ = 64
PAST_LEN = 2048
CHUNK = 64
HEADS = 16
HEAD_DIM = 128
BAND_ROWS = 512
REL_CLIP = 256
MLA_RANK = 512
MLA_NOPE = 128
MLA_ROPE = 64
MLA_QK = 256
ROPE_THETA = 10000.0
D_FF = 8192
EPS = 1e-6
NEG = -1e30
LOG2E = math.log2(math.e)

T_PROMPT = BATCH * SEQ
T_SAMPLE = DEC_BATCH * DEC_SEQ

VMEM_LIMIT_BYTES = 58 * 1024 * 1024
FFN_VMEM_LIMIT_BYTES = 62 * 1024 * 1024

BAND_TQ = 256
BAND_TK = BAND_ROWS + BAND_TQ
BAND_F = 1024
BAND_NQ = SEQ // BAND_TQ
BAND_SUB = 32
BAND_Q_SCALE = HEAD_DIM ** -0.5 * LOG2E

MLA_TQ = 512
MLA_TK = 512
MLA_TK_MAX = 1024
MLA_NQ = SEQ // MLA_TQ
MLA_SUB = 32
MLA_HALF = MLA_TQ // 2
MLA_Q_SCALE = (MLA_NOPE + MLA_ROPE) ** -0.5 * LOG2E

MLA_SAMPLE_HEADS = 16
QKV_TN = 1024
QKV_CHUNK = 256
QKV_ROWS = 256
PROJ_SUB = 128
FFN_CHUNK = 512

_NT = (((1,), (1,)), ((), ()))


def _params(*sem):
    return pltpu.CompilerParams(dimension_semantics=sem, vmem_limit_bytes=VMEM_LIMIT_BYTES)


def _rms(x, g):
    ms = jnp.mean(x * x, axis=-1, keepdims=True)
    return x * lax.rsqrt(ms + EPS) * g


def _norm_qkv_kernel(x_ref, g_ref, w_ref, hm_ref, kf_ref, vf_ref, h_ref, *, keep):
    j = pl.program_id(1)
    tm = x_ref.shape[0]
    heads_per_chunk = QKV_CHUNK // HEAD_DIM

    def project_first():
        g = g_ref[...]
        for u in range(tm // QKV_ROWS):
            rows = slice(u * QKV_ROWS, (u + 1) * QKV_ROWS)
            hu = _rms(x_ref[rows, :], g).astype(BF16)
            h_ref[rows, :] = hu
            kept = u * QKV_ROWS - (tm - keep)
            for c in range(QKV_TN // QKV_CHUNK):
                cols = slice(c * QKV_CHUNK, (c + 1) * QKV_CHUNK)
                acc = jnp.dot(hu, w_ref[:, cols], preferred_element_type=F32)
                if kept >= 0:
                    kf_ref[kept:kept + QKV_ROWS, cols] = acc
                for hh in range(heads_per_chunk):
                    hm_ref[c * heads_per_chunk + hh, rows, :] = (
                        acc[:, hh * HEAD_DIM:(hh + 1) * HEAD_DIM].astype(BF16))

    def project(scale, f_ref, f_col):
        h = h_ref[...]
        for c in range(QKV_TN // QKV_CHUNK):
            cols = slice(c * QKV_CHUNK, (c + 1) * QKV_CHUNK)
            acc = jnp.dot(h, w_ref[:, cols], preferred_element_type=F32)
            if f_ref is not None:
                f_ref[:, f_col + c * QKV_CHUNK:f_col + (c + 1) * QKV_CHUNK] = acc[tm - keep:, :]
            if scale is not None:
                acc = acc * scale
            for hh in range(heads_per_chunk):
                hm_ref[c * heads_per_chunk + hh] = (
                    acc[:, hh * HEAD_DIM:(hh + 1) * HEAD_DIM].astype(BF16))

    pl.when(j == 0)(project_first)
    for jj, f_ref in ((1, kf_ref), (2, vf_ref), (3, vf_ref)):
        pl.when(j == jj)(functools.partial(project, None, f_ref, (jj % 2) * QKV_TN))

    pl.when(j >= 4)(functools.partial(project, BAND_Q_SCALE, None, 0))


def norm_qkv(x, g, w, *, tm, keep, tiles_per_keep):
    m, k = x.shape
    nb = 3 * D_MODEL // QKV_TN
    heads_per_block = QKV_TN // HEAD_DIM
    n_keep = m // (tm * tiles_per_keep) * keep

    def col(j):
        return (j + 2) % nb

    return pl.pallas_call(
        functools.partial(_norm_qkv_kernel, keep=keep),
        grid=(m // tm, nb),
        in_specs=[
            pl.BlockSpec((tm, k), lambda i, j: (i, 0)),
            pl.BlockSpec((1, k), lambda i, j: (0, 0)),
            pl.BlockSpec((k, QKV_TN), lambda i, j: (0, col(j))),
        ],
        out_specs=[
            pl.BlockSpec((heads_per_block, tm, HEAD_DIM), lambda i, j: (col(j), i, 0)),
            pl.BlockSpec((keep, D_MODEL), lambda i, j: (i // tiles_per_keep, 0)),
            pl.BlockSpec((keep, D_MODEL), lambda i, j: (i // tiles_per_keep, 0)),
        ],
        out_shape=[
            jax.ShapeDtypeStruct((3 * HEADS, m, HEAD_DIM), BF16),
            jax.ShapeDtypeStruct((n_keep, D_MODEL), F32),
            jax.ShapeDtypeStruct((n_keep, D_MODEL), F32),
        ],
        scratch_shapes=[pltpu.VMEM((tm, k), BF16)],
        compiler_params=_params("arbitrary", "arbitrary"),
        name="norm_qkv",
    )(x, g.reshape(1, k), w)


def _toeplitz_bias(f_row, rows):
    x = jnp.broadcast_to(f_row, (rows, BAND_F))
    return pltpu.roll(x, 0, 1, stride=1, stride_axis=0)


def _band_prompt_kernel(q_ref, k_ref, v_ref, f_ref, w1f_ref, w2f_ref, wof_ref,
                        o_ref, w1b_ref, w2b_ref, wob_ref,
                        vcat_ref, bias_ref, s_ref, s2_ref, p_ref, p2_ref):
    w1b_ref[...] = w1f_ref[...].astype(BF16)
    w2b_ref[...] = w2f_ref[...].astype(BF16)
    wob_ref[...] = wof_ref[...].astype(BF16)

    @pl.when((pl.program_id(0) == 0) & (pl.program_id(1) == 0))
    def _():
        vcat_ref[:, HEAD_DIM:] = jnp.ones((SEQ, HEAD_DIM), BF16)

    @pl.when(pl.program_id(1) == 0)
    def _():
        r = lax.broadcasted_iota(jnp.int32, (BAND_TQ, BAND_TK), 0) // CHUNK
        jc = lax.broadcasted_iota(jnp.int32, (BAND_TQ, BAND_TK), 1) // CHUNK
        allowed = (jc >= r) & (jc <= r + BAND_ROWS // CHUNK)
        bias_ref[...] = jnp.where(allowed, _toeplitz_bias(f_ref[0], BAND_TQ)[:, :BAND_TK], NEG)

    vcat_ref[:, 0:HEAD_DIM] = v_ref[0]

    def window(qi):
        end = (qi + 1) * BAND_TQ
        start = max(end - BAND_TK, 0)
        return start, end - start

    def scores(qi, dst_ref):
        k0, w = window(qi)
        dst_ref[:, :w] = lax.dot_general(q_ref[0, qi * BAND_TQ:(qi + 1) * BAND_TQ, :],
                                         k_ref[0, k0:k0 + w, :], _NT, preferred_element_type=F32)

    def softmax(qi, cur_ref, p_dst):
        _, w = window(qi)
        for u in range(BAND_TQ // BAND_SUB):
            rows = slice(u * BAND_SUB, (u + 1) * BAND_SUB)
            s = cur_ref[rows, :w] + bias_ref[rows, BAND_TK - w:]
            p_dst[rows, :w] = jnp.exp2(s - jnp.max(s, axis=-1, keepdims=True)).astype(BF16)

    def values(qi, p_src):
        k0, w = window(qi)
        pv = jnp.dot(p_src[:, :w], vcat_ref[k0:k0 + w, :], preferred_element_type=F32)
        o_ref[qi * BAND_TQ:(qi + 1) * BAND_TQ, :] = (
            pv[:, :HEAD_DIM] / pv[:, HEAD_DIM:]).astype(BF16)

    s_bufs = (s_ref, s2_ref)
    p_bufs = (p_ref, p2_ref)
    scores(0, s_bufs[0])
    scores(1, s_bufs[1])
    softmax(0, s_bufs[0], p_bufs[0])
    for t in range(BAND_NQ):
        if t + 2 < BAND_NQ:
            scores(t + 2, s_bufs[t % 2])
        if t + 1 < BAND_NQ:
            softmax(t + 1, s_bufs[(t + 1) % 2], p_bufs[(t + 1) % 2])
        values(t, p_bufs[t % 2])


def _weight_cast_specs(weights, step):
    n = HEADS * BATCH

    def in_spec(w, layer):
        return pl.BlockSpec((None, w.shape[1] // n, w.shape[2]), lambda *g: (layer, step(*g), 0))

    in_specs = [in_spec(w, layer) for w, layer in weights]
    out_specs = [pl.BlockSpec((w.shape[1] // n, w.shape[2]), lambda *g: (step(*g), 0))
                 for w, _ in weights]
    out_shape = [jax.ShapeDtypeStruct(w.shape[1:], BF16) for w, _ in weights]
    return in_specs, out_specs, out_shape


def band_prompt(qkv_hm, f_rows, weights):
    w_in, w_out, w_shape = _weight_cast_specs(weights, lambda h, b: h * BATCH + b)
    return pl.pallas_call(
        _band_prompt_kernel,
        grid=(HEADS, BATCH),
        in_specs=[
            pl.BlockSpec((1, SEQ, HEAD_DIM), lambda h, b: (h, b, 0)),
            pl.BlockSpec((1, SEQ, HEAD_DIM), lambda h, b: (HEADS + h, b, 0)),
            pl.BlockSpec((1, SEQ, HEAD_DIM), lambda h, b: (2 * HEADS + h, b, 0)),
            pl.BlockSpec((1, 1, BAND_F), lambda h, b: (h, 0, 0)),
        ] + w_in,
        out_specs=[pl.BlockSpec((SEQ, HEAD_DIM), lambda h, b: (b, h))] + w_out,
        out_shape=[jax.ShapeDtypeStruct((T_PROMPT, D_MODEL), BF16)] + w_shape,
        scratch_shapes=[
            pltpu.VMEM((SEQ, 2 * HEAD_DIM), BF16),
            pltpu.VMEM((BAND_TQ, BAND_TK), F32),
            pltpu.VMEM((BAND_TQ, BAND_TK), F32),
            pltpu.VMEM((BAND_TQ, BAND_TK), F32),
            pltpu.VMEM((BAND_TQ, BAND_TK), BF16),
            pltpu.VMEM((BAND_TQ, BAND_TK), BF16),
        ],
        compiler_params=_params("arbitrary", "arbitrary"),
        name="band_prompt",
    )(qkv_hm, qkv_hm, qkv_hm, f_rows, *[w for w, _ in weights])


def _band_sample_kernel(q_ref, kn_ref, vn_ref, ck_ref, cv_ref, f_ref, o_ref):
    for h in range(HEADS):
        cols = slice(h * HEAD_DIM, (h + 1) * HEAD_DIM)
        bias = _toeplitz_bias(f_ref[h], DEC_SEQ)
        q = q_ref[h]
        kc = ck_ref[0, pl.ds(h, BAND_ROWS, stride=HEADS), :].astype(BF16)
        vc = cv_ref[0, pl.ds(h, BAND_ROWS, stride=HEADS), :].astype(BF16)
        s_c = lax.dot_general(q, kc, _NT, preferred_element_type=F32) + bias[:, :BAND_ROWS]
        s_n = (lax.dot_general(q, kn_ref[h], _NT, preferred_element_type=F32)
               + bias[:, BAND_ROWS:BAND_ROWS + DEC_SEQ])
        m = jnp.maximum(jnp.max(s_c, axis=-1, keepdims=True), jnp.max(s_n, axis=-1, keepdims=True))
        p_c = jnp.exp2(s_c - m)
        p_n = jnp.exp2(s_n - m)
        l = jnp.sum(p_c, axis=-1, keepdims=True) + jnp.sum(p_n, axis=-1, keepdims=True)
        o = (jnp.dot(p_c.astype(BF16), vc, preferred_element_type=F32)
             + jnp.dot(p_n.astype(BF16), vn_ref[h], preferred_element_type=F32)) / l
        o_ref[:, cols] = o.astype(BF16)


def band_sample(qkv_hm, cache_k, cache_v, f_rows):
    return pl.pallas_call(
        _band_sample_kernel,
        grid=(DEC_BATCH,),
        in_specs=[
            pl.BlockSpec((HEADS, DEC_SEQ, HEAD_DIM), lambda s: (0, s, 0)),
            pl.BlockSpec((HEADS, DEC_SEQ, HEAD_DIM), lambda s: (1, s, 0)),
            pl.BlockSpec((HEADS, DEC_SEQ, HEAD_DIM), lambda s: (2, s, 0)),
            pl.BlockSpec((1, BAND_ROWS * HEADS, HEAD_DIM), lambda s: (s, 0, 0)),
            pl.BlockSpec((1, BAND_ROWS * HEADS, HEAD_DIM), lambda s: (s, 0, 0)),
            pl.BlockSpec((HEADS, 1, BAND_F), lambda s: (0, 0, 0)),
        ],
        out_specs=pl.BlockSpec((DEC_SEQ, D_MODEL), lambda s: (s, 0)),
        out_shape=jax.ShapeDtypeStruct((T_SAMPLE, D_MODEL), BF16),
        compiler_params=_params("arbitrary"),
        name="band_sample",
    )(qkv_hm, qkv_hm, qkv_hm, cache_k, cache_v, f_rows)


def _proj_residual_kernel(a_ref, w_ref, g_ref, gn_ref, x_ref, o_ref, h_ref):
    w = w_ref[...]
    g = g_ref[...]
    gn = gn_ref[...]
    for u in range(a_ref.shape[0] // PROJ_SUB):
        rows = slice(u * PROJ_SUB, (u + 1) * PROJ_SUB)
        y = jnp.dot(a_ref[rows, :], w, preferred_element_type=F32)
        x1 = x_ref[rows, :] + _rms(y, g)
        o_ref[rows, :] = x1
        h_ref[rows, :] = _rms(x1, gn).astype(BF16)


def proj_residual(a, w, g, g_next, x, *, tm):
    m, k = a.shape
    n = w.shape[1]
    return pl.pallas_call(
        _proj_residual_kernel,
        grid=(m // tm,),
        in_specs=[
            pl.BlockSpec((tm, k), lambda i: (i, 0)),
            pl.BlockSpec((k, n), lambda i: (0, 0)),
            pl.BlockSpec((1, n), lambda i: (0, 0)),
            pl.BlockSpec((1, n), lambda i: (0, 0)),
            pl.BlockSpec((tm, n), lambda i: (i, 0)),
        ],
        out_specs=[
            pl.BlockSpec((tm, n), lambda i: (i, 0)),
            pl.BlockSpec((tm, n), lambda i: (i, 0)),
        ],
        out_shape=[
            jax.ShapeDtypeStruct((m, n), F32),
            jax.ShapeDtypeStruct((m, n), BF16),
        ],
        compiler_params=_params("arbitrary"),
        name="proj_residual",
    )(a, w, g.reshape(1, n), g_next.reshape(1, n), x)


def _ffn_kernel(x_ref, h_ref, w1a_ref, w1b_ref, w2a_ref, w2b_ref, g2_ref, o_ref, ssq_ref):
    j = pl.program_id(1)

    @pl.when(j == 0)
    def _():
        o_ref[...] = jnp.zeros(o_ref.shape, F32)

    def hidden(w1_ref):
        a = jnp.maximum(jnp.dot(h_ref[...], w1_ref[...], preferred_element_type=F32), 0.0)
        return (a * a).astype(BF16)

    aa = hidden(w1a_ref)
    ab = hidden(w1b_ref)
    for c in range(o_ref.shape[1] // FFN_CHUNK):
        cols = slice(c * FFN_CHUNK, (c + 1) * FFN_CHUNK)
        y = (o_ref[:, cols] + jnp.dot(aa, w2a_ref[:, cols], preferred_element_type=F32)
             + jnp.dot(ab, w2b_ref[:, cols], preferred_element_type=F32))
        o_ref[:, cols] = y
        sq = y * y
        ssq_ref[:, c * 128:(c + 1) * 128] = functools.reduce(
            jnp.add, [sq[:, b * 128:(b + 1) * 128] for b in range(FFN_CHUNK // 128)])

    @pl.when(j == pl.num_programs(1) - 1)
    def _():
        ms = jnp.sum(ssq_ref[...], axis=-1, keepdims=True) * (1.0 / o_ref.shape[1])
        o_ref[...] = x_ref[...] + o_ref[...] * lax.rsqrt(ms + EPS) * g2_ref[...]


def ffn(x, h, w1, w2, g2, *, tm, tf):
    m, d = x.shape
    f = w1.shape[1]
    return pl.pallas_call(
        _ffn_kernel,
        grid=(m // tm, f // (2 * tf)),
        in_specs=[
            pl.BlockSpec((tm, d), lambda i, j: (i, 0)),
            pl.BlockSpec((tm, d), lambda i, j: (i, 0)),
            pl.BlockSpec((d, tf), lambda i, j: (0, 2 * j)),
            pl.BlockSpec((d, tf), lambda i, j: (0, 2 * j + 1)),
            pl.BlockSpec((tf, d), lambda i, j: (2 * j, 0)),
            pl.BlockSpec((tf, d), lambda i, j: (2 * j + 1, 0)),
            pl.BlockSpec((1, d), lambda i, j: (0, 0)),
        ],
        out_specs=pl.BlockSpec((tm, d), lambda i, j: (i, 0)),
        out_shape=jax.ShapeDtypeStruct((m, d), F32),
        scratch_shapes=[pltpu.VMEM((tm, 128 * (d // FFN_CHUNK)), F32)],
        compiler_params=pltpu.CompilerParams(dimension_semantics=("arbitrary", "arbitrary"),
                                             vmem_limit_bytes=FFN_VMEM_LIMIT_BYTES),
        name="ffn",
    )(x, h, w1, w1, w2, w2, g2.reshape(1, d))


def _rope128(x, cos, sin):
    lane = lax.broadcasted_iota(jnp.int32, x.shape, 1)
    half = MLA_ROPE // 2
    swapped = jnp.where(lane < half, pltpu.roll(x, 128 - half, 1), pltpu.roll(x, half, 1))
    return x * cos + swapped * sin


def _mla_proj_kernel(x_ref, g_ref, wd_ref, qn_ref, kvn_ref, wq_ref, wuk_ref, wuv_ref, cos_ref, sin_ref,
                     ckv_ref, kr_ref, q_ref, *kv_refs):
    for u in range(x_ref.shape[0] // MLA_PROJ_SUB):
        rows = slice(u * MLA_PROJ_SUB, (u + 1) * MLA_PROJ_SUB)
        d = jnp.dot(_rms(x_ref[rows, :], g_ref[...]).astype(BF16), wd_ref[...],
                    preferred_element_type=F32)
        cos = cos_ref[rows, :]
        sin = sin_ref[rows, :]
        cq = _rms(d[:, :MLA_RANK], qn_ref[...]).astype(BF16)
        ckv = _rms(d[:, MLA_RANK:2 * MLA_RANK], kvn_ref[...])
        ckv_ref[rows, :] = ckv
        kr = _rope128(d[:, 2 * MLA_RANK:], cos, sin)
        kr_ref[rows, :] = kr[:, :MLA_ROPE]

        q = jnp.dot(cq, wq_ref[...], preferred_element_type=F32) * MLA_Q_SCALE
        for h in range(HEADS):
            base = h * MLA_QK
            q_ref[h, rows, 0:MLA_NOPE] = q[:, base:base + MLA_NOPE].astype(BF16)
            q_ref[h, rows, MLA_NOPE:] = _rope128(
                q[:, base + MLA_NOPE:base + MLA_QK], cos, sin).astype(BF16)

        if kv_refs:
            k_ref, v_ref = kv_refs
            kr_b = kr.astype(BF16)
            ckv_b = ckv.astype(BF16)
            kn = jnp.dot(ckv_b, wuk_ref[...], preferred_element_type=F32)
            vv = jnp.dot(ckv_b, wuv_ref[...], preferred_element_type=F32)
            ones = jnp.ones((MLA_PROJ_SUB, HEAD_DIM), BF16)
            for h in range(HEADS):
                k_ref[h, rows, 0:MLA_NOPE] = kn[:, h * MLA_NOPE:(h + 1) * MLA_NOPE].astype(BF16)
                k_ref[h, rows, MLA_NOPE:] = kr_b
                v_ref[h, rows, 0:HEAD_DIM] = vv[:, h * HEAD_DIM:(h + 1) * HEAD_DIM].astype(BF16)
                v_ref[h, rows, HEAD_DIM:] = ones


def mla_proj(x, g, w_d, q_norm, kv_norm, wq, wuk, wuv, cos, sin, *, tm, with_kv):
    m = x.shape[0]
    full = lambda i: (0, 0)
    rope_tiles = cos.shape[0] // tm
    hm_spec = pl.BlockSpec((HEADS, tm, MLA_QK), lambda i: (0, i, 0))
    hm_shape = jax.ShapeDtypeStruct((HEADS, m, MLA_QK), BF16)
    n_hm = 3 if with_kv else 1
    return pl.pallas_call(
        _mla_proj_kernel,
        grid=(m // tm,),
        in_specs=[
            pl.BlockSpec((tm, D_MODEL), lambda i: (i, 0)),
            pl.BlockSpec((1, D_MODEL), full),
            pl.BlockSpec(w_d.shape, full),
            pl.BlockSpec((1, MLA_RANK), full),
            pl.BlockSpec((1, MLA_RANK), full),
            pl.BlockSpec(wq.shape, full),
            pl.BlockSpec(wuk.shape, full),
            pl.BlockSpec(wuv.shape, full),
            pl.BlockSpec((tm, 128), lambda i: (i % rope_tiles, 0)),
            pl.BlockSpec((tm, 128), lambda i: (i % rope_tiles, 0)),
        ],
        out_specs=[
            pl.BlockSpec((tm, MLA_RANK), lambda i: (i, 0)),
            pl.BlockSpec((tm, MLA_ROPE), lambda i: (i, 0)),
        ] + [hm_spec] * n_hm,
        out_shape=[
            jax.ShapeDtypeStruct((m, MLA_RANK), F32),
            jax.ShapeDtypeStruct((m, MLA_ROPE), F32),
        ] + [hm_shape] * n_hm,
        compiler_params=_params("arbitrary"),
        name="mla_proj",
    )(x, g.reshape(1, -1), w_d, q_norm.reshape(1, -1), kv_norm.reshape(1, -1), wq, wuk, wuv, cos, sin)


def _mla_prompt_kernel(q_ref, k_ref, v_ref, w1f_ref, w2f_ref, wof_ref, o_ref, w1b_ref, w2b_ref, wob_ref,
                       s_ref, s2_ref, p_ref, p2_ref, alpha_ref, alpha2_ref, bias_ref, m_ref, acc_ref):
    w1b_ref[...] = w1f_ref[...].astype(BF16)
    w2b_ref[...] = w2f_ref[...].astype(BF16)
    wob_ref[...] = wof_ref[...].astype(BF16)

    @pl.when((pl.program_id(0) == 0) & (pl.program_id(1) == 0))
    def _():
        rc = lax.broadcasted_iota(jnp.int32, (MLA_TQ, MLA_TK), 0) // CHUNK
        cc = lax.broadcasted_iota(jnp.int32, (MLA_TQ, MLA_TK), 1) // CHUNK
        bias_ref[...] = jnp.where(cc <= rc, 0.0, NEG)

    def is_last(qi, k0, klen):
        return k0 + klen == (qi + 1) * MLA_TQ

    def width(qi, k0, klen, row0):
        return klen - MLA_HALF if is_last(qi, k0, klen) and row0 < MLA_HALF else klen

    def scores(qi, k0, klen, dst_ref):
        for row0 in range(0, MLA_TQ, MLA_HALF):
            w = width(qi, k0, klen, row0)
            q = q_ref[0, qi * MLA_TQ + row0:qi * MLA_TQ + row0 + MLA_HALF, :]
            dst_ref[row0:row0 + MLA_HALF, :w] = lax.dot_general(
                q, k_ref[0, k0:k0 + w, :], _NT, preferred_element_type=F32)

    def softmax(qi, k0, klen, cur_ref, p_dst, alpha_dst):
        first = k0 == 0
        last = is_last(qi, k0, klen)
        for u in range(MLA_TQ // MLA_SUB):
            rows = slice(u * MLA_SUB, (u + 1) * MLA_SUB)
            if last:
                w = width(qi, k0, klen, u * MLA_SUB)
                d0 = klen - MLA_TK
                parts = [cur_ref[rows, d0:w] + bias_ref[rows, :w - d0]]
                if d0 > 0:
                    parts.insert(0, cur_ref[rows, :d0])
            else:
                parts = [cur_ref[rows, :klen]]
            m_cur = functools.reduce(jnp.maximum, [jnp.max(x, axis=-1, keepdims=True) for x in parts])
            if first:
                m_new = jnp.broadcast_to(m_cur, (MLA_SUB, 128))
            else:
                m_prev = m_ref[rows, :]
                m_new = jnp.maximum(m_prev, m_cur)
                alpha_dst[rows, :] = jnp.exp2(m_prev - m_new)
            if not last:
                m_ref[rows, :] = m_new
            col = 0
            for x in parts:
                p_dst[rows, col:col + x.shape[1]] = jnp.exp2(
                    x - jnp.tile(m_new, (1, x.shape[1] // 128))).astype(BF16)
                col += x.shape[1]

    def values(qi, k0, klen, p_src, alpha_src):
        first = k0 == 0
        last = is_last(qi, k0, klen)
        for u in range(MLA_TQ // MLA_HALF):
            rows = slice(u * MLA_HALF, (u + 1) * MLA_HALF)
            w = width(qi, k0, klen, u * MLA_HALF)
            acc = jnp.dot(p_src[rows, :w], v_ref[0, k0:k0 + w, :], preferred_element_type=F32)
            if not first:
                acc = acc + jnp.tile(alpha_src[rows, :], (1, 2)) * acc_ref[rows, :]
            if last:
                o_ref[qi * MLA_TQ + u * MLA_HALF:qi * MLA_TQ + (u + 1) * MLA_HALF, :] = (
                    acc[:, :HEAD_DIM] / acc[:, HEAD_DIM:]).astype(BF16)
            else:
                acc_ref[rows, :] = acc

    blocks = [(qi, k0, min(MLA_TK_MAX, (qi + 1) * MLA_TQ - k0))
              for qi in range(MLA_NQ) for k0 in range(0, (qi + 1) * MLA_TQ, MLA_TK_MAX)]
    n = len(blocks)
    s_bufs = (s_ref, s2_ref)
    p_bufs = (p_ref, p2_ref)
    a_bufs = (alpha_ref, alpha2_ref)
    scores(*blocks[0], s_bufs[0])
    scores(*blocks[1], s_bufs[1])
    softmax(*blocks[0], s_bufs[0], p_bufs[0], a_bufs[0])
    for t in range(n):
        if t + 2 < n:
            scores(*blocks[t + 2], s_bufs[t % 2])
        if t + 1 < n:
            softmax(*blocks[t + 1], s_bufs[(t + 1) % 2], p_bufs[(t + 1) % 2], a_bufs[(t + 1) % 2])
        values(*blocks[t], p_bufs[t % 2], a_bufs[t % 2])


def mla_prompt(q, k, v, weights):
    w_in, w_out, w_shape = _weight_cast_specs(weights, lambda b, h: b * HEADS + h)
    return pl.pallas_call(
        _mla_prompt_kernel,
        grid=(BATCH, HEADS),
        in_specs=[
            pl.BlockSpec((1, SEQ, MLA_QK), lambda b, h: (h, b, 0)),
            pl.BlockSpec((1, SEQ, MLA_QK), lambda b, h: (h, b, 0)),
            pl.BlockSpec((1, SEQ, 2 * HEAD_DIM), lambda b, h: (h, b, 0)),
        ] + w_in,
        out_specs=[pl.BlockSpec((SEQ, HEAD_DIM), lambda b, h: (b, h))] + w_out,
        out_shape=[jax.ShapeDtypeStruct((T_PROMPT, D_MODEL), BF16)] + w_shape,
        scratch_shapes=[
            pltpu.VMEM((MLA_TQ, MLA_TK_MAX), F32),
            pltpu.VMEM((MLA_TQ, MLA_TK_MAX), F32),
            pltpu.VMEM((MLA_TQ, MLA_TK_MAX), BF16),
            pltpu.VMEM((MLA_TQ, MLA_TK_MAX), BF16),
            pltpu.VMEM((MLA_TQ, 128), F32),
            pltpu.VMEM((MLA_TQ, 128), F32),
            pltpu.VMEM((MLA_TQ, MLA_TK), F32),
            pltpu.VMEM((MLA_TQ, 128), F32),
            pltpu.VMEM((MLA_TQ, 2 * HEAD_DIM), F32),
        ],
        compiler_params=_params("arbitrary", "arbitrary"),
        name="mla_prompt",
    )(q, k, v, *[w for w, _ in weights])


def _mla_sample_kernel(q_ref, cn_ref, rn_ref, cc_ref, cr_ref, wuk_ref, wuv_ref, o_ref,
                       ql_ref, qr_ref):
    cc = cc_ref[0].astype(BF16)
    cr_t = cr_ref[0].astype(BF16)
    cn = cn_ref[...].astype(BF16)
    rn = rn_ref[...].astype(BF16)
    for g in range(HEADS // MLA_SAMPLE_HEADS):
        heads = range(g * MLA_SAMPLE_HEADS, (g + 1) * MLA_SAMPLE_HEADS)
        grp = slice(g * MLA_SAMPLE_HEADS * DEC_SEQ, (g + 1) * MLA_SAMPLE_HEADS * DEC_SEQ)
        for h in heads:
            rows = slice(h * DEC_SEQ, (h + 1) * DEC_SEQ)
            ql_ref[rows, :] = lax.dot_general(
                q_ref[h, :, 0:MLA_NOPE], wuk_ref[:, h * MLA_NOPE:(h + 1) * MLA_NOPE], _NT,
                preferred_element_type=F32).astype(BF16)
            qr_ref[rows, :] = q_ref[h, :, MLA_NOPE:]
        ql = ql_ref[grp, :]
        qr = qr_ref[grp, :][:, :MLA_ROPE]
        s_c = (lax.dot_general(ql, cc, _NT, preferred_element_type=F32)
               + jnp.dot(qr, cr_t, preferred_element_type=F32))
        s_n = (lax.dot_general(ql, cn, _NT, preferred_element_type=F32)
               + lax.dot_general(qr, rn, _NT, preferred_element_type=F32))
        m = jnp.maximum(jnp.max(s_c, axis=-1, keepdims=True), jnp.max(s_n, axis=-1, keepdims=True))
        p_c = jnp.exp2(s_c - m)
        p_n = jnp.exp2(s_n - m)
        l = jnp.sum(p_c, axis=-1, keepdims=True) + jnp.sum(p_n, axis=-1, keepdims=True)
        o_lat = ((jnp.dot(p_c.astype(BF16), cc, preferred_element_type=F32)
                  + jnp.dot(p_n.astype(BF16), cn, preferred_element_type=F32)) / l).astype(BF16)
        for i, h in enumerate(heads):
            cols = slice(h * HEAD_DIM, (h + 1) * HEAD_DIM)
            o_ref[:, cols] = jnp.dot(o_lat[i * DEC_SEQ:(i + 1) * DEC_SEQ], wuv_ref[:, cols],
                                     preferred_element_type=F32).astype(BF16)


def mla_sample(q, ckv, kr, cache_ckv, cache_kr, wuk, wuv):
    full = lambda s: (0, 0)
    return pl.pallas_call(
        _mla_sample_kernel,
        grid=(DEC_BATCH,),
        in_specs=[
            pl.BlockSpec((HEADS, DEC_SEQ, MLA_QK), lambda s: (0, s, 0)),
            pl.BlockSpec((DEC_SEQ, MLA_RANK), lambda s: (s, 0)),
            pl.BlockSpec((DEC_SEQ, MLA_ROPE), lambda s: (s, 0)),
            pl.BlockSpec((1, PAST_LEN, MLA_RANK), lambda s: (s, 0, 0)),
            pl.BlockSpec((1, MLA_ROPE, PAST_LEN), lambda s: (s, 0, 0)),
            pl.BlockSpec(wuk.shape, full),
            pl.BlockSpec(wuv.shape, full),
        ],
        out_specs=pl.BlockSpec((DEC_SEQ, D_MODEL), lambda s: (s, 0)),
        out_shape=jax.ShapeDtypeStruct((T_SAMPLE, D_MODEL), BF16),
        scratch_shapes=[
            pltpu.VMEM((HEADS * DEC_SEQ, MLA_RANK), BF16),
            pltpu.VMEM((HEADS * DEC_SEQ, 128), BF16),
        ],
        compiler_params=_params("arbitrary"),
        name="mla_sample",
    )(q, ckv, kr, cache_ckv, cache_kr, wuk, wuv)


def _bias_rows(rel_bias):
    far = jnp.broadcast_to(rel_bias[:, 2 * REL_CLIP:], (HEADS, BAND_F // 4))
    mid = rel_bias[:, 1:][:, ::-1]
    return (jnp.concatenate([far, mid, far], axis=1) * LOG2E).reshape(HEADS, 1, BAND_F)


def _rope_tables(pos):
    half = MLA_ROPE // 2
    inv = 1.0 / (ROPE_THETA ** (jnp.arange(half, dtype=F32) / half))
    ang = pos.astype(F32)[:, None] * inv[None, :]
    c, s = jnp.cos(ang), jnp.sin(ang)
    z = jnp.zeros((pos.shape[0], 128 - MLA_ROPE), F32)
    return jnp.concatenate([c, c, z], axis=1), jnp.concatenate([-s, s, z], axis=1)


PROMPT_TM_QKV = 1024
MLA_PROJ_TM = 256
MLA_PROJ_SUB = 256


def kernel(x_prompt, x_sample, cache_a_k, cache_a_v, cache_mla_ckv, cache_mla_kr, ln_mix_pre, ln_mix_post, ln_ffn_pre, ln_ffn_post, a_w_qkv, a_w_o, a_rel_bias, mla_w_dq, mla_q_norm, mla_w_uq, mla_w_dkv, mla_kv_norm, mla_w_uk, mla_w_uv, mla_w_o, ffn_w1, ffn_w2):
    xp = x_prompt.reshape(T_PROMPT, D_MODEL)
    xs = x_sample.reshape(T_SAMPLE, D_MODEL)
    def mix_out_and_ffn(i, o, w_o, x, w1, w2):
        x, h = proj_residual(o, w_o, ln_mix_post[i], ln_ffn_pre[i], x, tm=512)
        return ffn(x, h, w1, w2, ln_ffn_post[i], tm=512, tf=1024)

    w_qkv = a_w_qkv[0].astype(BF16)
    f_rows = _bias_rows(a_rel_bias[0])
    hm_p, k_p, v_p = norm_qkv(xp, ln_mix_pre[0], w_qkv, tm=PROMPT_TM_QKV, keep=BAND_ROWS,
                              tiles_per_keep=SEQ // PROMPT_TM_QKV)
    hm_s, k_s, v_s = norm_qkv(xs, ln_mix_pre[0], w_qkv, tm=512, keep=512, tiles_per_keep=1)
    o_p, w1, w2, w_o = band_prompt(hm_p, f_rows, ((ffn_w1, 0), (ffn_w2, 0), (a_w_o, 0)))
    o_s = band_sample(hm_s,
                      cache_a_k[0].reshape(DEC_BATCH, BAND_ROWS * HEADS, HEAD_DIM),
                      cache_a_v[0].reshape(DEC_BATCH, BAND_ROWS * HEADS, HEAD_DIM), f_rows)
    xp = mix_out_and_ffn(0, o_p, w_o, xp, w1, w2)
    xs = mix_out_and_ffn(0, o_s, w_o, xs, w1, w2)

    pad = jnp.zeros((D_MODEL, 128 - MLA_ROPE), F32)
    w_d = jnp.concatenate([mla_w_dq[0], mla_w_dkv[0], pad], axis=1).astype(BF16)
    wq = jnp.pad(mla_w_uq[0], ((0, 0), (0, 0), (0, MLA_QK - MLA_NOPE - MLA_ROPE)))
    wq = wq.reshape(MLA_RANK, HEADS * MLA_QK).astype(BF16)
    wuk = mla_w_uk[0].reshape(MLA_RANK, HEADS * MLA_NOPE).astype(BF16)
    wuv = mla_w_uv[0].reshape(MLA_RANK, HEADS * HEAD_DIM).astype(BF16)
    cos_p, sin_p = _rope_tables(jnp.arange(SEQ))
    cos_s, sin_s = _rope_tables(jnp.tile(PAST_LEN + jnp.arange(DEC_SEQ), MLA_PROJ_TM // DEC_SEQ))

    ckv_p, kr_p, q_p, kk_p, vv_p = mla_proj(xp, ln_mix_pre[1], w_d, mla_q_norm[0], mla_kv_norm[0],
                                            wq, wuk, wuv, cos_p, sin_p, tm=MLA_PROJ_TM, with_kv=True)
    ckv_s, kr_s, q_s = mla_proj(xs, ln_mix_pre[1], w_d, mla_q_norm[0], mla_kv_norm[0],
                                wq, wuk, wuv, cos_s, sin_s, tm=MLA_PROJ_TM, with_kv=False)
    o_p, w1, w2, w_o = mla_prompt(q_p, kk_p, vv_p, ((ffn_w1, 1), (ffn_w2, 1), (mla_w_o, 0)))
    o_s = mla_sample(q_s, ckv_s, kr_s, cache_mla_ckv[0],
                     jnp.swapaxes(cache_mla_kr[0], 1, 2), wuk, wuv)
    xp = mix_out_and_ffn(1, o_p, w_o, xp, w1, w2)
    xs = mix_out_and_ffn(1, o_s, w_o, xs, w1, w2)

    return (
        xp.reshape(BATCH, SEQ, D_MODEL),
        xs.reshape(DEC_BATCH, DEC_SEQ, D_MODEL),
        k_p.reshape(1, BATCH, BAND_ROWS, HEADS, HEAD_DIM),
        v_p.reshape(1, BATCH, BAND_ROWS, HEADS, HEAD_DIM),
        k_s.reshape(1, DEC_BATCH, DEC_SEQ, HEADS, HEAD_DIM),
        v_s.reshape(1, DEC_BATCH, DEC_SEQ, HEADS, HEAD_DIM),
        ckv_p.reshape(1, BATCH, SEQ, MLA_RANK),
        kr_p.reshape(1, BATCH, SEQ, MLA_ROPE),
        ckv_s.reshape(1, DEC_BATCH, DEC_SEQ, MLA_RANK),
        kr_s.reshape(1, DEC_BATCH, DEC_SEQ, MLA_ROPE),
    )
```
